```python
import math
import jax, jax.numpy as jnp
from jax import lax
import numpy as np

D_MODEL = 1024
BATCH = 32
SEQ = 256
DEPTH = 1
DEC_BATCH = 8
DEC_SEQ = 1024
PAST_LEN = 512

GRID_W = 64
D_CONV = 512
CONV_WIDTH = 31
N_HEADS_NA = 8
HEAD_DIM = 64
D_ATTN = N_HEADS_NA * HEAD_DIM
NA_ROWS_MAX = 8
NA_COLS = 16
N_EXPERTS = 16
D_EXPERT = 2048
EC_CAPACITY = 2
N_MOD = 6
EPS = 1e-6
D_IN = 2 * D_CONV + 3 * D_ATTN + 2 * D_MODEL

kernel_name = "hybrid_conformer_natten_ec_diffusion_step"


def rms_norm(x, g):
    xf = x.astype(jnp.float32)
    xf = xf * lax.rsqrt(jnp.mean(xf * xf, axis=-1, keepdims=True) + EPS)
    return (xf * g.astype(jnp.float32)).astype(x.dtype)


def layer_norm(x, g, b):
    xf = x.astype(jnp.float32)
    mu = jnp.mean(xf, axis=-1, keepdims=True)
    var = jnp.mean(jnp.square(xf - mu), axis=-1, keepdims=True)
    y = (xf - mu) * lax.rsqrt(var + EPS)
    return (y * g.astype(jnp.float32) + b.astype(jnp.float32)).astype(x.dtype)


def ada_modulation(cond, w_ada, b_ada):
    m = jax.nn.silu(cond) @ w_ada + b_ada
    return jnp.split(m[:, None, :], N_MOD, axis=-1)


def modulate(h, shift, scale):
    return h * (1 + scale) + shift


def combined_projection(h, w_in):
    B, T, _ = h.shape
    u = h @ w_in
    conv_in = u[..., :2 * D_CONV]
    qkv = u[..., 2 * D_CONV:2 * D_CONV + 3 * D_ATTN].reshape(B, T, 3, N_HEADS_NA, HEAD_DIM)
    gates = jax.nn.sigmoid(u[..., 2 * D_CONV + 3 * D_ATTN:])
    return conv_in, qkv[:, :, 0], qkv[:, :, 1], qkv[:, :, 2], gates


def conformer_conv_branch(u, dw_w, dw_b, ln_g, ln_b, w_pw):
    val, gate = jnp.split(u, 2, axis=-1)
    v = val * jax.nn.sigmoid(gate)
    pad = CONV_WIDTH // 2
    v = lax.conv_general_dilated(
        v, dw_w[:, None, :].astype(v.dtype), window_strides=(1,), padding=[(pad, pad)],
        dimension_numbers=('NWC', 'WIO', 'NWC'), feature_group_count=D_CONV) + dw_b
    v = jax.nn.silu(layer_norm(v, ln_g, ln_b))
    return v @ w_pw


def context_attention(q, k, v):
    s = jnp.einsum('bqhd,bkhd->bhqk', q, k).astype(jnp.float32) * (HEAD_DIM ** -0.5)
    p = jax.nn.softmax(s, axis=-1).astype(v.dtype)
    return jnp.einsum('bhqk,bkhd->bqhd', p, v)


def neighbourhood_attention(q, k, v, k_ctx, v_ctx, rpb):
    B, T, H, dh = q.shape
    rows = T // GRID_W
    kh = min(NA_ROWS_MAX, rows)
    kw = NA_COLS
    n_cb = GRID_W // kw
    band_w = 2 * kw
    L = kh * band_w
    r = np.arange(rows)
    row_start = np.clip(r - kh // 2, 0, rows - kh)
    cb0 = np.arange(n_cb) * kw
    band_start = np.clip(cb0 - kw // 2, 0, GRID_W - band_w)
    key_rows = row_start[:, None] + np.arange(kh)
    key_cols = band_start[:, None] + np.arange(band_w)
    key_idx = (key_rows[:, None, :, None] * GRID_W
               + key_cols[None, :, None, :]).reshape(rows, n_cb, L)
    q_cols = cb0[:, None] + np.arange(kw)
    col_start = np.clip(q_cols - kw // 2, 0, GRID_W - kw)
    kc = np.tile(key_cols, (1, kh))
    mask = ((kc[:, None, :] >= col_start[:, :, None])
            & (kc[:, None, :] < col_start[:, :, None] + kw))
    dr = np.repeat(key_rows - r[:, None], band_w, axis=1) + NA_ROWS_MAX - 1
    dc = np.clip(kc[:, None, :] - q_cols[:, :, None], -(kw - 1), kw - 1) + kw - 1
    bias = rpb[:, dr[:, None, None, :], dc[None, :, :, :]]
    bias = jnp.transpose(bias, (1, 2, 0, 3, 4)).astype(jnp.float32)

    kg = k[:, key_idx]
    vg = v[:, key_idx]
    qb = q.reshape(B, rows, n_cb, kw, H, dh)
    scale = HEAD_DIM ** -0.5
    s_loc = jnp.einsum('brnqhd,brnlhd->brnhql', qb, kg).astype(jnp.float32) * scale + bias[None]
    s_loc = jnp.where(mask[None, None, :, None, :, :], s_loc, -jnp.inf)
    s_ctx = jnp.einsum('brnqhd,bphd->brnhqp', qb, k_ctx).astype(jnp.float32) * scale
    p = jax.nn.softmax(jnp.concatenate([s_loc, s_ctx], axis=-1), axis=-1).astype(v.dtype)
    o = (jnp.einsum('brnhql,brnlhd->brnqhd', p[..., :L], vg)
         + jnp.einsum('brnhqp,bphd->brnqhd', p[..., L:], v_ctx))
    return o.reshape(B, T, H, dh)


def merge_branches(conv_out, attn_heads, gates, w_attn_o, w_out):
    B, T, _ = conv_out.shape
    attn_out = attn_heads.reshape(B, T, D_ATTN) @ w_attn_o
    g_a, g_b = jnp.split(gates, 2, axis=-1)
    return (g_a * conv_out + g_b * attn_out) @ w_out


def expert_choice_ffn(h, w_router, w_gate, w_up, w_down):
    B, T, D = h.shape
    cap = EC_CAPACITY * T // N_EXPERTS
    aff = jax.nn.softmax((h @ w_router).astype(jnp.float32), axis=-1)
    g, idx = lax.top_k(jnp.swapaxes(aff, 1, 2), cap)
    xe = jax.vmap(lambda hb, ib: hb[ib])(h, idx)
    hid = (jax.nn.silu(jnp.einsum('becd,edf->becf', xe, w_gate))
           * jnp.einsum('becd,edf->becf', xe, w_up))
    ye = jnp.einsum('becf,efd->becd', hid, w_down) * g[..., None].astype(h.dtype)
    return jax.vmap(lambda yb, ib: jnp.zeros((T, D), yb.dtype).at[ib.reshape(-1)].add(
        yb.reshape(-1, D)))(ye, idx)


def setup_inputs(seed: int = 0) -> dict:
    key = jax.random.key(seed)
    ks = jax.random.split(key, 24)

    def nrm(k, shape, scale):
        return jax.random.normal(k, shape, jnp.float32) * scale

    L = DEPTH
    return {
        'x_prompt': nrm(ks[0], (BATCH, SEQ, D_MODEL), 1.0),
        'x_sample': nrm(ks[1], (DEC_BATCH, DEC_SEQ, D_MODEL), 1.0),
        'cache_ctx_k': nrm(ks[2], (DEC_BATCH, L, PAST_LEN, N_HEADS_NA, HEAD_DIM), 1.0),
        'cache_ctx_v': nrm(ks[3], (DEC_BATCH, L, PAST_LEN, N_HEADS_NA, HEAD_DIM), 1.0),
        'c': nrm(ks[4], (DEC_BATCH, D_MODEL), 1.0),
        'c_ctx': nrm(ks[5], (D_MODEL,), 1.0),
        'w_ada': nrm(ks[6], (L, D_MODEL, N_MOD * D_MODEL), 0.5 * D_MODEL ** -0.5),
        'b_ada': nrm(ks[7], (L, N_MOD * D_MODEL), 0.02),
        'norm1_g': 1.0 + nrm(ks[8], (L, D_MODEL), 0.02),
        'w_in': nrm(ks[9], (L, D_MODEL, D_IN), D_MODEL ** -0.5),
        'conv_dw_w': nrm(ks[10], (L, CONV_WIDTH, D_CONV), CONV_WIDTH ** -0.5),
        'conv_dw_b': nrm(ks[11], (L, D_CONV), 0.02),
        'conv_ln_g': 1.0 + nrm(ks[12], (L, D_CONV), 0.02),
        'conv_ln_b': nrm(ks[13], (L, D_CONV), 0.02),
        'w_conv_pw': nrm(ks[14], (L, D_CONV, D_MODEL), D_CONV ** -0.5),
        'na_rpb': nrm(ks[15], (L, N_HEADS_NA, 2 * NA_ROWS_MAX - 1, 2 * NA_COLS - 1), 0.1),
        'w_attn_o': nrm(ks[16], (L, D_ATTN, D_MODEL), D_ATTN ** -0.5),
        'w_out': nrm(ks[17], (L, D_MODEL, D_MODEL), D_MODEL ** -0.5),
        'norm2_g': 1.0 + nrm(ks[18], (L, D_MODEL), 0.02),
        'w_router': nrm(ks[19], (L, D_MODEL, N_EXPERTS), D_MODEL ** -0.5),
        'w_gate_e': nrm(ks[20], (L, N_EXPERTS, D_MODEL, D_EXPERT), D_MODEL ** -0.5),
        'w_up_e': nrm(ks[21], (L, N_EXPERTS, D_MODEL, D_EXPERT), D_MODEL ** -0.5),
        'w_down_e': nrm(ks[22], (L, N_EXPERTS, D_EXPERT, D_MODEL), D_EXPERT ** -0.5),
        'final_g': 1.0 + nrm(ks[23], (D_MODEL,), 0.02),
    }


def reference(x_prompt, x_sample, cache_ctx_k, cache_ctx_v, c, c_ctx, w_ada, b_ada, norm1_g,
              w_in, conv_dw_w, conv_dw_b, conv_ln_g, conv_ln_b, w_conv_pw, na_rpb, w_attn_o,
              w_out, norm2_g, w_router, w_gate_e, w_up_e, w_down_e, final_g):
    xp = x_prompt
    xs = x_sample
    new_k = []
    new_v = []
    for l in range(DEPTH):
        sh1, sc1, g1, sh2, sc2, g2 = ada_modulation(c_ctx[None, :], w_ada[l], b_ada[l])
        h = modulate(rms_norm(xp, norm1_g[l]), sh1, sc1)
        conv_in, q, k, v, gates = combined_projection(h, w_in[l])
        conv_out = conformer_conv_branch(conv_in, conv_dw_w[l], conv_dw_b[l], conv_ln_g[l],
                                         conv_ln_b[l], w_conv_pw[l])
        attn = context_attention(q, k, v)
        xp = xp + g1 * merge_branches(conv_out, attn, gates, w_attn_o[l], w_out[l])
        h = modulate(rms_norm(xp, norm2_g[l]), sh2, sc2)
        xp = xp + g2 * expert_choice_ffn(h, w_router[l], w_gate_e[l], w_up_e[l], w_down_e[l])
        new_k.append(k)
        new_v.append(v)

        sh1, sc1, g1, sh2, sc2, g2 = ada_modulation(c, w_ada[l], b_ada[l])
        h = modulate(rms_norm(xs, norm1_g[l]), sh1, sc1)
        conv_in, q, k, v, gates = combined_projection(h, w_in[l])
        conv_out = conformer_conv_branch(conv_in, conv_dw_w[l], conv_dw_b[l], conv_ln_g[l],
                                         conv_ln_b[l], w_conv_pw[l])
        attn = neighbourhood_attention(q, k, v, cache_ctx_k[:, l], cache_ctx_v[:, l], na_rpb[l])
        xs = xs + g1 * merge_branches(conv_out, attn, gates, w_attn_o[l], w_out[l])
        h = modulate(rms_norm(xs, norm2_g[l]), sh2, sc2)
        xs = xs + g2 * expert_choice_ffn(h, w_router[l], w_gate_e[l], w_up_e[l], w_down_e[l])

    y_prompt = rms_norm(xp, final_g)
    y_sample = rms_norm(xs, final_g)
    state_ctx_k = jnp.stack(new_k, axis=1)
    state_ctx_v = jnp.stack(new_v, axis=1)
    return (y_prompt, y_sample, state_ctx_k, state_ctx_v)
```

```python
import functools

import jax
import jax.numpy as jnp
from jax import lax
from jax.experimental import pallas as pl
from jax.experimental.pallas import tpu as pltpu

F32 = jnp.float32
BF16 = jnp.bfloat16

D_MODEL = 1024
D_CONV = 512
CONV_WIDTH = 31
N_HEADS = 8
HEAD_DIM = 64
D_ATTN = N_HEADS * HEAD_DIM
GRID_W = 64
NA_ROWS = 8
NA_COLS = 16
N_EXPERTS = 16
D_EXPERT = 2048
EC_CAPACITY = 2
N_MOD = 6
EPS = 1e-6
D_PROJ_A = 2 * D_CONV + 3 * D_ATTN
LANES = 128
REQ_PER_GROUP = LANES // N_EXPERTS
MASK_NEG = -1e30
VMEM_LIMIT = 56 * 1024 * 1024

_NT = (((1,), (1,)), ((), ()))


def _params(*sem):
    return pltpu.CompilerParams(dimension_semantics=sem, vmem_limit_bytes=VMEM_LIMIT)


def _dot(a, b):
    return jnp.dot(a, b, preferred_element_type=F32)


def _sigmoid(x):
    return 1.0 / (1.0 + jnp.exp(-x))


def _split_bf16(x):
    hi = x.astype(BF16)
    lo = (x - hi.astype(F32)).astype(BF16)
    return hi, lo


def _rms_mod(x, g, shift, scale):
    xn = x * lax.rsqrt(jnp.mean(x * x, axis=-1, keepdims=True) + EPS) * g
    return xn * (1.0 + scale) + shift


def _ada_kernel(cond_ref, w_ref, b_ref, o_ref):
    c = cond_ref[...]
    a = c * _sigmoid(c)
    ah, al = _split_bf16(a)
    wh, wl = _split_bf16(w_ref[...])
    o_ref[...] = _dot(ah, wh) + _dot(al, wh) + _dot(ah, wl) + b_ref[...]


def _ada(cond, w_ada, b_ada):
    n = w_ada.shape[1]
    bn = 1024
    return pl.pallas_call(
        _ada_kernel,
        grid=(n // bn,),
        in_specs=[
            pl.BlockSpec(cond.shape, lambda j: (0, 0)),
            pl.BlockSpec((D_MODEL, bn), lambda j: (0, j)),
            pl.BlockSpec((1, bn), lambda j: (0, j)),
        ],
        out_specs=pl.BlockSpec((cond.shape[0], bn), lambda j: (0, j)),
        out_shape=jax.ShapeDtypeStruct((cond.shape[0], n), F32),
        compiler_params=_params("arbitrary"),
        name="ada",
    )(cond, w_ada, b_ada)


CONV_PAD = 16
FRONT_ROWS = 256
CONV_ROWS = 64
SUBLANES = 8


def _front_kernel(x_ref, mods_ref, g1_ref, w_ref, dww_ref, dwb_ref, lng_ref, lnb_ref,
                  q_ref, k_ref, v_ref, c_ref, gpad_ref, *, seq):
    m = mods_ref[0]
    shift, scale = m[0:1], m[1:2]
    zeros = jnp.zeros((CONV_PAD, D_CONV), F32)
    gpad_ref[0:CONV_PAD, :] = zeros
    gpad_ref[CONV_PAD + seq:CONV_PAD + seq + CONV_PAD, :] = zeros

    def proj(i, carry):
        r0 = pl.multiple_of(i * FRONT_ROWS, FRONT_ROWS)
        x = x_ref[0, pl.ds(r0, FRONT_ROWS), :]
        h = _rms_mod(x, g1_ref[...], shift, scale).astype(BF16)
        u = _dot(h, w_ref[...])
        glu = u[:, :D_CONV] * _sigmoid(u[:, D_CONV:2 * D_CONV])
        gpad_ref[pl.ds(CONV_PAD + r0, FRONT_ROWS), :] = glu
        o = 2 * D_CONV
        q_ref[0, pl.ds(r0, FRONT_ROWS), :] = (u[:, o:o + D_ATTN] * (HEAD_DIM ** -0.5)).astype(q_ref.dtype)
        k_ref[0, pl.ds(r0, FRONT_ROWS), :] = u[:, o + D_ATTN:o + 2 * D_ATTN].astype(k_ref.dtype)
        v_ref[0, pl.ds(r0, FRONT_ROWS), :] = u[:, o + 2 * D_ATTN:o + 3 * D_ATTN].astype(v_ref.dtype)
        return carry

    lax.fori_loop(0, seq // FRONT_ROWS, proj, 0)

    def conv(i, carry):
        r0 = pl.multiple_of(i * CONV_ROWS, CONV_ROWS)
        cols = []
        for cb in range(D_CONV // LANES):
            cs = slice(cb * LANES, (cb + 1) * LANES)
            win = gpad_ref[pl.ds(r0, CONV_ROWS + 2 * CONV_PAD), cs]
            acc = jnp.zeros((CONV_ROWS, LANES), F32)
            for s in range(SUBLANES):
                ws = win[s:s + CONV_ROWS + 2 * CONV_PAD - SUBLANES]
                for a in range(2 * CONV_PAD // SUBLANES):
                    t = SUBLANES * a + s - (CONV_PAD - CONV_WIDTH // 2)
                    if 0 <= t < CONV_WIDTH:
                        acc = acc + ws[SUBLANES * a:SUBLANES * a + CONV_ROWS] * dww_ref[t:t + 1, cs]
            cols.append(acc)
        y = jnp.concatenate(cols, axis=1) + dwb_ref[...]
        mu = jnp.mean(y, axis=-1, keepdims=True)
        yc = y - mu
        var = jnp.mean(yc * yc, axis=-1, keepdims=True)
        z = yc * lax.rsqrt(var + EPS) * lng_ref[...] + lnb_ref[...]
        c_ref[0, pl.ds(r0, CONV_ROWS), :] = (z * _sigmoid(z)).astype(c_ref.dtype)
        return carry

    lax.fori_loop(0, seq // CONV_ROWS, conv, 0)


def _front(x, mods, mod_row, g1, w_a, dww, dwb, lng, lnb, kv_dtype):
    nreq, seq, _ = x.shape
    const2 = lambda b: (0, 0)
    tok = lambda b: (b, 0, 0)
    out_tok = pl.BlockSpec((1, seq, D_ATTN), tok)
    return pl.pallas_call(
        functools.partial(_front_kernel, seq=seq),
        grid=(nreq,),
        in_specs=[
            pl.BlockSpec((1, seq, D_MODEL), tok),
            pl.BlockSpec((1, N_MOD, D_MODEL), lambda b: (mod_row(b), 0, 0)),
            pl.BlockSpec((1, D_MODEL), const2),
            pl.BlockSpec((D_MODEL, D_PROJ_A), const2),
            pl.BlockSpec((CONV_WIDTH, D_CONV), const2),
            pl.BlockSpec((1, D_CONV), const2),
            pl.BlockSpec((1, D_CONV), const2),
            pl.BlockSpec((1, D_CONV), const2),
        ],
        out_specs=[out_tok, out_tok, out_tok, out_tok],
        out_shape=[
            jax.ShapeDtypeStruct((nreq, seq, D_ATTN), BF16),
            jax.ShapeDtypeStruct((nreq, seq, D_ATTN), kv_dtype),
            jax.ShapeDtypeStruct((nreq, seq, D_ATTN), kv_dtype),
            jax.ShapeDtypeStruct((nreq, seq, D_CONV), BF16),
        ],
        scratch_shapes=[pltpu.VMEM((seq + 2 * CONV_PAD, D_CONV), F32)],
        compiler_params=_params("arbitrary"),
        name="front",
    )(x, mods, g1, w_a, dww, dwb, lng, lnb)


def _pair_scores_q(q_pair):
    lane = lax.broadcasted_iota(jnp.int32, q_pair.shape, 1)
    zero = jnp.zeros_like(q_pair)
    return jnp.concatenate([jnp.where(lane < HEAD_DIM, q_pair, zero),
                            jnp.where(lane >= HEAD_DIM, q_pair, zero)], axis=0)


def _pair_merge(o2, t):
    lane = lax.broadcasted_iota(jnp.int32, (t, LANES), 1)
    return jnp.where(lane < HEAD_DIM, o2[:t], o2[t:])


def _ctx_attn_kernel(q_ref, k_ref, v_ref, o_ref):
    seq = q_ref.shape[1]
    for p in range(N_HEADS // 2):
        sl = slice(p * LANES, (p + 1) * LANES)
        q2 = _pair_scores_q(q_ref[0, :, sl])
        kp = k_ref[0, :, sl].astype(BF16)
        vp = v_ref[0, :, sl].astype(BF16)
        s = lax.dot_general(q2, kp, _NT, preferred_element_type=F32)
        mx = jnp.max(s, axis=-1, keepdims=True)
        e = jnp.exp(s - mx)
        den = jnp.sum(e, axis=-1, keepdims=True)
        o2 = _dot(e.astype(BF16), vp) / den
        o_ref[0, :, sl] = _pair_merge(o2, seq).astype(o_ref.dtype)


def _ctx_attn(q, k, v):
    nreq, seq, _ = q.shape
    spec = pl.BlockSpec((1, seq, D_ATTN), lambda b: (b, 0, 0))
    return pl.pallas_call(
        _ctx_attn_kernel,
        grid=(nreq,),
        in_specs=[spec, spec, spec],
        out_specs=spec,
        out_shape=jax.ShapeDtypeStruct((nreq, seq, D_ATTN), BF16),
        compiler_params=_params("arbitrary"),
        name="ctx_attn",
    )(q, k, v)


N_DR = 2 * NA_ROWS - 1
N_DC = 2 * NA_COLS - 1


def _bias_kernel(rpb_ref, o_ref, t2_ref):
    h = pl.program_id(0)
    qi = lax.broadcasted_iota(jnp.int32, (GRID_W, LANES), 0)
    lane = lax.broadcasted_iota(jnp.int32, (GRID_W, LANES), 1)
    kc = lane & (GRID_W - 1)
    d = kc - qi
    cs = jnp.clip(qi - NA_COLS // 2, 0, GRID_W - NA_COLS)
    inside = (kc >= cs) & (kc < cs + NA_COLS)
    for dr in range(N_DR):
        t = jnp.full((GRID_W, LANES), MASK_NEG, F32)
        for j in range(N_DC):
            val = rpb_ref[h * (N_DR * N_DC) + dr * N_DC + j]
            t = jnp.where(d == j - (NA_COLS - 1), val, t)
        t2_ref[dr] = jnp.where(inside, t, MASK_NEG)
    for o in range(NA_ROWS):
        for jj in range(NA_ROWS // 2):
            o_ref[0, o, :, jj * LANES:(jj + 1) * LANES] = jnp.where(
                lane < GRID_W, t2_ref[o + 2 * jj], t2_ref[o + 2 * jj + 1])


def _bias_table(rpb):
    return pl.pallas_call(
        _bias_kernel,
        grid=(N_HEADS,),
        in_specs=[pl.BlockSpec(memory_space=pltpu.SMEM)],
        out_specs=pl.BlockSpec((1, NA_ROWS, GRID_W, NA_ROWS * GRID_W), lambda h: (h, 0, 0, 0)),
        out_shape=jax.ShapeDtypeStruct((N_HEADS, NA_ROWS, GRID_W, NA_ROWS * GRID_W), F32),
        scratch_shapes=[pltpu.VMEM((N_DR, GRID_W, LANES), F32)],
        compiler_params=_params("arbitrary"),
        name="na_bias",
    )(rpb.reshape(-1))


def _na_attn_kernel(q_ref, k_ref, v_ref, kc_ref, vc_ref, tab_ref, o_ref, kcb_ref, vcb_ref, *, rows):
    r = pl.program_id(1)

    @pl.when(r == 0)
    def _():
        kcb_ref[...] = kc_ref[0].astype(BF16)
        vcb_ref[...] = vc_ref[0].astype(BF16)

    rs = jnp.clip(r - NA_ROWS // 2, 0, rows - NA_ROWS)
    k0 = pl.multiple_of(rs * GRID_W, GRID_W)
    off = rs - r + (NA_ROWS - 1)
    nloc = NA_ROWS * GRID_W
    for p in range(N_HEADS // 2):
        sl = slice(p * LANES, (p + 1) * LANES)
        q2 = _pair_scores_q(q_ref[0, :, sl])
        kl = k_ref[0, pl.ds(k0, nloc), sl]
        vl = v_ref[0, pl.ds(k0, nloc), sl]
        bias = jnp.concatenate([tab_ref[2 * p, off], tab_ref[2 * p + 1, off]], axis=0)
        s_loc = lax.dot_general(q2, kl, _NT, preferred_element_type=F32) + bias
        s_ctx = lax.dot_general(q2, kcb_ref[:, sl], _NT, preferred_element_type=F32)
        mx = jnp.maximum(jnp.max(s_loc, axis=-1, keepdims=True),
                         jnp.max(s_ctx, axis=-1, keepdims=True))
        e_loc = jnp.exp(s_loc - mx)
        e_ctx = jnp.exp(s_ctx - mx)
        den = jnp.sum(e_loc, axis=-1, keepdims=True) + jnp.sum(e_ctx, axis=-1, keepdims=True)
        o2 = (_dot(e_loc.astype(BF16), vl) + _dot(e_ctx.astype(BF16), vcb_ref[:, sl])) / den
        o_ref[0, :, sl] = _pair_merge(o2, GRID_W).astype(o_ref.dtype)


def _na_attn(q, k, v, kc, vc, tab):
    nreq, seq, _ = q.shape
    rows = seq // GRID_W
    past = kc.shape[1]
    qspec = pl.BlockSpec((1, GRID_W, D_ATTN), lambda b, r: (b, r, 0))
    kvspec = pl.BlockSpec((1, seq, D_ATTN), lambda b, r: (b, 0, 0))
    cspec = pl.BlockSpec((1, past, D_ATTN), lambda b, r: (b, 0, 0))
    return pl.pallas_call(
        functools.partial(_na_attn_kernel, rows=rows),
        grid=(nreq, rows),
        in_specs=[qspec, kvspec, kvspec, cspec, cspec,
                  pl.BlockSpec(tab.shape, lambda b, r: (0, 0, 0, 0))],
        out_specs=qspec,
        out_shape=jax.ShapeDtypeStruct((nreq, seq, D_ATTN), BF16),
        scratch_shapes=[pltpu.VMEM((past, D_ATTN), BF16), pltpu.VMEM((past, D_ATTN), BF16)],
        compiler_params=_params("arbitrary", "arbitrary"),
        name="na_attn",
    )(q, k, v, kc, vc, tab)


MERGE_ROWS = 256


def _merge_kernel(x_ref, mods_ref, g1_ref, g2_ref, c_ref, a_ref, wg_ref, wpw_ref, wao_ref,
                  wout_ref, wr_ref, x1_ref, h2_ref, aff_ref):
    x = x_ref[0]
    m = mods_ref[0]
    h = _rms_mod(x, g1_ref[...], m[0:1], m[1:2]).astype(BF16)
    gates = _sigmoid(_dot(h, wg_ref[...]))
    conv_out = _dot(c_ref[0], wpw_ref[...])
    attn_out = _dot(a_ref[0], wao_ref[...])
    merged = gates[:, :D_MODEL] * conv_out + gates[:, D_MODEL:] * attn_out
    x1 = x + m[2:3] * _dot(merged.astype(BF16), wout_ref[...])
    x1_ref[0] = x1
    h2 = _rms_mod(x1, g2_ref[...], m[3:4], m[4:5]).astype(BF16)
    h2_ref[0] = h2
    logits = _dot(h2, wr_ref[...])
    e = jnp.exp(logits - jnp.max(logits, axis=-1, keepdims=True))
    aff_ref[0] = e / jnp.sum(e, axis=-1, keepdims=True)


def _merge(x, mods, mod_row, g1, g2, cact, heads, w_g, w_pw, w_ao, w_out, w_r):
    nreq, seq, _ = x.shape
    tm = MERGE_ROWS
    tok = lambda b, j: (b, j, 0)
    const2 = lambda b, j: (0, 0)
    return pl.pallas_call(
        _merge_kernel,
        grid=(nreq, seq // tm),
        in_specs=[
            pl.BlockSpec((1, tm, D_MODEL), tok),
            pl.BlockSpec((1, N_MOD, D_MODEL), lambda b, j: (mod_row(b), 0, 0)),
            pl.BlockSpec((1, D_MODEL), const2),
            pl.BlockSpec((1, D_MODEL), const2),
            pl.BlockSpec((1, tm, D_CONV), tok),
            pl.BlockSpec((1, tm, D_ATTN), tok),
            pl.BlockSpec(w_g.shape, const2),
            pl.BlockSpec(w_pw.shape, const2),
            pl.BlockSpec(w_ao.shape, const2),
            pl.BlockSpec(w_out.shape, const2),
            pl.BlockSpec(w_r.shape, const2),
        ],
        out_specs=[
            pl.BlockSpec((1, tm, D_MODEL), tok),
            pl.BlockSpec((1, tm, D_MODEL), tok),
            pl.BlockSpec((1, tm, N_EXPERTS), tok),
        ],
        out_shape=[
            jax.ShapeDtypeStruct((nreq, seq, D_MODEL), F32),
            jax.ShapeDtypeStruct((nreq, seq, D_MODEL), BF16),
            jax.ShapeDtypeStruct((nreq, seq, N_EXPERTS), F32),
        ],
        compiler_params=_params("arbitrary", "arbitrary"),
        name="merge",
    )(x, mods, g1, g2, cact, heads, w_g, w_pw, w_ao, w_out, w_r)


def _route_kernel(aff_ref, slot_t_ref, slot_r_ref, aff_r_ref, *, cap):
    a = aff_ref[0]
    seq = a.shape[0]
    one, zero = jnp.ones_like(a), jnp.zeros_like(a)

    def search(i, thr):
        cand = thr | jnp.left_shift(jnp.int32(1), 30 - i)
        cnt = jnp.sum(jnp.where(a >= pltpu.bitcast(cand, F32), one, zero), axis=0, keepdims=True)
        return jnp.where(cnt >= cap, cand, thr)

    thr = lax.fori_loop(0, 31, search, jnp.zeros((1, LANES), jnp.int32))
    gt = a >= pltpu.bitcast(thr + 1, F32)
    eq = (a >= pltpu.bitcast(thr, F32)) & jnp.logical_not(gt)
    need = cap - jnp.sum(jnp.where(gt, one, zero), axis=0, keepdims=True)
    ti = lax.broadcasted_iota(jnp.int32, (seq, seq), 0)
    tj = lax.broadcasted_iota(jnp.int32, (seq, seq), 1)
    before = jnp.where(tj < ti, 1.0, 0.0).astype(BF16)
    eq_rank = _dot(before, jnp.where(eq, one, zero).astype(BF16))
    sel = gt | (eq & (eq_rank < need))
    pos = _dot(before, jnp.where(sel, one, zero).astype(BF16))
    slot = jnp.where(sel, pos, -1.0)
    slot_t_ref[0] = slot
    slot_r_ref[0] = slot.T
    aff_r_ref[0] = a.T


def _route(aff_t, cap):
    ngroup, seq, _ = aff_t.shape
    tspec = pl.BlockSpec((1, seq, LANES), lambda g: (g, 0, 0))
    rspec = pl.BlockSpec((1, LANES, seq), lambda g: (g, 0, 0))
    return pl.pallas_call(
        functools.partial(_route_kernel, cap=cap),
        grid=(ngroup,),
        in_specs=[tspec],
        out_specs=[tspec, rspec, rspec],
        out_shape=[
            jax.ShapeDtypeStruct((ngroup, seq, LANES), F32),
            jax.ShapeDtypeStruct((ngroup, LANES, seq), F32),
            jax.ShapeDtypeStruct((ngroup, LANES, seq), F32),
        ],
        compiler_params=_params("arbitrary"),
        name="route",
    )(aff_t)


GATHER_ROWS = 512


def _gather_kernel(slot_ref, aff_ref, h_ref, x_ref, g_ref, p_ref, *, cap):
    seq = h_ref.shape[1]
    nslot = N_EXPERTS * cap
    sub = lax.broadcasted_iota(jnp.int32, (cap, seq), 0).astype(F32)
    one, zero = jnp.ones((cap, seq), F32), jnp.zeros((cap, seq), F32)
    for e in range(N_EXPERTS):
        row = jnp.broadcast_to(slot_ref[0, e:e + 1, :], (cap, seq))
        p_ref[e * cap:(e + 1) * cap, :] = jnp.where(row == sub, one, zero).astype(BF16)
    h = h_ref[0]
    per_chunk = GATHER_ROWS // cap
    for c in range(nslot // GATHER_ROWS):
        rows = slice(c * GATHER_ROWS, (c + 1) * GATHER_ROWS)
        x = _dot(p_ref[rows, :], h).astype(x_ref.dtype)
        for i in range(per_chunk):
            x_ref[c * per_chunk + i] = x[i * cap:(i + 1) * cap]
    a = aff_ref[0]
    a_hi = a.astype(BF16)
    r1 = a - a_hi.astype(F32)
    a_mid = r1.astype(BF16)
    a_lo = (r1 - a_mid.astype(F32)).astype(BF16)
    a3 = jnp.concatenate([a_hi, a_mid, a_lo], axis=0)
    g3 = lax.dot_general(a3, p_ref[...], _NT, preferred_element_type=F32)
    gfull = g3[0:N_EXPERTS] + g3[N_EXPERTS:2 * N_EXPERTS] + g3[2 * N_EXPERTS:3 * N_EXPERTS]
    ei = lax.broadcasted_iota(jnp.int32, (N_EXPERTS, nslot), 0)
    si = lax.broadcasted_iota(jnp.int32, (N_EXPERTS, nslot), 1)
    own = (si >= ei * cap) & (si < (ei + 1) * cap)
    g_ref[0] = jnp.sum(jnp.where(own, gfull, 0.0), axis=0, keepdims=True)


def _gather(slot_r, aff_r, h2, cap):
    nreq, seq, _ = h2.shape
    nslot = N_EXPERTS * cap
    grp = lambda b: (b // REQ_PER_GROUP, b % REQ_PER_GROUP, 0)
    x, g = pl.pallas_call(
        functools.partial(_gather_kernel, cap=cap),
        grid=(nreq,),
        in_specs=[
            pl.BlockSpec((1, N_EXPERTS, seq), grp),
            pl.BlockSpec((1, N_EXPERTS, seq), grp),
            pl.BlockSpec((1, seq, D_MODEL), lambda b: (b, 0, 0)),
        ],
        out_specs=[
            pl.BlockSpec((N_EXPERTS, cap, D_MODEL), lambda b: (0, b, 0)),
            pl.BlockSpec((1, 1, nslot), lambda b: (b, 0, 0)),
        ],
        out_shape=[
            jax.ShapeDtypeStruct((N_EXPERTS, nreq * cap, D_MODEL), BF16),
            jax.ShapeDtypeStruct((nreq, 1, nslot), F32),
        ],
        scratch_shapes=[pltpu.VMEM((nslot, seq), BF16)],
        compiler_params=_params("arbitrary"),
        name="gather",
    )(slot_r, aff_r, h2)
    return x, g


EXPERT_FCHUNK = 512
EXPERT_ROWS = 256


def _expert_kernel(xc_ref, xl_ref, g_ref, wg_ref, wu_ref, wd_ref, yc_ref, yl_ref,
                   acc_ref, wgb_ref, wub_ref, wdb_ref):
    f = pl.program_id(1)
    nf = pl.num_programs(1)
    half = xc_ref.shape[1]
    wgb_ref[...] = wg_ref[0].astype(BF16)
    wub_ref[...] = wu_ref[0].astype(BF16)
    wdb_ref[...] = wd_ref[0].astype(BF16)

    @pl.when(f == 0)
    def _():
        acc_ref[...] = jnp.zeros_like(acc_ref)

    for part, x_ref in enumerate((xc_ref, xl_ref)):
        def tile(i, carry, x_ref=x_ref, part=part):
            r0 = pl.multiple_of(i * EXPERT_ROWS, EXPERT_ROWS)
            x = x_ref[0, pl.ds(r0, EXPERT_ROWS), :]
            gq = _dot(x, wgb_ref[...])
            uq = _dot(x, wub_ref[...])
            hid = (gq * _sigmoid(gq) * uq).astype(BF16)
            acc_ref[pl.ds(part * half + r0, EXPERT_ROWS), :] += _dot(hid, wdb_ref[...])
            return carry
        lax.fori_loop(0, half // EXPERT_ROWS, tile, 0)

    @pl.when(f == nf - 1)
    def _():
        ri = lax.broadcasted_iota(jnp.int32, (EXPERT_ROWS, EXPERT_ROWS), 0)
        ci = lax.broadcasted_iota(jnp.int32, (EXPERT_ROWS, EXPERT_ROWS), 1)
        for part, y_ref in enumerate((yc_ref, yl_ref)):
            for i in range(half // EXPERT_ROWS):
                t = part * (half // EXPERT_ROWS) + i
                grow = jnp.broadcast_to(g_ref[0, t:t + 1, :], (EXPERT_ROWS, EXPERT_ROWS))
                gcol = jnp.sum(jnp.where(ri == ci, grow, 0.0), axis=-1, keepdims=True)
                rows = slice(i * EXPERT_ROWS, (i + 1) * EXPERT_ROWS)
                y = acc_ref[t * EXPERT_ROWS:(t + 1) * EXPERT_ROWS, :] * gcol
                y_ref[0, rows, :] = y.astype(y_ref.dtype)


def _experts(xc, xl, g, w_gate, w_up, w_down):
    ne, half, _ = xc.shape
    fc = EXPERT_FCHUNK
    xspec = pl.BlockSpec((1, half, D_MODEL), lambda e, f: (e, 0, 0))
    return pl.pallas_call(
        _expert_kernel,
        grid=(ne, D_EXPERT // fc),
        in_specs=[
            xspec, xspec,
            pl.BlockSpec((1,) + g.shape[1:], lambda e, f: (e, 0, 0)),
            pl.BlockSpec((1, D_MODEL, fc), lambda e, f: (e, 0, f)),
            pl.BlockSpec((1, D_MODEL, fc), lambda e, f: (e, 0, f)),
            pl.BlockSpec((1, fc, D_MODEL), lambda e, f: (e, f, 0)),
        ],
        out_specs=[xspec, xspec],
        out_shape=[jax.ShapeDtypeStruct(xc.shape, BF16), jax.ShapeDtypeStruct(xl.shape, BF16)],
        scratch_shapes=[
            pltpu.VMEM((2 * half, D_MODEL), F32),
            pltpu.VMEM((D_MODEL, fc), BF16),
            pltpu.VMEM((D_MODEL, fc), BF16),
            pltpu.VMEM((fc, D_MODEL), BF16),
        ],
        compiler_params=_params("arbitrary", "arbitrary"),
        name="experts",
    )(xc, xl, g, w_gate, w_up, w_down)


COMBINE_ROWS = 256


def _combine_kernel(x1_ref, mods_ref, slot_ref, y_ref, gf_ref, o_ref, *, cap):
    b = pl.program_id(0)
    nslot = N_EXPERTS * cap
    li = lax.broadcasted_iota(jnp.int32, (LANES, nslot), 0)
    si = lax.broadcasted_iota(jnp.int32, (LANES, nslot), 1)
    lane0 = (b % REQ_PER_GROUP) * N_EXPERTS
    expand = jnp.where(li == lane0 + si // cap, 1.0, 0.0).astype(BF16)
    slot_exp = _dot(slot_ref[0].astype(BF16), expand)
    want = (lax.broadcasted_iota(jnp.int32, slot_exp.shape, 1) & (cap - 1)).astype(F32)
    scat = jnp.where(slot_exp == want, 1.0, 0.0).astype(BF16)
    moe = _dot(scat, y_ref[...].reshape(nslot, D_MODEL))
    x2 = x1_ref[0] + mods_ref[0][5:6] * moe
    o_ref[0] = x2 * lax.rsqrt(jnp.mean(x2 * x2, axis=-1, keepdims=True) + EPS) * gf_ref[...]


def _combine(x1, mods, mod_row, slot_t, y, gf, cap):
    nreq, seq, _ = x1.shape
    nslot = N_EXPERTS * cap
    tm = COMBINE_ROWS
    tok = lambda b, j: (b, j, 0)
    return pl.pallas_call(
        functools.partial(_combine_kernel, cap=cap),
        grid=(nreq, seq // tm),
        in_specs=[
            pl.BlockSpec((1, tm, D_MODEL), tok),
            pl.BlockSpec((1, N_MOD, D_MODEL), lambda b, j: (mod_row(b), 0, 0)),
            pl.BlockSpec((1, tm, LANES), lambda b, j: (b // REQ_PER_GROUP, j, 0)),
            pl.BlockSpec((N_EXPERTS, cap, D_MODEL), lambda b, j: (0, b, 0)),
            pl.BlockSpec((1, D_MODEL), lambda b, j: (0, 0)),
        ],
        out_specs=pl.BlockSpec((1, tm, D_MODEL), tok),
        out_shape=jax.ShapeDtypeStruct(x1.shape, F32),
        compiler_params=_params("arbitrary", "arbitrary"),
        name="combine",
    )(x1, mods, slot_t, y, gf)


def _group_lanes(aff):
    nreq, seq, ne = aff.shape
    a = aff.reshape(nreq // REQ_PER_GROUP, REQ_PER_GROUP, seq, ne)
    return jnp.transpose(a, (0, 2, 1, 3)).reshape(nreq // REQ_PER_GROUP, seq, REQ_PER_GROUP * ne)


def kernel(x_prompt, x_sample, cache_ctx_k, cache_ctx_v, c, c_ctx, w_ada, b_ada, norm1_g, w_in,
           conv_dw_w, conv_dw_b, conv_ln_g, conv_ln_b, w_conv_pw, na_rpb, w_attn_o, w_out,
           norm2_g, w_router, w_gate_e, w_up_e, w_down_e, final_g):
    assert w_ada.shape[0] == 1, "single trunk layer"
    nctx, ctx_seq, _ = x_prompt.shape
    nlat, lat_seq, _ = x_sample.shape
    ctx_cap = EC_CAPACITY * ctx_seq // N_EXPERTS
    lat_cap = EC_CAPACITY * lat_seq // N_EXPERTS
    assert nctx * ctx_cap == nlat * lat_cap

    cond = jnp.zeros((2 * nlat, D_MODEL), F32).at[:nlat].set(c).at[nlat].set(c_ctx)
    mods = _ada(cond, w_ada[0], b_ada).reshape(2 * nlat, N_MOD, D_MODEL)
    ctx_row = lambda b: nlat
    lat_row = lambda b: b

    w_in_b = w_in[0].astype(BF16)
    w_a, w_g = w_in_b[:, :D_PROJ_A], w_in_b[:, D_PROJ_A:]
    w_pw = w_conv_pw[0].astype(BF16)
    w_ao = w_attn_o[0].astype(BF16)
    w_o = w_out[0].astype(BF16)
    w_r = w_router[0].astype(BF16)
    conv_args = (conv_dw_w[0], conv_dw_b, conv_ln_g, conv_ln_b)

    q_c, k_c, v_c, cact_c = _front(x_prompt, mods, ctx_row, norm1_g, w_a, *conv_args, F32)
    q_l, k_l, v_l, cact_l = _front(x_sample, mods, lat_row, norm1_g, w_a, *conv_args, BF16)

    heads_c = _ctx_attn(q_c, k_c, v_c)
    tab = _bias_table(na_rpb[0])
    past = cache_ctx_k.shape[2]
    heads_l = _na_attn(q_l, k_l, v_l, cache_ctx_k[:, 0].reshape(nlat, past, D_ATTN),
                       cache_ctx_v[:, 0].reshape(nlat, past, D_ATTN), tab)

    merge_w = (w_g, w_pw, w_ao, w_o, w_r)
    x1_c, h2_c, aff_c = _merge(x_prompt, mods, ctx_row, norm1_g, norm2_g, cact_c, heads_c, *merge_w)
    x1_l, h2_l, aff_l = _merge(x_sample, mods, lat_row, norm1_g, norm2_g, cact_l, heads_l, *merge_w)

    slot_t_c, slot_r_c, aff_r_c = _route(_group_lanes(aff_c), ctx_cap)
    slot_t_l, slot_r_l, aff_r_l = _route(_group_lanes(aff_l), lat_cap)

    xg_c, g_c = _gather(slot_r_c, aff_r_c, h2_c, ctx_cap)
    xg_l, g_l = _gather(slot_r_l, aff_r_l, h2_l, lat_cap)

    def gate_rows(g, nreq, cap):
        g = jnp.transpose(g.reshape(nreq, N_EXPERTS, cap), (1, 0, 2))
        return g.reshape(N_EXPERTS, nreq * cap // EXPERT_ROWS, EXPERT_ROWS)

    gates = jnp.concatenate([gate_rows(g_c, nctx, ctx_cap), gate_rows(g_l, nlat, lat_cap)], axis=1)
    y_c, y_l = _experts(xg_c, xg_l, gates, w_gate_e[0], w_up_e[0], w_down_e[0])

    gf = final_g.reshape(1, D_MODEL)
    y_prompt = _combine(x1_c, mods, ctx_row, slot_t_c, y_c, gf, ctx_cap)
    y_sample = _combine(x1_l, mods, lat_row, slot_t_l, y_l, gf, lat_cap)

    state_k = k_c.reshape(nctx, 1, ctx_seq, N_HEADS, HEAD_DIM)
    state_v = v_c.reshape(nctx, 1, ctx_seq, N_HEADS, HEAD_DIM)
    return (y_prompt, y_sample, state_k, state_v)
```

```python
import functools

import jax
import jax.numpy as jnp
from jax import lax
from jax.experimental import pallas as pl
from jax.experimental.pallas import tpu as pltpu

F32 = jnp.float32
BF16 = jnp.bfloat16

D_MODEL = 1024
D_CONV = 512
CONV_WIDTH = 31
N_HEADS = 8
HEAD_DIM = 64
D_ATTN = N_HEADS * HEAD_DIM
GRID_W = 64
NA_ROWS = 8
NA_COLS = 16
N_EXPERTS = 16
D_EXPERT = 2048
EC_CAPACITY = 2
N_MOD = 6
EPS = 1e-6
D_PROJ_A = 2 * D_CONV + 3 * D_ATTN
D_IN = D_PROJ_A + 2 * D_MODEL
LANES = 128
REQ_PER_GROUP = LANES // N_EXPERTS
MASK_NEG = -1e30
VMEM_LIMIT = 56 * 1024 * 1024

_NT = (((1,), (1,)), ((), ()))


def _params(*sem):
    return pltpu.CompilerParams(dimension_semantics=sem, vmem_limit_bytes=VMEM_LIMIT)


def _dot(a, b):
    return jnp.dot(a, b, preferred_element_type=F32)


def _sigmoid(x):
    return 1.0 / (1.0 + jnp.exp(-x))


def _split_bf16(x):
    hi = x.astype(BF16)
    lo = (x - hi.astype(F32)).astype(BF16)
    return hi, lo


def _rms_mod(x, g, shift, scale):
    xn = x * lax.rsqrt(jnp.mean(x * x, axis=-1, keepdims=True) + EPS) * g
    return xn * (1.0 + scale) + shift


def _ada_kernel(cond_ref, w_ref, b_ref, o_ref):
    c = cond_ref[...]
    a = c * _sigmoid(c)
    ah, al = _split_bf16(a)
    wh, wl = _split_bf16(w_ref[...])
    o_ref[...] = _dot(ah, wh) + _dot(al, wh) + _dot(ah, wl) + b_ref[...]


def _ada(cond, w_ada, b_ada):
    n = w_ada.shape[1]
    bn = 1024
    return pl.pallas_call(
        _ada_kernel,
        grid=(n // bn,),
        in_specs=[
            pl.BlockSpec(cond.shape, lambda j: (0, 0)),
            pl.BlockSpec((D_MODEL, bn), lambda j: (0, j)),
            pl.BlockSpec((1, bn), lambda j: (0, j)),
        ],
        out_specs=pl.BlockSpec((cond.shape[0], bn), lambda j: (0, j)),
        out_shape=jax.ShapeDtypeStruct((cond.shape[0], n), F32),
        compiler_params=_params("arbitrary"),
        name="ada",
    )(cond, w_ada, b_ada)


CONV_PAD = 16
FRONT_ROWS = 256
CONV_ROWS = 64
SUBLANES = 8


def _front_kernel(x_ref, mods_ref, g1_ref, w_ref, dww_ref, dwb_ref, lng_ref, lnb_ref,
                  q_ref, k_ref, v_ref, c_ref, gpad_ref, *, seq):
    m = mods_ref[0]
    shift, scale = m[0:1], m[1:2]
    zeros = jnp.zeros((CONV_PAD, D_CONV), F32)
    gpad_ref[0:CONV_PAD, :] = zeros
    gpad_ref[CONV_PAD + seq:CONV_PAD + seq + CONV_PAD, :] = zeros

    def proj(i, carry):
        r0 = pl.multiple_of(i * FRONT_ROWS, FRONT_ROWS)
        x = x_ref[0, pl.ds(r0, FRONT_ROWS), :]
        h = _rms_mod(x, g1_ref[...], shift, scale).astype(BF16)
        u = _dot(h, w_ref[...])
        glu = u[:, :D_CONV] * _sigmoid(u[:, D_CONV:2 * D_CONV])
        gpad_ref[pl.ds(CONV_PAD + r0, FRONT_ROWS), :] = glu
        o = 2 * D_CONV
        q_ref[0, pl.ds(r0, FRONT_ROWS), :] = (u[:, o:o + D_ATTN] * (HEAD_DIM ** -0.5)).astype(q_ref.dtype)
        k_ref[0, pl.ds(r0, FRONT_ROWS), :] = u[:, o + D_ATTN:o + 2 * D_ATTN].astype(k_ref.dtype)
        v_ref[0, pl.ds(r0, FRONT_ROWS), :] = u[:, o + 2 * D_ATTN:o + 3 * D_ATTN].astype(v_ref.dtype)
        return carry

    lax.fori_loop(0, seq // FRONT_ROWS, proj, 0)

    def conv(i, carry):
        r0 = pl.multiple_of(i * CONV_ROWS, CONV_ROWS)
        cols = []
        for cb in range(D_CONV // LANES):
            cs = slice(cb * LANES, (cb + 1) * LANES)
            win = gpad_ref[pl.ds(r0, CONV_ROWS + 2 * CONV_PAD), cs]
            acc = None
            for s in range(SUBLANES):
                nrow = CONV_ROWS + (SUBLANES if s else 0)
                z = None
                for a in range(2 * CONV_PAD // SUBLANES):
                    t = SUBLANES * a + s - (CONV_PAD - CONV_WIDTH // 2)
                    if 0 <= t < CONV_WIDTH:
                        term = win[SUBLANES * a:SUBLANES * a + nrow] * dww_ref[t:t + 1, cs]
                        z = term if z is None else z + term
                z = z[s:s + CONV_ROWS]
                acc = z if acc is None else acc + z
            cols.append(acc)
        y = jnp.concatenate(cols, axis=1) + dwb_ref[...]
        mu = jnp.mean(y, axis=-1, keepdims=True)
        yc = y - mu
        var = jnp.mean(yc * yc, axis=-1, keepdims=True)
        z = yc * lax.rsqrt(var + EPS) * lng_ref[...] + lnb_ref[...]
        c_ref[0, pl.ds(r0, CONV_ROWS), :] = (z * _sigmoid(z)).astype(c_ref.dtype)
        return carry

    lax.fori_loop(0, seq // CONV_ROWS, conv, 0)


def _front(x, mods, mod_row, g1, w_a, dww, dwb, lng, lnb, kv_dtype):
    nreq, seq, _ = x.shape
    const2 = lambda b: (0, 0)
    tok = lambda b: (b, 0, 0)
    out_tok = pl.BlockSpec((1, seq, D_ATTN), tok)
    return pl.pallas_call(
        functools.partial(_front_kernel, seq=seq),
        grid=(nreq,),
        in_specs=[
            pl.BlockSpec((1, seq, D_MODEL), tok),
            pl.BlockSpec((1, N_MOD, D_MODEL), lambda b: (mod_row(b), 0, 0)),
            pl.BlockSpec((1, D_MODEL), const2),
            pl.BlockSpec((D_MODEL, D_PROJ_A), const2),
            pl.BlockSpec((CONV_WIDTH, D_CONV), const2),
            pl.BlockSpec((1, D_CONV), const2),
            pl.BlockSpec((1, D_CONV), const2),
            pl.BlockSpec((1, D_CONV), const2),
        ],
        out_specs=[out_tok, out_tok, out_tok, out_tok],
        out_shape=[
            jax.ShapeDtypeStruct((nreq, seq, D_ATTN), BF16),
            jax.ShapeDtypeStruct((nreq, seq, D_ATTN), kv_dtype),
            jax.ShapeDtypeStruct((nreq, seq, D_ATTN), kv_dtype),
            jax.ShapeDtypeStruct((nreq, seq, D_CONV), BF16),
        ],
        scratch_shapes=[pltpu.VMEM((seq + 2 * CONV_PAD, D_CONV), F32)],
        compiler_params=_params("arbitrary"),
        name="front",
    )(x, mods, g1, w_a, dww, dwb, lng, lnb)


def _pair_scores_q(q_pair):
    lane = lax.broadcasted_iota(jnp.int32, q_pair.shape, 1)
    zero = jnp.zeros_like(q_pair)
    return jnp.concatenate([jnp.where(lane < HEAD_DIM, q_pair, zero),
                            jnp.where(lane >= HEAD_DIM, q_pair, zero)], axis=0)


def _pair_merge(o2, t):
    lane = lax.broadcasted_iota(jnp.int32, (t, LANES), 1)
    return jnp.where(lane < HEAD_DIM, o2[:t], o2[t:])


def _ctx_attn_kernel(q_ref, k_ref, v_ref, o_ref):
    seq = q_ref.shape[1]
    for p in range(N_HEADS // 2):
        sl = slice(p * LANES, (p + 1) * LANES)
        q2 = _pair_scores_q(q_ref[0, :, sl])
        kp = k_ref[0, :, sl].astype(BF16)
        vp = v_ref[0, :, sl].astype(BF16)
        s = lax.dot_general(q2, kp, _NT, preferred_element_type=F32)
        mx = jnp.max(s, axis=-1, keepdims=True)
        e = jnp.exp(s - mx)
        den = jnp.sum(e, axis=-1, keepdims=True)
        o2 = _dot(e.astype(BF16), vp) / den
        o_ref[0, :, sl] = _pair_merge(o2, seq).astype(o_ref.dtype)


def _ctx_attn(q, k, v):
    nreq, seq, _ = q.shape
    spec = pl.BlockSpec((1, seq, D_ATTN), lambda b: (b, 0, 0))
    return pl.pallas_call(
        _ctx_attn_kernel,
        grid=(nreq,),
        in_specs=[spec, spec, spec],
        out_specs=spec,
        out_shape=jax.ShapeDtypeStruct((nreq, seq, D_ATTN), BF16),
        compiler_params=_params("arbitrary"),
        name="ctx_attn",
    )(q, k, v)


N_DR = 2 * NA_ROWS - 1
N_DC = 2 * NA_COLS - 1


def _bias_kernel(rpb_ref, o_ref, t2_ref):
    h = pl.program_id(0)
    qi = lax.broadcasted_iota(jnp.int32, (GRID_W, LANES), 0)
    lane = lax.broadcasted_iota(jnp.int32, (GRID_W, LANES), 1)
    kc = lane & (GRID_W - 1)
    d = kc - qi
    cs = jnp.clip(qi - NA_COLS // 2, 0, GRID_W - NA_COLS)
    inside = (kc >= cs) & (kc < cs + NA_COLS)
    for dr in range(N_DR):
        t = jnp.full((GRID_W, LANES), MASK_NEG, F32)
        for j in range(N_DC):
            val = rpb_ref[h * (N_DR * N_DC) + dr * N_DC + j]
            t = jnp.where(d == j - (NA_COLS - 1), val, t)
        t2_ref[dr] = jnp.where(inside, t, MASK_NEG)
    for o in range(NA_ROWS):
        for jj in range(NA_ROWS // 2):
            o_ref[0, o, :, jj * LANES:(jj + 1) * LANES] = jnp.where(
                lane < GRID_W, t2_ref[o + 2 * jj], t2_ref[o + 2 * jj + 1])


def _bias_table(rpb):
    return pl.pallas_call(
        _bias_kernel,
        grid=(N_HEADS,),
        in_specs=[pl.BlockSpec(memory_space=pltpu.SMEM)],
        out_specs=pl.BlockSpec((1, NA_ROWS, GRID_W, NA_ROWS * GRID_W), lambda h: (h, 0, 0, 0)),
        out_shape=jax.ShapeDtypeStruct((N_HEADS, NA_ROWS, GRID_W, NA_ROWS * GRID_W), F32),
        scratch_shapes=[pltpu.VMEM((N_DR, GRID_W, LANES), F32)],
        compiler_params=_params("arbitrary"),
        name="na_bias",
    )(rpb.reshape(-1))


def _na_attn_kernel(q_ref, k_ref, v_ref, kc_ref, vc_ref, tab_ref, o_ref, *, rows):
    r = pl.program_id(1)
    rs = jnp.clip(r - NA_ROWS // 2, 0, rows - NA_ROWS)
    k0 = pl.multiple_of(rs * GRID_W, GRID_W)
    off = rs - r + (NA_ROWS - 1)
    nloc = NA_ROWS * GRID_W
    for p in range(N_HEADS // 2):
        sl = slice(p * LANES, (p + 1) * LANES)
        q2 = _pair_scores_q(q_ref[0, :, sl])
        kl = k_ref[0, pl.ds(k0, nloc), sl]
        vl = v_ref[0, pl.ds(k0, nloc), sl]
        bias = jnp.concatenate([tab_ref[2 * p, off], tab_ref[2 * p + 1, off]], axis=0)
        s_loc = lax.dot_general(q2, kl, _NT, preferred_element_type=F32) + bias
        s_ctx = lax.dot_general(q2, kc_ref[0, :, sl], _NT, preferred_element_type=F32)
        mx = jnp.maximum(jnp.max(s_loc, axis=-1, keepdims=True),
                         jnp.max(s_ctx, axis=-1, keepdims=True))
        e_loc = jnp.exp(s_loc - mx)
        e_ctx = jnp.exp(s_ctx - mx)
        den = jnp.sum(e_loc, axis=-1, keepdims=True) + jnp.sum(e_ctx, axis=-1, keepdims=True)
        o2 = (_dot(e_loc.astype(BF16), vl) + _dot(e_ctx.astype(BF16), vc_ref[0, :, sl])) / den
        o_ref[0, :, sl] = _pair_merge(o2, GRID_W).astype(o_ref.dtype)


def _na_attn(q, k, v, kc, vc, tab):
    nreq, seq, _ = q.shape
    rows = seq // GRID_W
    past = kc.shape[1]
    qspec = pl.BlockSpec((1, GRID_W, D_ATTN), lambda b, r: (b, r, 0))
    kvspec = pl.BlockSpec((1, seq, D_ATTN), lambda b, r: (b, 0, 0))
    cspec = pl.BlockSpec((1, past, D_ATTN), lambda b, r: (b, 0, 0))
    return pl.pallas_call(
        functools.partial(_na_attn_kernel, rows=rows),
        grid=(nreq, rows),
        in_specs=[qspec, kvspec, kvspec, cspec, cspec,
                  pl.BlockSpec(tab.shape, lambda b, r: (0, 0, 0, 0))],
        out_specs=qspec,
        out_shape=jax.ShapeDtypeStruct((nreq, seq, D_ATTN), BF16),
        compiler_params=_params("arbitrary", "arbitrary"),
        name="na_attn",
    )(q, k, v, kc, vc, tab)


MERGE_ROWS = 512
MERGE_SUB = 256


def _merge_kernel(x_ref, mods_ref, g1_ref, g2_ref, c_ref, a_ref, wg_ref, wpw_ref, wao_ref,
                  wout_ref, wr_ref, x1_ref, h2_ref, aff_ref):
    m = mods_ref[0]
    for i in range(MERGE_ROWS // MERGE_SUB):
        rows = slice(i * MERGE_SUB, (i + 1) * MERGE_SUB)
        x = x_ref[0, rows, :]
        h = _rms_mod(x, g1_ref[...], m[0:1], m[1:2]).astype(BF16)
        gates = _sigmoid(_dot(h, wg_ref[:, D_PROJ_A - D_IN // 2:]))
        conv_out = _dot(c_ref[0, rows, :], wpw_ref[...])
        attn_out = _dot(a_ref[0, rows, :], wao_ref[...])
        merged = gates[:, :D_MODEL] * conv_out + gates[:, D_MODEL:] * attn_out
        x1 = x + m[2:3] * _dot(merged.astype(BF16), wout_ref[...])
        x1_ref[0, rows, :] = x1
        h2 = _rms_mod(x1, g2_ref[...], m[3:4], m[4:5]).astype(BF16)
        h2_ref[0, rows, :] = h2
        logits = _dot(h2, wr_ref[...])
        e = jnp.exp(logits - jnp.max(logits, axis=-1, keepdims=True))
        aff_ref[0, rows, :] = e / jnp.sum(e, axis=-1, keepdims=True)


def _merge(x, mods, mod_row, g1, g2, cact, heads, w_g, w_pw, w_ao, w_out, w_r):
    nreq, seq, _ = x.shape
    tm = MERGE_ROWS
    tok = lambda b, j: (b, j, 0)
    const2 = lambda b, j: (0, 0)
    return pl.pallas_call(
        _merge_kernel,
        grid=(nreq, seq // tm),
        in_specs=[
            pl.BlockSpec((1, tm, D_MODEL), tok),
            pl.BlockSpec((1, N_MOD, D_MODEL), lambda b, j: (mod_row(b), 0, 0)),
            pl.BlockSpec((1, D_MODEL), const2),
            pl.BlockSpec((1, D_MODEL), const2),
            pl.BlockSpec((1, tm, D_CONV), tok),
            pl.BlockSpec((1, tm, D_ATTN), tok),
            pl.BlockSpec((D_MODEL, D_IN // 2), lambda b, j: (0, 1)),
            pl.BlockSpec(w_pw.shape, const2),
            pl.BlockSpec(w_ao.shape, const2),
            pl.BlockSpec(w_out.shape, const2),
            pl.BlockSpec(w_r.shape, const2),
        ],
        out_specs=[
            pl.BlockSpec((1, tm, D_MODEL), tok),
            pl.BlockSpec((1, tm, D_MODEL), tok),
            pl.BlockSpec((1, tm, N_EXPERTS), tok),
        ],
        out_shape=[
            jax.ShapeDtypeStruct((nreq, seq, D_MODEL), F32),
            jax.ShapeDtypeStruct((nreq, seq, D_MODEL), BF16),
            jax.ShapeDtypeStruct((nreq, seq, N_EXPERTS), F32),
        ],
        compiler_params=_params("arbitrary", "arbitrary"),
        name="merge",
    )(x, mods, g1, g2, cact, heads, w_g, w_pw, w_ao, w_out, w_r)


def _route_kernel(aff_ref, slot_t_ref, slot_r_ref, aff_r_ref, *, cap):
    a = aff_ref[0]
    seq = a.shape[0]
    one, zero = jnp.ones_like(a), jnp.zeros_like(a)

    def search(i, thr):
        cand = thr | jnp.left_shift(jnp.int32(1), 30 - i)
        cnt = jnp.sum(jnp.where(a >= pltpu.bitcast(cand, F32), one, zero), axis=0, keepdims=True)
        return jnp.where(cnt >= cap, cand, thr)

    thr = lax.fori_loop(0, 31, search, jnp.zeros((1, LANES), jnp.int32))
    gt = a >= pltpu.bitcast(thr + 1, F32)
    eq = (a >= pltpu.bitcast(thr, F32)) & jnp.logical_not(gt)
    need = cap - jnp.sum(jnp.where(gt, one, zero), axis=0, keepdims=True)
    ti = lax.broadcasted_iota(jnp.int32, (seq, seq), 0)
    tj = lax.broadcasted_iota(jnp.int32, (seq, seq), 1)
    before = jnp.where(tj < ti, 1.0, 0.0).astype(BF16)
    eq_rank = _dot(before, jnp.where(eq, one, zero).astype(BF16))
    sel = gt | (eq & (eq_rank < need))
    pos = _dot(before, jnp.where(sel, one, zero).astype(BF16))
    slot = jnp.where(sel, pos, -1.0)
    slot_t_ref[0] = slot
    slot_r_ref[0] = slot.T
    aff_r_ref[0] = a.T


def _route(aff_t, cap):
    ngroup, seq, _ = aff_t.shape
    tspec = pl.BlockSpec((1, seq, LANES), lambda g: (g, 0, 0))
    rspec = pl.BlockSpec((1, LANES, seq), lambda g: (g, 0, 0))
    return pl.pallas_call(
        functools.partial(_route_kernel, cap=cap),
        grid=(ngroup,),
        in_specs=[tspec],
        out_specs=[tspec, rspec, rspec],
        out_shape=[
            jax.ShapeDtypeStruct((ngroup, seq, LANES), F32),
            jax.ShapeDtypeStruct((ngroup, LANES, seq), F32),
            jax.ShapeDtypeStruct((ngroup, LANES, seq), F32),
        ],
        compiler_params=_params("arbitrary"),
        name="route",
    )(aff_t)


GATHER_ROWS = 512


def _gather_kernel(slot_ref, aff_ref, h_ref, x_ref, g_ref, p_ref, *, cap):
    seq = h_ref.shape[1]
    nslot = N_EXPERTS * cap
    sub = lax.broadcasted_iota(jnp.int32, (cap, seq), 0).astype(F32)
    one, zero = jnp.ones((cap, seq), F32), jnp.zeros((cap, seq), F32)
    for e in range(N_EXPERTS):
        row = jnp.broadcast_to(slot_ref[0, e:e + 1, :], (cap, seq))
        p_ref[e * cap:(e + 1) * cap, :] = jnp.where(row == sub, one, zero).astype(BF16)
    h = h_ref[0]
    per_chunk = GATHER_ROWS // cap
    for c in range(nslot // GATHER_ROWS):
        rows = slice(c * GATHER_ROWS, (c + 1) * GATHER_ROWS)
        x = _dot(p_ref[rows, :], h).astype(x_ref.dtype)
        for i in range(per_chunk):
            x_ref[c * per_chunk + i] = x[i * cap:(i + 1) * cap]
    a = aff_ref[0]
    a_hi = a.astype(BF16)
    r1 = a - a_hi.astype(F32)
    a_mid = r1.astype(BF16)
    a_lo = (r1 - a_mid.astype(F32)).astype(BF16)
    a3 = jnp.concatenate([a_hi, a_mid, a_lo], axis=0)
    g3 = lax.dot_general(a3, p_ref[...], _NT, preferred_element_type=F32)
    gfull = g3[0:N_EXPERTS] + g3[N_EXPERTS:2 * N_EXPERTS] + g3[2 * N_EXPERTS:3 * N_EXPERTS]
    ei = lax.broadcasted_iota(jnp.int32, (N_EXPERTS, nslot), 0)
    si = lax.broadcasted_iota(jnp.int32, (N_EXPERTS, nslot), 1)
    own = (si >= ei * cap) & (si < (ei + 1) * cap)
    g_ref[0] = jnp.sum(jnp.where(own, gfull, 0.0), axis=0, keepdims=True)


def _gather(slot_r, aff_r, h2, cap):
    nreq, seq, _ = h2.shape
    nslot = N_EXPERTS * cap
    grp = lambda b: (b // REQ_PER_GROUP, b % REQ_PER_GROUP, 0)
    x, g = pl.pallas_call(
        functools.partial(_gather_kernel, cap=cap),
        grid=(nreq,),
        in_specs=[
            pl.BlockSpec((1, N_EXPERTS, seq), grp),
            pl.BlockSpec((1, N_EXPERTS, seq), grp),
            pl.BlockSpec((1, seq, D_MODEL), lambda b: (b, 0, 0)),
        ],
        out_specs=[
            pl.BlockSpec((N_EXPERTS, cap, D_MODEL), lambda b: (0, b, 0)),
            pl.BlockSpec((1, 1, nslot), lambda b: (b, 0, 0)),
        ],
        out_shape=[
            jax.ShapeDtypeStruct((N_EXPERTS, nreq * cap, D_MODEL), BF16),
            jax.ShapeDtypeStruct((nreq, 1, nslot), F32),
        ],
        scratch_shapes=[pltpu.VMEM((nslot, seq), BF16)],
        compiler_params=_params("arbitrary"),
        name="gather",
    )(slot_r, aff_r, h2)
    return x, g


EXPERT_FCHUNK = 512
EXPERT_ROWS = 512
GATE_ROWS = 256


def _expert_kernel(xc_ref, xl_ref, g_ref, wg_ref, wu_ref, wd_ref, yc_ref, yl_ref,
                   acc_ref, wgb_ref, wub_ref, wdb_ref):
    f = pl.program_id(1)
    nf = pl.num_programs(1)
    half = xc_ref.shape[1]
    wgb_ref[...] = wg_ref[0].astype(BF16)
    wub_ref[...] = wu_ref[0].astype(BF16)
    wdb_ref[...] = wd_ref[0].astype(BF16)

    @pl.when(f == 0)
    def _():
        acc_ref[...] = jnp.zeros_like(acc_ref)

    for part, x_ref in enumerate((xc_ref, xl_ref)):
        def tile(i, carry, x_ref=x_ref, part=part):
            r0 = pl.multiple_of(i * EXPERT_ROWS, EXPERT_ROWS)
            x = x_ref[0, pl.ds(r0, EXPERT_ROWS), :]
            gq = _dot(x, wgb_ref[...])
            uq = _dot(x, wub_ref[...])
            hid = (gq * _sigmoid(gq) * uq).astype(BF16)
            acc_ref[pl.ds(part * half + r0, EXPERT_ROWS), :] += _dot(hid, wdb_ref[...])
            return carry
        lax.fori_loop(0, half // EXPERT_ROWS, tile, 0)

    @pl.when(f == nf - 1)
    def _():
        ri = lax.broadcasted_iota(jnp.int32, (GATE_ROWS, GATE_ROWS), 0)
        ci = lax.broadcasted_iota(jnp.int32, (GATE_ROWS, GATE_ROWS), 1)
        for part, y_ref in enumerate((yc_ref, yl_ref)):
            for i in range(half // GATE_ROWS):
                t = part * (half // GATE_ROWS) + i
                grow = jnp.broadcast_to(g_ref[0, t:t + 1, :], (GATE_ROWS, GATE_ROWS))
                gcol = jnp.sum(jnp.where(ri == ci, grow, 0.0), axis=-1, keepdims=True)
                rows = slice(i * GATE_ROWS, (i + 1) * GATE_ROWS)
                y = acc_ref[t * GATE_ROWS:(t + 1) * GATE_ROWS, :] * gcol
                y_ref[0, rows, :] = y.astype(y_ref.dtype)


def _experts(xc, xl, g, w_gate, w_up, w_down):
    ne, half, _ = xc.shape
    fc = EXPERT_FCHUNK
    xspec = pl.BlockSpec((1, half, D_MODEL), lambda e, f: (e, 0, 0))
    return pl.pallas_call(
        _expert_kernel,
        grid=(ne, D_EXPERT // fc),
        in_specs=[
            xspec, xspec,
            pl.BlockSpec((1,) + g.shape[1:], lambda e, f: (e, 0, 0)),
            pl.BlockSpec((1, D_MODEL, fc), lambda e, f: (e, 0, f)),
            pl.BlockSpec((1, D_MODEL, fc), lambda e, f: (e, 0, f)),
            pl.BlockSpec((1, fc, D_MODEL), lambda e, f: (e, f, 0)),
        ],
        out_specs=[xspec, xspec],
        out_shape=[jax.ShapeDtypeStruct(xc.shape, BF16), jax.ShapeDtypeStruct(xl.shape, BF16)],
        scratch_shapes=[
            pltpu.VMEM((2 * half, D_MODEL), F32),
            pltpu.VMEM((D_MODEL, fc), BF16),
            pltpu.VMEM((D_MODEL, fc), BF16),
            pltpu.VMEM((fc, D_MODEL), BF16),
        ],
        compiler_params=_params("arbitrary", "arbitrary"),
        name="experts",
    )(xc, xl, g, w_gate, w_up, w_down)


COMBINE_ROWS = 256


def _combine_kernel(x1_ref, mods_ref, slot_ref, y_ref, gf_ref, o_ref, *, cap):
    b = pl.program_id(0)
    nslot = N_EXPERTS * cap
    li = lax.broadcasted_iota(jnp.int32, (LANES, nslot), 0)
    si = lax.broadcasted_iota(jnp.int32, (LANES, nslot), 1)
    lane0 = (b % REQ_PER_GROUP) * N_EXPERTS
    expand = jnp.where(li == lane0 + si // cap, 1.0, 0.0).astype(BF16)
    slot_exp = _dot(slot_ref[0].astype(BF16), expand)
    want = (lax.broadcasted_iota(jnp.int32, slot_exp.shape, 1) & (cap - 1)).astype(F32)
    scat = jnp.where(slot_exp == want, 1.0, 0.0).astype(BF16)
    moe = _dot(scat, y_ref[...].reshape(nslot, D_MODEL))
    x2 = x1_ref[0] + mods_ref[0][5:6] * moe
    o_ref[0] = x2 * lax.rsqrt(jnp.mean(x2 * x2, axis=-1, keepdims=True) + EPS) * gf_ref[...]


def _combine(x1, mods, mod_row, slot_t, y, gf, cap):
    nreq, seq, _ = x1.shape
    nslot = N_EXPERTS * cap
    tm = COMBINE_ROWS
    tok = lambda b, j: (b, j, 0)
    return pl.pallas_call(
        functools.partial(_combine_kernel, cap=cap),
        grid=(nreq, seq // tm),
        in_specs=[
            pl.BlockSpec((1, tm, D_MODEL), tok),
            pl.BlockSpec((1, N_MOD, D_MODEL), lambda b, j: (mod_row(b), 0, 0)),
            pl.BlockSpec((1, tm, LANES), lambda b, j: (b // REQ_PER_GROUP, j, 0)),
            pl.BlockSpec((N_EXPERTS, cap, D_MODEL), lambda b, j: (0, b, 0)),
            pl.BlockSpec((1, D_MODEL), lambda b, j: (0, 0)),
        ],
        out_specs=pl.BlockSpec((1, tm, D_MODEL), tok),
        out_shape=jax.ShapeDtypeStruct(x1.shape, F32),
        compiler_params=_params("arbitrary", "arbitrary"),
        name="combine",
    )(x1, mods, slot_t, y, gf)


def _group_lanes(aff):
    nreq, seq, ne = aff.shape
    a = aff.reshape(nreq // REQ_PER_GROUP, REQ_PER_GROUP, seq, ne)
    return jnp.transpose(a, (0, 2, 1, 3)).reshape(nreq // REQ_PER_GROUP, seq, REQ_PER_GROUP * ne)


def kernel(x_prompt, x_sample, cache_ctx_k, cache_ctx_v, c, c_ctx, w_ada, b_ada, norm1_g, w_in,
           conv_dw_w, conv_dw_b, conv_ln_g, conv_ln_b, w_conv_pw, na_rpb, w_attn_o, w_out,
           norm2_g, w_router, w_gate_e, w_up_e, w_down_e, final_g):
    assert w_ada.shape[0] == 1, "single trunk layer"
    nctx, ctx_seq, _ = x_prompt.shape
    nlat, lat_seq, _ = x_sample.shape
    ctx_cap = EC_CAPACITY * ctx_seq // N_EXPERTS
    lat_cap = EC_CAPACITY * lat_seq // N_EXPERTS
    assert nctx * ctx_cap == nlat * lat_cap

    cond = jnp.zeros((2 * nlat, D_MODEL), F32).at[:nlat].set(c).at[nlat].set(c_ctx)
    mods = _ada(cond, w_ada[0], b_ada).reshape(2 * nlat, N_MOD, D_MODEL)
    ctx_row = lambda b: nlat
    lat_row = lambda b: b

    w_a = w_g = w_in[0].astype(BF16)
    w_pw = w_conv_pw[0].astype(BF16)
    w_ao = w_attn_o[0].astype(BF16)
    w_o = w_out[0].astype(BF16)
    w_r = w_router[0].astype(BF16)
    conv_args = (conv_dw_w[0], conv_dw_b, conv_ln_g, conv_ln_b)

    q_c, k_c, v_c, cact_c = _front(x_prompt, mods, ctx_row, norm1_g, w_a, *conv_args, F32)
    q_l, k_l, v_l, cact_l = _front(x_sample, mods, lat_row, norm1_g, w_a, *conv_args, BF16)

    heads_c = _ctx_attn(q_c, k_c, v_c)
    tab = _bias_table(na_rpb[0])
    past = cache_ctx_k.shape[2]
    heads_l = _na_attn(q_l, k_l, v_l,
                       cache_ctx_k[:, 0].reshape(nlat, past, D_ATTN).astype(BF16),
                       cache_ctx_v[:, 0].reshape(nlat, past, D_ATTN).astype(BF16), tab)

    merge_w = (w_g, w_pw, w_ao, w_o, w_r)
    flat = lambda t: t.reshape(1, nctx * ctx_seq, t.shape[-1])
    x1_c, h2_c, aff_c = _merge(flat(x_prompt), mods, ctx_row, norm1_g, norm2_g, flat(cact_c),
                               flat(heads_c), *merge_w)
    x1_c, h2_c, aff_c = (t.reshape(nctx, ctx_seq, t.shape[-1]) for t in (x1_c, h2_c, aff_c))
    x1_l, h2_l, aff_l = _merge(x_sample, mods, lat_row, norm1_g, norm2_g, cact_l, heads_l, *merge_w)

    slot_t_c, slot_r_c, aff_r_c = _route(_group_lanes(aff_c), ctx_cap)
    slot_t_l, slot_r_l, aff_r_l = _route(_group_lanes(aff_l), lat_cap)

    xg_c, g_c = _gather(slot_r_c, aff_r_c, h2_c, ctx_cap)
    xg_l, g_l = _gather(slot_r_l, aff_r_l, h2_l, lat_cap)

    def gate_rows(g, nreq, cap):
        g = jnp.transpose(g.reshape(nreq, N_EXPERTS, cap), (1, 0, 2))
        return g.reshape(N_EXPERTS, nreq * cap // GATE_ROWS, GATE_ROWS)

    gates = jnp.concatenate([gate_rows(g_c, nctx, ctx_cap), gate_rows(g_l, nlat, lat_cap)], axis=1)
    y_c, y_l = _experts(xg_c, xg_l, gates, w_gate_e[0], w_up_e[0], w_down_e[0])

    gf = final_g.reshape(1, D_MODEL)
    y_prompt = _combine(x1_c, mods, ctx_row, slot_t_c, y_c, gf, ctx_cap)
    y_sample = _combine(x1_l, mods, lat_row, slot_t_l, y_l, gf, lat_cap)

    state_k = k_c.reshape(nctx, 1, ctx_seq, N_HEADS, HEAD_DIM)
    state_v = v_c.reshape(nctx, 1, ctx_seq, N_HEADS, HEAD_DIM)
    return (y_prompt, y_sample, state_k, state_v)
```

```python
import functools

import jax
import jax.numpy as jnp
from jax import lax
from jax.experimental import pallas as pl
from jax.experimental.pallas import tpu as pltpu

F32 = jnp.float32
BF16 = jnp.bfloat16

D_MODEL = 1024
D_CONV = 512
CONV_WIDTH = 31
N_HEADS = 8
HEAD_DIM = 64
D_ATTN = N_HEADS * HEAD_DIM
GRID_W = 64
NA_ROWS = 8
NA_COLS = 16
N_EXPERTS = 16
D_EXPERT = 2048
EC_CAPACITY = 2
N_MOD = 6
EPS = 1e-6
D_PROJ_A = 2 * D_CONV + 3 * D_ATTN
D_IN = D_PROJ_A + 2 * D_MODEL
LANES = 128
REQ_PER_GROUP = LANES // N_EXPERTS
MASK_NEG = -1e30
VMEM_LIMIT = 56 * 1024 * 1024

_NT = (((1,), (1,)), ((), ()))


def _params(*sem):
    return pltpu.CompilerParams(dimension_semantics=sem, vmem_limit_bytes=VMEM_LIMIT)


def _dot(a, b):
    return jnp.dot(a, b, preferred_element_type=F32)


def _sigmoid(x):
    return 1.0 / (1.0 + jnp.exp(-x))


def _split_bf16(x):
    hi = x.astype(BF16)
    lo = (x - hi.astype(F32)).astype(BF16)
    return hi, lo


def _rms_mod(x, g, shift, scale):
    xn = x * lax.rsqrt(jnp.mean(x * x, axis=-1, keepdims=True) + EPS) * g
    return xn * (1.0 + scale) + shift


def _ada_kernel(cond_ref, w_ref, b_ref, o_ref):
    c = cond_ref[...]
    a = c * _sigmoid(c)
    ah, al = _split_bf16(a)
    wh, wl = _split_bf16(w_ref[...])
    o_ref[...] = _dot(ah, wh) + _dot(al, wh) + _dot(ah, wl) + b_ref[...]


def _ada(cond, w_ada, b_ada):
    n = w_ada.shape[1]
    bn = 1024
    return pl.pallas_call(
        _ada_kernel,
        grid=(n // bn,),
        in_specs=[
            pl.BlockSpec(cond.shape, lambda j: (0, 0)),
            pl.BlockSpec((D_MODEL, bn), lambda j: (0, j)),
            pl.BlockSpec((1, bn), lambda j: (0, j)),
        ],
        out_specs=pl.BlockSpec((cond.shape[0], bn), lambda j: (0, j)),
        out_shape=jax.ShapeDtypeStruct((cond.shape[0], n), F32),
        compiler_params=_params("arbitrary"),
        name="ada",
    )(cond, w_ada, b_ada)


CONV_PAD = 16
FRONT_ROWS = 256
CONV_ROWS = 64
SUBLANES = 8


def _front_kernel(x_ref, mods_ref, g1_ref, w_ref, dww_ref, dwb_ref, lng_ref, lnb_ref,
                  q_ref, k_ref, v_ref, c_ref, gpad_ref, *, seq):
    m = mods_ref[0]
    shift, scale = m[0:1], m[1:2]
    zeros = jnp.zeros((CONV_PAD, D_CONV), F32)
    gpad_ref[0:CONV_PAD, :] = zeros
    gpad_ref[CONV_PAD + seq:CONV_PAD + seq + CONV_PAD, :] = zeros

    def proj(i, carry):
        r0 = pl.multiple_of(i * FRONT_ROWS, FRONT_ROWS)
        x = x_ref[0, pl.ds(r0, FRONT_ROWS), :]
        h = _rms_mod(x, g1_ref[...], shift, scale).astype(BF16)
        u = _dot(h, w_ref[...])
        glu = u[:, :D_CONV] * _sigmoid(u[:, D_CONV:2 * D_CONV])
        gpad_ref[pl.ds(CONV_PAD + r0, FRONT_ROWS), :] = glu
        o = 2 * D_CONV
        q_ref[0, pl.ds(r0, FRONT_ROWS), :] = (u[:, o:o + D_ATTN] * (HEAD_DIM ** -0.5)).astype(q_ref.dtype)
        k_ref[0, pl.ds(r0, FRONT_ROWS), :] = u[:, o + D_ATTN:o + 2 * D_ATTN].astype(k_ref.dtype)
        v_ref[0, pl.ds(r0, FRONT_ROWS), :] = u[:, o + 2 * D_ATTN:o + 3 * D_ATTN].astype(v_ref.dtype)
        return carry

    lax.fori_loop(0, seq // FRONT_ROWS, proj, 0)

    def conv(i, carry):
        r0 = pl.multiple_of(i * CONV_ROWS, CONV_ROWS)
        cols = []
        for cb in range(D_CONV // LANES):
            cs = slice(cb * LANES, (cb + 1) * LANES)
            win = gpad_ref[pl.ds(r0, CONV_ROWS + 2 * CONV_PAD), cs]
            acc = None
            for s in range(SUBLANES):
                nrow = CONV_ROWS + (SUBLANES if s else 0)
                z = None
                for a in range(2 * CONV_PAD // SUBLANES):
                    t = SUBLANES * a + s - (CONV_PAD - CONV_WIDTH // 2)
                    if 0 <= t < CONV_WIDTH:
                        term = win[SUBLANES * a:SUBLANES * a + nrow] * dww_ref[t:t + 1, cs]
                        z = term if z is None else z + term
                z = z[s:s + CONV_ROWS]
                acc = z if acc is None else acc + z
            cols.append(acc)
        y = jnp.concatenate(cols, axis=1) + dwb_ref[...]
        mu = jnp.mean(y, axis=-1, keepdims=True)
        yc = y - mu
        var = jnp.mean(yc * yc, axis=-1, keepdims=True)
        z = yc * lax.rsqrt(var + EPS) * lng_ref[...] + lnb_ref[...]
        c_ref[0, pl.ds(r0, CONV_ROWS), :] = (z * _sigmoid(z)).astype(c_ref.dtype)
        return carry

    lax.fori_loop(0, seq // CONV_ROWS, conv, 0)


def _front(x, mods, mod_row, g1, w_a, dww, dwb, lng, lnb, kv_dtype):
    nreq, seq, _ = x.shape
    const2 = lambda b: (0, 0)
    tok = lambda b: (b, 0, 0)
    out_tok = pl.BlockSpec((1, seq, D_ATTN), tok)
    return pl.pallas_call(
        functools.partial(_front_kernel, seq=seq),
        grid=(nreq,),
        in_specs=[
            pl.BlockSpec((1, seq, D_MODEL), tok),
            pl.BlockSpec((1, N_MOD, D_MODEL), lambda b: (mod_row(b), 0, 0)),
            pl.BlockSpec((1, D_MODEL), const2),
            pl.BlockSpec((D_MODEL, D_PROJ_A), const2),
            pl.BlockSpec((CONV_WIDTH, D_CONV), const2),
            pl.BlockSpec((1, D_CONV), const2),
            pl.BlockSpec((1, D_CONV), const2),
            pl.BlockSpec((1, D_CONV), const2),
        ],
        out_specs=[out_tok, out_tok, out_tok, out_tok],
        out_shape=[
            jax.ShapeDtypeStruct((nreq, seq, D_ATTN), BF16),
            jax.ShapeDtypeStruct((nreq, seq, D_ATTN), kv_dtype),
            jax.ShapeDtypeStruct((nreq, seq, D_ATTN), kv_dtype),
            jax.ShapeDtypeStruct((nreq, seq, D_CONV), BF16),
        ],
        scratch_shapes=[pltpu.VMEM((seq + 2 * CONV_PAD, D_CONV), F32)],
        compiler_params=_params("arbitrary"),
        name="front",
    )(x, mods, g1, w_a, dww, dwb, lng, lnb)


def _pair_scores_q(q_pair):
    lane = lax.broadcasted_iota(jnp.int32, q_pair.shape, 1)
    zero = jnp.zeros_like(q_pair)
    return jnp.concatenate([jnp.where(lane < HEAD_DIM, q_pair, zero),
                            jnp.where(lane >= HEAD_DIM, q_pair, zero)], axis=0)


def _pair_merge(o2, t):
    lane = lax.broadcasted_iota(jnp.int32, (t, LANES), 1)
    return jnp.where(lane < HEAD_DIM, o2[:t], o2[t:])


def _ctx_attn_kernel(q_ref, k_ref, v_ref, o_ref):
    seq = q_ref.shape[1]
    pairs = range(N_HEADS // 2)
    sls = [slice(p * LANES, (p + 1) * LANES) for p in pairs]
    s = [lax.dot_general(_pair_scores_q(q_ref[0, :, sls[p]]), k_ref[0, :, sls[p]].astype(BF16), _NT,
                         preferred_element_type=F32) for p in pairs]
    e = [jnp.exp(s[p] - jnp.max(s[p], axis=-1, keepdims=True)) for p in pairs]
    den = [jnp.sum(e[p], axis=-1, keepdims=True) for p in pairs]
    for p in pairs:
        o2 = _dot(e[p].astype(BF16), v_ref[0, :, sls[p]].astype(BF16)) / den[p]
        o_ref[0, :, sls[p]] = _pair_merge(o2, seq).astype(o_ref.dtype)


def _ctx_attn(q, k, v):
    nreq, seq, _ = q.shape
    spec = pl.BlockSpec((1, seq, D_ATTN), lambda b: (b, 0, 0))
    return pl.pallas_call(
        _ctx_attn_kernel,
        grid=(nreq,),
        in_specs=[spec, spec, spec],
        out_specs=spec,
        out_shape=jax.ShapeDtypeStruct((nreq, seq, D_ATTN), BF16),
        compiler_params=_params("arbitrary"),
        name="ctx_attn",
    )(q, k, v)


N_DR = 2 * NA_ROWS - 1
N_DC = 2 * NA_COLS - 1


def _bias_kernel(rpb_ref, o_ref, t2_ref):
    h = pl.program_id(0)
    qi = lax.broadcasted_iota(jnp.int32, (GRID_W, LANES), 0)
    lane = lax.broadcasted_iota(jnp.int32, (GRID_W, LANES), 1)
    kc = lane & (GRID_W - 1)
    d = kc - qi
    cs = jnp.clip(qi - NA_COLS // 2, 0, GRID_W - NA_COLS)
    inside = (kc >= cs) & (kc < cs + NA_COLS)
    for dr in range(N_DR):
        t = jnp.full((GRID_W, LANES), MASK_NEG, F32)
        for j in range(N_DC):
            val = rpb_ref[h * (N_DR * N_DC) + dr * N_DC + j]
            t = jnp.where(d == j - (NA_COLS - 1), val, t)
        t2_ref[dr] = jnp.where(inside, t, MASK_NEG)
    for o in range(NA_ROWS):
        for jj in range(NA_ROWS // 2):
            o_ref[0, o, :, jj * LANES:(jj + 1) * LANES] = jnp.where(
                lane < GRID_W, t2_ref[o + 2 * jj], t2_ref[o + 2 * jj + 1])


def _bias_table(rpb):
    return pl.pallas_call(
        _bias_kernel,
        grid=(N_HEADS,),
        in_specs=[pl.BlockSpec(memory_space=pltpu.SMEM)],
        out_specs=pl.BlockSpec((1, NA_ROWS, GRID_W, NA_ROWS * GRID_W), lambda h: (h, 0, 0, 0)),
        out_shape=jax.ShapeDtypeStruct((N_HEADS, NA_ROWS, GRID_W, NA_ROWS * GRID_W), F32),
        scratch_shapes=[pltpu.VMEM((N_DR, GRID_W, LANES), F32)],
        compiler_params=_params("arbitrary"),
        name="na_bias",
    )(rpb.reshape(-1))


def _na_attn_kernel(q_ref, k_ref, v_ref, kc_ref, vc_ref, tab_ref, o_ref, *, rows):
    r = pl.program_id(1)
    rs = jnp.clip(r - NA_ROWS // 2, 0, rows - NA_ROWS)
    k0 = pl.multiple_of(rs * GRID_W, GRID_W)
    off = rs - r + (NA_ROWS - 1)
    nloc = NA_ROWS * GRID_W
    pairs = range(N_HEADS // 2)
    sls = [slice(p * LANES, (p + 1) * LANES) for p in pairs]
    s_loc, s_ctx = [], []
    for p in pairs:
        q2 = _pair_scores_q(q_ref[0, :, sls[p]])
        bias = jnp.concatenate([tab_ref[2 * p, off], tab_ref[2 * p + 1, off]], axis=0)
        kl = k_ref[0, pl.ds(k0, nloc), sls[p]]
        s_loc.append(lax.dot_general(q2, kl, _NT, preferred_element_type=F32) + bias)
        s_ctx.append(lax.dot_general(q2, kc_ref[0, :, sls[p]], _NT, preferred_element_type=F32))
    mx = [jnp.maximum(jnp.max(s_loc[p], axis=-1, keepdims=True),
                      jnp.max(s_ctx[p], axis=-1, keepdims=True)) for p in pairs]
    e_loc = [jnp.exp(s_loc[p] - mx[p]) for p in pairs]
    e_ctx = [jnp.exp(s_ctx[p] - mx[p]) for p in pairs]
    den = [jnp.sum(e_loc[p], axis=-1, keepdims=True) + jnp.sum(e_ctx[p], axis=-1, keepdims=True)
           for p in pairs]
    for p in pairs:
        vl = v_ref[0, pl.ds(k0, nloc), sls[p]]
        o2 = (_dot(e_loc[p].astype(BF16), vl) + _dot(e_ctx[p].astype(BF16), vc_ref[0, :, sls[p]])) / den[p]
        o_ref[0, :, sls[p]] = _pair_merge(o2, GRID_W).astype(o_ref.dtype)


def _na_attn(q, k, v, kc, vc, tab):
    nreq, seq, _ = q.shape
    rows = seq // GRID_W
    past = kc.shape[1]
    qspec = pl.BlockSpec((1, GRID_W, D_ATTN), lambda b, r: (b, r, 0))
    kvspec = pl.BlockSpec((1, seq, D_ATTN), lambda b, r: (b, 0, 0))
    cspec = pl.BlockSpec((1, past, D_ATTN), lambda b, r: (b, 0, 0))
    return pl.pallas_call(
        functools.partial(_na_attn_kernel, rows=rows),
        grid=(nreq, rows),
        in_specs=[qspec, kvspec, kvspec, cspec, cspec,
                  pl.BlockSpec(tab.shape, lambda b, r: (0, 0, 0, 0))],
        out_specs=qspec,
        out_shape=jax.ShapeDtypeStruct((nreq, seq, D_ATTN), BF16),
        compiler_params=_params("arbitrary", "arbitrary"),
        name="na_attn",
    )(q, k, v, kc, vc, tab)


MERGE_ROWS = 512
MERGE_SUB = 256


def _merge_kernel(x_ref, mods_ref, g1_ref, g2_ref, c_ref, a_ref, wg_ref, wpw_ref, wao_ref,
                  wout_ref, wr_ref, x1_ref, h2_ref, aff_ref):
    m = mods_ref[0]
    subs = range(MERGE_ROWS // MERGE_SUB)
    rows = [slice(i * MERGE_SUB, (i + 1) * MERGE_SUB) for i in subs]
    x = [x_ref[0, rows[i], :] for i in subs]
    h = [_rms_mod(x[i], g1_ref[...], m[0:1], m[1:2]).astype(BF16) for i in subs]
    gates = [_sigmoid(_dot(h[i], wg_ref[:, D_PROJ_A - D_IN // 2:])) for i in subs]
    conv_out = [_dot(c_ref[0, rows[i], :], wpw_ref[...]) for i in subs]
    attn_out = [_dot(a_ref[0, rows[i], :], wao_ref[...]) for i in subs]
    merged = [(gates[i][:, :D_MODEL] * conv_out[i] + gates[i][:, D_MODEL:] * attn_out[i]).astype(BF16)
              for i in subs]
    x1 = [x[i] + m[2:3] * _dot(merged[i], wout_ref[...]) for i in subs]
    h2 = [_rms_mod(x1[i], g2_ref[...], m[3:4], m[4:5]).astype(BF16) for i in subs]
    logits = [_dot(h2[i], wr_ref[...]) for i in subs]
    e = [jnp.exp(logits[i] - jnp.max(logits[i], axis=-1, keepdims=True)) for i in subs]
    for i in subs:
        x1_ref[0, rows[i], :] = x1[i]
        h2_ref[0, rows[i], :] = h2[i]
        aff_ref[0, rows[i], :] = e[i] / jnp.sum(e[i], axis=-1, keepdims=True)


def _merge(x, mods, mod_row, g1, g2, cact, heads, w_g, w_pw, w_ao, w_out, w_r):
    nreq, seq, _ = x.shape
    tm = MERGE_ROWS
    tok = lambda b, j: (b, j, 0)
    const2 = lambda b, j: (0, 0)
    return pl.pallas_call(
        _merge_kernel,
        grid=(nreq, seq // tm),
        in_specs=[
            pl.BlockSpec((1, tm, D_MODEL), tok),
            pl.BlockSpec((1, N_MOD, D_MODEL), lambda b, j: (mod_row(b), 0, 0)),
            pl.BlockSpec((1, D_MODEL), const2),
            pl.BlockSpec((1, D_MODEL), const2),
            pl.BlockSpec((1, tm, D_CONV), tok),
            pl.BlockSpec((1, tm, D_ATTN), tok),
            pl.BlockSpec((D_MODEL, D_IN // 2), lambda b, j: (0, 1)),
            pl.BlockSpec(w_pw.shape, const2),
            pl.BlockSpec(w_ao.shape, const2),
            pl.BlockSpec(w_out.shape, const2),
            pl.BlockSpec(w_r.shape, const2),
        ],
        out_specs=[
            pl.BlockSpec((1, tm, D_MODEL), tok),
            pl.BlockSpec((1, tm, D_MODEL), tok),
            pl.BlockSpec((1, tm, N_EXPERTS), tok),
        ],
        out_shape=[
            jax.ShapeDtypeStruct((nreq, seq, D_MODEL), F32),
            jax.ShapeDtypeStruct((nreq, seq, D_MODEL), BF16),
            jax.ShapeDtypeStruct((nreq, seq, N_EXPERTS), F32),
        ],
        compiler_params=_params("arbitrary", "arbitrary"),
        name="merge",
    )(x, mods, g1, g2, cact, heads, w_g, w_pw, w_ao, w_out, w_r)


def _route_kernel(aff_ref, slot_t_ref, slot_r_ref, aff_r_ref, *, cap):
    a = aff_ref[0]
    seq = a.shape[0]
    one, zero = jnp.ones_like(a), jnp.zeros_like(a)

    def search(i, thr):
        cand = thr | jnp.left_shift(jnp.int32(1), 30 - i)
        cnt = jnp.sum(jnp.where(a >= pltpu.bitcast(cand, F32), one, zero), axis=0, keepdims=True)
        return jnp.where(cnt >= cap, cand, thr)

    thr = lax.fori_loop(0, 31, search, jnp.zeros((1, LANES), jnp.int32))
    gt = a >= pltpu.bitcast(thr + 1, F32)
    eq = (a >= pltpu.bitcast(thr, F32)) & jnp.logical_not(gt)
    need = cap - jnp.sum(jnp.where(gt, one, zero), axis=0, keepdims=True)
    ti = lax.broadcasted_iota(jnp.int32, (seq, seq), 0)
    tj = lax.broadcasted_iota(jnp.int32, (seq, seq), 1)
    before = jnp.where(tj < ti, 1.0, 0.0).astype(BF16)
    eq_rank = _dot(before, jnp.where(eq, one, zero).astype(BF16))
    sel = gt | (eq & (eq_rank < need))
    pos = _dot(before, jnp.where(sel, one, zero).astype(BF16))
    slot = jnp.where(sel, pos, -1.0)
    slot_t_ref[0] = slot
    slot_r_ref[0] = slot.T
    aff_r_ref[0] = a.T


def _route(aff_t, cap):
    ngroup, seq, _ = aff_t.shape
    tspec = pl.BlockSpec((1, seq, LANES), lambda g: (g, 0, 0))
    rspec = pl.BlockSpec((1, LANES, seq), lambda g: (g, 0, 0))
    return pl.pallas_call(
        functools.partial(_route_kernel, cap=cap),
        grid=(ngroup,),
        in_specs=[tspec],
        out_specs=[tspec, rspec, rspec],
        out_shape=[
            jax.ShapeDtypeStruct((ngroup, seq, LANES), F32),
            jax.ShapeDtypeStruct((ngroup, LANES, seq), F32),
            jax.ShapeDtypeStruct((ngroup, LANES, seq), F32),
        ],
        compiler_params=_params("arbitrary"),
        name="route",
    )(aff_t)


GATHER_ROWS = 512


def _gather_kernel(slot_ref, aff_ref, h_ref, x_ref, g_ref, p_ref, *, cap):
    seq = h_ref.shape[1]
    nslot = N_EXPERTS * cap
    sub = lax.broadcasted_iota(jnp.int32, (cap, seq), 0).astype(F32)
    one, zero = jnp.ones((cap, seq), F32), jnp.zeros((cap, seq), F32)
    for e in range(N_EXPERTS):
        row = jnp.broadcast_to(slot_ref[0, e:e + 1, :], (cap, seq))
        p_ref[e * cap:(e + 1) * cap, :] = jnp.where(row == sub, one, zero).astype(BF16)
    h = h_ref[0]
    per_chunk = GATHER_ROWS // cap
    for c in range(nslot // GATHER_ROWS):
        rows = slice(c * GATHER_ROWS, (c + 1) * GATHER_ROWS)
        x = _dot(p_ref[rows, :], h).astype(x_ref.dtype)
        for i in range(per_chunk):
            x_ref[c * per_chunk + i] = x[i * cap:(i + 1) * cap]
    a = aff_ref[0]
    a_hi = a.astype(BF16)
    r1 = a - a_hi.astype(F32)
    a_mid = r1.astype(BF16)
    a_lo = (r1 - a_mid.astype(F32)).astype(BF16)
    a3 = jnp.concatenate([a_hi, a_mid, a_lo], axis=0)
    g3 = lax.dot_general(a3, p_ref[...], _NT, preferred_element_type=F32)
    gfull = g3[0:N_EXPERTS] + g3[N_EXPERTS:2 * N_EXPERTS] + g3[2 * N_EXPERTS:3 * N_EXPERTS]
    ei = lax.broadcasted_iota(jnp.int32, (N_EXPERTS, nslot), 0)
    si = lax.broadcasted_iota(jnp.int32, (N_EXPERTS, nslot), 1)
    own = (si >= ei * cap) & (si < (ei + 1) * cap)
    g_ref[0] = jnp.sum(jnp.where(own, gfull, 0.0), axis=0, keepdims=True)


def _gather(slot_r, aff_r, h2, cap):
    nreq, seq, _ = h2.shape
    nslot = N_EXPERTS * cap
    grp = lambda b: (b // REQ_PER_GROUP, b % REQ_PER_GROUP, 0)
    x, g = pl.pallas_call(
        functools.partial(_gather_kernel, cap=cap),
        grid=(nreq,),
        in_specs=[
            pl.BlockSpec((1, N_EXPERTS, seq), grp),
            pl.BlockSpec((1, N_EXPERTS, seq), grp),
            pl.BlockSpec((1, seq, D_MODEL), lambda b: (b, 0, 0)),
        ],
        out_specs=[
            pl.BlockSpec((N_EXPERTS, cap, D_MODEL), lambda b: (0, b, 0)),
            pl.BlockSpec((1, 1, nslot), lambda b: (b, 0, 0)),
        ],
        out_shape=[
            jax.ShapeDtypeStruct((N_EXPERTS, nreq * cap, D_MODEL), BF16),
            jax.ShapeDtypeStruct((nreq, 1, nslot), F32),
        ],
        scratch_shapes=[pltpu.VMEM((nslot, seq), BF16)],
        compiler_params=_params("arbitrary"),
        name="gather",
    )(slot_r, aff_r, h2)
    return x, g


EXPERT_FCHUNK = 512
EXPERT_ROWS = 512
GATE_ROWS = 256


def _expert_kernel(xc_ref, xl_ref, g_ref, wg_ref, wu_ref, wd_ref, yc_ref, yl_ref,
                   acc_ref, wgb_ref, wub_ref, wdb_ref):
    f = pl.program_id(1)
    nf = pl.num_programs(1)
    half = xc_ref.shape[1]
    wgb_ref[...] = wg_ref[0].astype(BF16)
    wub_ref[...] = wu_ref[0].astype(BF16)
    wdb_ref[...] = wd_ref[0].astype(BF16)

    @pl.when(f == 0)
    def _():
        acc_ref[...] = jnp.zeros_like(acc_ref)

    for part, x_ref in enumerate((xc_ref, xl_ref)):
        def tile(i, carry, x_ref=x_ref, part=part):
            r0 = pl.multiple_of(i * EXPERT_ROWS, EXPERT_ROWS)
            x = x_ref[0, pl.ds(r0, EXPERT_ROWS), :]
            gq = _dot(x, wgb_ref[...])
            uq = _dot(x, wub_ref[...])
            hid = (gq * _sigmoid(gq) * uq).astype(BF16)
            acc_ref[pl.ds(part * half + r0, EXPERT_ROWS), :] += _dot(hid, wdb_ref[...])
            return carry
        lax.fori_loop(0, half // EXPERT_ROWS, tile, 0, unroll=True)

    @pl.when(f == nf - 1)
    def _():
        ri = lax.broadcasted_iota(jnp.int32, (GATE_ROWS, GATE_ROWS), 0)
        ci = lax.broadcasted_iota(jnp.int32, (GATE_ROWS, GATE_ROWS), 1)
        for part, y_ref in enumerate((yc_ref, yl_ref)):
            for i in range(half // GATE_ROWS):
                t = part * (half // GATE_ROWS) + i
                grow = jnp.broadcast_to(g_ref[0, t:t + 1, :], (GATE_ROWS, GATE_ROWS))
                gcol = jnp.sum(jnp.where(ri == ci, grow, 0.0), axis=-1, keepdims=True)
                rows = slice(i * GATE_ROWS, (i + 1) * GATE_ROWS)
                y = acc_ref[t * GATE_ROWS:(t + 1) * GATE_ROWS, :] * gcol
                y_ref[0, rows, :] = y.astype(y_ref.dtype)


def _experts(xc, xl, g, w_gate, w_up, w_down):
    ne, half, _ = xc.shape
    fc = EXPERT_FCHUNK
    xspec = pl.BlockSpec((1, half, D_MODEL), lambda e, f: (e, 0, 0))
    return pl.pallas_call(
        _expert_kernel,
        grid=(ne, D_EXPERT // fc),
        in_specs=[
            xspec, xspec,
            pl.BlockSpec((1,) + g.shape[1:], lambda e, f: (e, 0, 0)),
            pl.BlockSpec((1, D_MODEL, fc), lambda e, f: (e, 0, f)),
            pl.BlockSpec((1, D_MODEL, fc), lambda e, f: (e, 0, f)),
            pl.BlockSpec((1, fc, D_MODEL), lambda e, f: (e, f, 0)),
        ],
        out_specs=[xspec, xspec],
        out_shape=[jax.ShapeDtypeStruct(xc.shape, BF16), jax.ShapeDtypeStruct(xl.shape, BF16)],
        scratch_shapes=[
            pltpu.VMEM((2 * half, D_MODEL), F32),
            pltpu.VMEM((D_MODEL, fc), BF16),
            pltpu.VMEM((D_MODEL, fc), BF16),
            pltpu.VMEM((fc, D_MODEL), BF16),
        ],
        compiler_params=_params("arbitrary", "arbitrary"),
        name="experts",
    )(xc, xl, g, w_gate, w_up, w_down)


COMBINE_ROWS = 256


def _combine_kernel(x1_ref, mods_ref, slot_ref, y_ref, gf_ref, o_ref, *, cap):
    b = pl.program_id(0)
    nslot = N_EXPERTS * cap
    li = lax.broadcasted_iota(jnp.int32, (LANES, nslot), 0)
    si = lax.broadcasted_iota(jnp.int32, (LANES, nslot), 1)
    lane0 = (b % REQ_PER_GROUP) * N_EXPERTS
    expand = jnp.where(li == lane0 + si // cap, 1.0, 0.0).astype(BF16)
    slot_exp = _dot(slot_ref[0].astype(BF16), expand)
    want = (lax.broadcasted_iota(jnp.int32, slot_exp.shape, 1) & (cap - 1)).astype(F32)
    scat = jnp.where(slot_exp == want, 1.0, 0.0).astype(BF16)
    moe = _dot(scat, y_ref[...].reshape(nslot, D_MODEL))
    x2 = x1_ref[0] + mods_ref[0][5:6] * moe
    o_ref[0] = x2 * lax.rsqrt(jnp.mean(x2 * x2, axis=-1, keepdims=True) + EPS) * gf_ref[...]


def _combine(x1, mods, mod_row, slot_t, y, gf, cap):
    nreq, seq, _ = x1.shape
    nslot = N_EXPERTS * cap
    tm = COMBINE_ROWS
    tok = lambda b, j: (b, j, 0)
    return pl.pallas_call(
        functools.partial(_combine_kernel, cap=cap),
        grid=(nreq, seq // tm),
        in_specs=[
            pl.BlockSpec((1, tm, D_MODEL), tok),
            pl.BlockSpec((1, N_MOD, D_MODEL), lambda b, j: (mod_row(b), 0, 0)),
            pl.BlockSpec((1, tm, LANES), lambda b, j: (b // REQ_PER_GROUP, j, 0)),
            pl.BlockSpec((N_EXPERTS, cap, D_MODEL), lambda b, j: (0, b, 0)),
            pl.BlockSpec((1, D_MODEL), lambda b, j: (0, 0)),
        ],
        out_specs=pl.BlockSpec((1, tm, D_MODEL), tok),
        out_shape=jax.ShapeDtypeStruct(x1.shape, F32),
        compiler_params=_params("arbitrary", "arbitrary"),
        name="combine",
    )(x1, mods, slot_t, y, gf)


def _group_lanes(aff):
    nreq, seq, ne = aff.shape
    a = aff.reshape(nreq // REQ_PER_GROUP, REQ_PER_GROUP, seq, ne)
    return jnp.transpose(a, (0, 2, 1, 3)).reshape(nreq // REQ_PER_GROUP, seq, REQ_PER_GROUP * ne)


def kernel(x_prompt, x_sample, cache_ctx_k, cache_ctx_v, c, c_ctx, w_ada, b_ada, norm1_g, w_in,
           conv_dw_w, conv_dw_b, conv_ln_g, conv_ln_b, w_conv_pw, na_rpb, w_attn_o, w_out,
           norm2_g, w_router, w_gate_e, w_up_e, w_down_e, final_g):
    assert w_ada.shape[0] == 1, "single trunk layer"
    nctx, ctx_seq, _ = x_prompt.shape
    nlat, lat_seq, _ = x_sample.shape
    ctx_cap = EC_CAPACITY * ctx_seq // N_EXPERTS
    lat_cap = EC_CAPACITY * lat_seq // N_EXPERTS
    assert nctx * ctx_cap == nlat * lat_cap

    cond = jnp.zeros((2 * nlat, D_MODEL), F32).at[:nlat].set(c).at[nlat].set(c_ctx)
    mods = _ada(cond, w_ada[0], b_ada).reshape(2 * nlat, N_MOD, D_MODEL)
    ctx_row = lambda b: nlat
    lat_row = lambda b: b

    w_a = w_g = w_in[0].astype(BF16)
    w_pw = w_conv_pw[0].astype(BF16)
    w_ao = w_attn_o[0].astype(BF16)
    w_o = w_out[0].astype(BF16)
    w_r = w_router[0].astype(BF16)
    conv_args = (conv_dw_w[0], conv_dw_b, conv_ln_g, conv_ln_b)

    q_c, k_c, v_c, cact_c = _front(x_prompt, mods, ctx_row, norm1_g, w_a, *conv_args, F32)
    q_l, k_l, v_l, cact_l = _front(x_sample, mods, lat_row, norm1_g, w_a, *conv_args, BF16)

    heads_c = _ctx_attn(q_c, k_c, v_c)
    tab = _bias_table(na_rpb[0])
    past = cache_ctx_k.shape[2]
    heads_l = _na_attn(q_l, k_l, v_l,
                       cache_ctx_k[:, 0].reshape(nlat, past, D_ATTN).astype(BF16),
                       cache_ctx_v[:, 0].reshape(nlat, past, D_ATTN).astype(BF16), tab)

    merge_w = (w_g, w_pw, w_ao, w_o, w_r)
    flat = lambda t: t.reshape(1, nctx * ctx_seq, t.shape[-1])
    x1_c, h2_c, aff_c = _merge(flat(x_prompt), mods, ctx_row, norm1_g, norm2_g, flat(cact_c),
                               flat(heads_c), *merge_w)
    x1_c, h2_c, aff_c = (t.reshape(nctx, ctx_seq, t.shape[-1]) for t in (x1_c, h2_c, aff_c))
    x1_l, h2_l, aff_l = _merge(x_sample, mods, lat_row, norm1_g, norm2_g, cact_l, heads_l, *merge_w)

    slot_t_c, slot_r_c, aff_r_c = _route(_group_lanes(aff_c), ctx_cap)
    slot_t_l, slot_r_l, aff_r_l = _route(_group_lanes(aff_l), lat_cap)

    xg_c, g_c = _gather(slot_r_c, aff_r_c, h2_c, ctx_cap)
    xg_l, g_l = _gather(slot_r_l, aff_r_l, h2_l, lat_cap)

    def gate_rows(g, nreq, cap):
        g = jnp.transpose(g.reshape(nreq, N_EXPERTS, cap), (1, 0, 2))
        return g.reshape(N_EXPERTS, nreq * cap // GATE_ROWS, GATE_ROWS)

    gates = jnp.concatenate([gate_rows(g_c, nctx, ctx_cap), gate_rows(g_l, nlat, lat_cap)], axis=1)
    y_c, y_l = _experts(xg_c, xg_l, gates, w_gate_e[0], w_up_e[0], w_down_e[0])

    gf = final_g.reshape(1, D_MODEL)
    y_prompt = _combine(x1_c, mods, ctx_row, slot_t_c, y_c, gf, ctx_cap)
    y_sample = _combine(x1_l, mods, lat_row, slot_t_l, y_l, gf, lat_cap)

    state_k = k_c.reshape(nctx, 1, ctx_seq, N_HEADS, HEAD_DIM)
    state_v = v_c.reshape(nctx, 1, ctx_seq, N_HEADS, HEAD_DIM)
    return (y_prompt, y_sample, state_k, state_v)
```

```python
import functools

import jax
import jax.numpy as jnp
from jax import lax
from jax.experimental import pallas as pl
from jax.experimental.pallas import tpu as pltpu

F32 = jnp.float32
BF16 = jnp.bfloat16

D_MODEL = 1024
D_CONV = 512
CONV_WIDTH = 31
N_HEADS = 8
HEAD_DIM = 64
D_ATTN = N_HEADS * HEAD_DIM
GRID_W = 64
NA_ROWS = 8
NA_COLS = 16
N_EXPERTS = 16
D_EXPERT = 2048
EC_CAPACITY = 2
N_MOD = 6
EPS = 1e-6
D_PROJ_A = 2 * D_CONV + 3 * D_ATTN
D_IN = D_PROJ_A + 2 * D_MODEL
LANES = 128
REQ_PER_GROUP = LANES // N_EXPERTS
MASK_NEG = -1e30
VMEM_LIMIT = 56 * 1024 * 1024

_NT = (((1,), (1,)), ((), ()))


def _params(*sem):
    return pltpu.CompilerParams(dimension_semantics=sem, vmem_limit_bytes=VMEM_LIMIT)


def _dot(a, b):
    return jnp.dot(a, b, preferred_element_type=F32)


def _sigmoid(x):
    return 1.0 / (1.0 + jnp.exp(-x))


def _split_bf16(x):
    hi = x.astype(BF16)
    lo = (x - hi.astype(F32)).astype(BF16)
    return hi, lo


def _rms_mod(x, g, shift, scale):
    xn = x * lax.rsqrt(jnp.mean(x * x, axis=-1, keepdims=True) + EPS) * g
    return xn * (1.0 + scale) + shift


def _ada_kernel(cond_ref, w_ref, b_ref, o_ref):
    c = cond_ref[...]
    a = c * _sigmoid(c)
    ah, al = _split_bf16(a)
    wh, wl = _split_bf16(w_ref[...])
    o_ref[...] = _dot(ah, wh) + _dot(al, wh) + _dot(ah, wl) + b_ref[...]


def _ada(cond, w_ada, b_ada):
    n = w_ada.shape[1]
    bn = 1024
    return pl.pallas_call(
        _ada_kernel,
        grid=(n // bn,),
        in_specs=[
            pl.BlockSpec(cond.shape, lambda j: (0, 0)),
            pl.BlockSpec((D_MODEL, bn), lambda j: (0, j)),
            pl.BlockSpec((1, bn), lambda j: (0, j)),
        ],
        out_specs=pl.BlockSpec((cond.shape[0], bn), lambda j: (0, j)),
        out_shape=jax.ShapeDtypeStruct((cond.shape[0], n), F32),
        compiler_params=_params("arbitrary"),
        name="ada",
    )(cond, w_ada, b_ada)


CONV_PAD = 16
FRONT_ROWS = 256
CONV_ROWS = 128
SUBLANES = 8


def _front_kernel(x_ref, mods_ref, g1_ref, w_ref, dww_ref, dwb_ref, lng_ref, lnb_ref,
                  q_ref, k_ref, v_ref, c_ref, gpad_ref, *, seq):
    m = mods_ref[0]
    shift, scale = m[0:1], m[1:2]
    zeros = jnp.zeros((CONV_PAD, D_CONV), F32)
    gpad_ref[0:CONV_PAD, :] = zeros
    gpad_ref[CONV_PAD + seq:CONV_PAD + seq + CONV_PAD, :] = zeros

    def proj(i, carry):
        r0 = pl.multiple_of(i * FRONT_ROWS, FRONT_ROWS)
        x = x_ref[0, pl.ds(r0, FRONT_ROWS), :]
        h = _rms_mod(x, g1_ref[...], shift, scale).astype(BF16)
        u = _dot(h, w_ref[...])
        glu = u[:, :D_CONV] * _sigmoid(u[:, D_CONV:2 * D_CONV])
        gpad_ref[pl.ds(CONV_PAD + r0, FRONT_ROWS), :] = glu
        o = 2 * D_CONV
        q_ref[0, pl.ds(r0, FRONT_ROWS), :] = (u[:, o:o + D_ATTN] * (HEAD_DIM ** -0.5)).astype(q_ref.dtype)
        k_ref[0, pl.ds(r0, FRONT_ROWS), :] = u[:, o + D_ATTN:o + 2 * D_ATTN].astype(k_ref.dtype)
        v_ref[0, pl.ds(r0, FRONT_ROWS), :] = u[:, o + 2 * D_ATTN:o + 3 * D_ATTN].astype(v_ref.dtype)
        return carry

    lax.fori_loop(0, seq // FRONT_ROWS, proj, 0)

    def conv(i, carry):
        r0 = pl.multiple_of(i * CONV_ROWS, CONV_ROWS)
        cols = []
        for cb in range(D_CONV // LANES):
            cs = slice(cb * LANES, (cb + 1) * LANES)
            win = gpad_ref[pl.ds(r0, CONV_ROWS + 2 * CONV_PAD), cs]
            acc = None
            for s in range(SUBLANES):
                nrow = CONV_ROWS + (SUBLANES if s else 0)
                z = None
                for a in range(2 * CONV_PAD // SUBLANES):
                    t = SUBLANES * a + s - (CONV_PAD - CONV_WIDTH // 2)
                    if 0 <= t < CONV_WIDTH:
                        term = win[SUBLANES * a:SUBLANES * a + nrow] * dww_ref[t:t + 1, cs]
                        z = term if z is None else z + term
                z = z[s:s + CONV_ROWS]
                acc = z if acc is None else acc + z
            cols.append(acc)
        y = jnp.concatenate(cols, axis=1) + dwb_ref[...]
        mu = jnp.mean(y, axis=-1, keepdims=True)
        yc = y - mu
        var = jnp.mean(yc * yc, axis=-1, keepdims=True)
        z = yc * lax.rsqrt(var + EPS) * lng_ref[...] + lnb_ref[...]
        c_ref[0, pl.ds(r0, CONV_ROWS), :] = (z * _sigmoid(z)).astype(c_ref.dtype)
        return carry

    lax.fori_loop(0, seq // CONV_ROWS, conv, 0, unroll=2)


def _front(x, mods, mod_row, g1, w_a, dww, dwb, lng, lnb, kv_dtype):
    nreq, seq, _ = x.shape
    const2 = lambda b: (0, 0)
    tok = lambda b: (b, 0, 0)
    out_tok = pl.BlockSpec((1, seq, D_ATTN), tok)
    return pl.pallas_call(
        functools.partial(_front_kernel, seq=seq),
        grid=(nreq,),
        in_specs=[
            pl.BlockSpec((1, seq, D_MODEL), tok),
            pl.BlockSpec((1, N_MOD, D_MODEL), lambda b: (mod_row(b), 0, 0)),
            pl.BlockSpec((1, D_MODEL), const2),
            pl.BlockSpec((D_MODEL, D_PROJ_A), const2),
            pl.BlockSpec((CONV_WIDTH, D_CONV), const2),
            pl.BlockSpec((1, D_CONV), const2),
            pl.BlockSpec((1, D_CONV), const2),
            pl.BlockSpec((1, D_CONV), const2),
        ],
        out_specs=[out_tok, out_tok, out_tok, out_tok],
        out_shape=[
            jax.ShapeDtypeStruct((nreq, seq, D_ATTN), BF16),
            jax.ShapeDtypeStruct((nreq, seq, D_ATTN), kv_dtype),
            jax.ShapeDtypeStruct((nreq, seq, D_ATTN), kv_dtype),
            jax.ShapeDtypeStruct((nreq, seq, D_CONV), BF16),
        ],
        scratch_shapes=[pltpu.VMEM((seq + 2 * CONV_PAD, D_CONV), F32)],
        compiler_params=_params("arbitrary"),
        name="front",
    )(x, mods, g1, w_a, dww, dwb, lng, lnb)


def _pair_scores_q(q_pair):
    lane = lax.broadcasted_iota(jnp.int32, q_pair.shape, 1)
    zero = jnp.zeros_like(q_pair)
    return jnp.concatenate([jnp.where(lane < HEAD_DIM, q_pair, zero),
                            jnp.where(lane >= HEAD_DIM, q_pair, zero)], axis=0)


def _pair_merge(o2, t):
    lane = lax.broadcasted_iota(jnp.int32, (t, LANES), 1)
    return jnp.where(lane < HEAD_DIM, o2[:t], o2[t:])


def _ctx_attn_kernel(q_ref, k_ref, v_ref, o_ref):
    seq = q_ref.shape[1]
    pairs = range(N_HEADS // 2)
    sls = [slice(p * LANES, (p + 1) * LANES) for p in pairs]
    s = [lax.dot_general(_pair_scores_q(q_ref[0, :, sls[p]]), k_ref[0, :, sls[p]].astype(BF16), _NT,
                         preferred_element_type=F32) for p in pairs]
    e = [jnp.exp(s[p] - jnp.max(s[p], axis=-1, keepdims=True)) for p in pairs]
    den = [jnp.sum(e[p], axis=-1, keepdims=True) for p in pairs]
    for p in pairs:
        o2 = _dot(e[p].astype(BF16), v_ref[0, :, sls[p]].astype(BF16)) / den[p]
        o_ref[0, :, sls[p]] = _pair_merge(o2, seq).astype(o_ref.dtype)


def _ctx_attn(q, k, v):
    nreq, seq, _ = q.shape
    spec = pl.BlockSpec((1, seq, D_ATTN), lambda b: (b, 0, 0))
    return pl.pallas_call(
        _ctx_attn_kernel,
        grid=(nreq,),
        in_specs=[spec, spec, spec],
        out_specs=spec,
        out_shape=jax.ShapeDtypeStruct((nreq, seq, D_ATTN), BF16),
        compiler_params=_params("arbitrary"),
        name="ctx_attn",
    )(q, k, v)


N_DR = 2 * NA_ROWS - 1
N_DC = 2 * NA_COLS - 1


def _bias_kernel(rpb_ref, o_ref, t2_ref):
    h = pl.program_id(0)
    qi = lax.broadcasted_iota(jnp.int32, (GRID_W, LANES), 0)
    lane = lax.broadcasted_iota(jnp.int32, (GRID_W, LANES), 1)
    kc = lane & (GRID_W - 1)
    d = kc - qi
    cs = jnp.clip(qi - NA_COLS // 2, 0, GRID_W - NA_COLS)
    inside = (kc >= cs) & (kc < cs + NA_COLS)
    for dr in range(N_DR):
        t = jnp.full((GRID_W, LANES), MASK_NEG, F32)
        for j in range(N_DC):
            val = rpb_ref[h * (N_DR * N_DC) + dr * N_DC + j]
            t = jnp.where(d == j - (NA_COLS - 1), val, t)
        t2_ref[dr] = jnp.where(inside, t, MASK_NEG)
    for o in range(NA_ROWS):
        for jj in range(NA_ROWS // 2):
            o_ref[0, o, :, jj * LANES:(jj + 1) * LANES] = jnp.where(
                lane < GRID_W, t2_ref[o + 2 * jj], t2_ref[o + 2 * jj + 1])


def _bias_table(rpb):
    return pl.pallas_call(
        _bias_kernel,
        grid=(N_HEADS,),
        in_specs=[pl.BlockSpec(memory_space=pltpu.SMEM)],
        out_specs=pl.BlockSpec((1, NA_ROWS, GRID_W, NA_ROWS * GRID_W), lambda h: (h, 0, 0, 0)),
        out_shape=jax.ShapeDtypeStruct((N_HEADS, NA_ROWS, GRID_W, NA_ROWS * GRID_W), F32),
        scratch_shapes=[pltpu.VMEM((N_DR, GRID_W, LANES), F32)],
        compiler_params=_params("arbitrary"),
        name="na_bias",
    )(rpb.reshape(-1))


def _na_attn_kernel(q_ref, k_ref, v_ref, kct_ref, vct_ref, tab_ref, o_ref, kcb_ref, vcb_ref, *, rows):
    r = pl.program_id(1)

    @pl.when(r == 0)
    def _():
        kcb_ref[...] = kct_ref[0].astype(BF16)
        vcb_ref[...] = vct_ref[0].astype(BF16)

    rs = jnp.clip(r - NA_ROWS // 2, 0, rows - NA_ROWS)
    k0 = pl.multiple_of(rs * GRID_W, GRID_W)
    off = rs - r + (NA_ROWS - 1)
    nloc = NA_ROWS * GRID_W
    pairs = range(N_HEADS // 2)
    sls = [slice(p * LANES, (p + 1) * LANES) for p in pairs]
    s_loc, s_ctx = [], []
    for p in pairs:
        q2 = _pair_scores_q(q_ref[0, :, sls[p]])
        bias = jnp.concatenate([tab_ref[2 * p, off], tab_ref[2 * p + 1, off]], axis=0)
        kl = k_ref[0, pl.ds(k0, nloc), sls[p]]
        s_loc.append(lax.dot_general(q2, kl, _NT, preferred_element_type=F32) + bias)
        s_ctx.append(_dot(q2, kcb_ref[sls[p], :]))
    mx = [jnp.maximum(jnp.max(s_loc[p], axis=-1, keepdims=True),
                      jnp.max(s_ctx[p], axis=-1, keepdims=True)) for p in pairs]
    e_loc = [jnp.exp(s_loc[p] - mx[p]) for p in pairs]
    e_ctx = [jnp.exp(s_ctx[p] - mx[p]) for p in pairs]
    den = [jnp.sum(e_loc[p], axis=-1, keepdims=True) + jnp.sum(e_ctx[p], axis=-1, keepdims=True)
           for p in pairs]
    for p in pairs:
        vl = v_ref[0, pl.ds(k0, nloc), sls[p]]
        o_ctx = lax.dot_general(e_ctx[p].astype(BF16), vcb_ref[sls[p], :], _NT, preferred_element_type=F32)
        o2 = (_dot(e_loc[p].astype(BF16), vl) + o_ctx) / den[p]
        o_ref[0, :, sls[p]] = _pair_merge(o2, GRID_W).astype(o_ref.dtype)


def _na_attn(q, k, v, kc, vc, tab):
    nreq, seq, _ = q.shape
    rows = seq // GRID_W
    past = kc.shape[2]
    qspec = pl.BlockSpec((1, GRID_W, D_ATTN), lambda b, r: (b, r, 0))
    kvspec = pl.BlockSpec((1, seq, D_ATTN), lambda b, r: (b, 0, 0))
    cspec = pl.BlockSpec((1, D_ATTN, past), lambda b, r: (b, 0, 0))
    return pl.pallas_call(
        functools.partial(_na_attn_kernel, rows=rows),
        grid=(nreq, rows),
        in_specs=[qspec, kvspec, kvspec, cspec, cspec,
                  pl.BlockSpec(tab.shape, lambda b, r: (0, 0, 0, 0))],
        out_specs=qspec,
        out_shape=jax.ShapeDtypeStruct((nreq, seq, D_ATTN), BF16),
        scratch_shapes=[pltpu.VMEM((D_ATTN, past), BF16), pltpu.VMEM((D_ATTN, past), BF16)],
        compiler_params=_params("arbitrary", "arbitrary"),
        name="na_attn",
    )(q, k, v, kc, vc, tab)


MERGE_ROWS = 512
MERGE_SUB = 256


def _merge_kernel(x_ref, mods_ref, g1_ref, g2_ref, c_ref, a_ref, wg_ref, wpw_ref, wao_ref,
                  wout_ref, wr_ref, x1_ref, h2_ref, aff_ref):
    m = mods_ref[0]
    subs = range(MERGE_ROWS // MERGE_SUB)
    rows = [slice(i * MERGE_SUB, (i + 1) * MERGE_SUB) for i in subs]
    x = [x_ref[0, rows[i], :] for i in subs]
    h = [_rms_mod(x[i], g1_ref[...], m[0:1], m[1:2]).astype(BF16) for i in subs]
    gates = [_sigmoid(_dot(h[i], wg_ref[:, D_PROJ_A - D_IN // 2:])) for i in subs]
    conv_out = [_dot(c_ref[0, rows[i], :], wpw_ref[...]) for i in subs]
    attn_out = [_dot(a_ref[0, rows[i], :], wao_ref[...]) for i in subs]
    merged = [(gates[i][:, :D_MODEL] * conv_out[i] + gates[i][:, D_MODEL:] * attn_out[i]).astype(BF16)
              for i in subs]
    x1 = [x[i] + m[2:3] * _dot(merged[i], wout_ref[...]) for i in subs]
    h2 = [_rms_mod(x1[i], g2_ref[...], m[3:4], m[4:5]).astype(BF16) for i in subs]
    logits = [_dot(h2[i], wr_ref[...]) for i in subs]
    e = [jnp.exp(logits[i] - jnp.max(logits[i], axis=-1, keepdims=True)) for i in subs]
    for i in subs:
        x1_ref[0, rows[i], :] = x1[i]
        h2_ref[0, rows[i], :] = h2[i]
        aff_ref[0, rows[i], :] = e[i] / jnp.sum(e[i], axis=-1, keepdims=True)


def _merge(x, mods, mod_row, g1, g2, cact, heads, w_g, w_pw, w_ao, w_out, w_r):
    nreq, seq, _ = x.shape
    tm = MERGE_ROWS
    tok = lambda b, j: (b, j, 0)
    const2 = lambda b, j: (0, 0)
    return pl.pallas_call(
        _merge_kernel,
        grid=(nreq, seq // tm),
        in_specs=[
            pl.BlockSpec((1, tm, D_MODEL), tok),
            pl.BlockSpec((1, N_MOD, D_MODEL), lambda b, j: (mod_row(b), 0, 0)),
            pl.BlockSpec((1, D_MODEL), const2),
            pl.BlockSpec((1, D_MODEL), const2),
            pl.BlockSpec((1, tm, D_CONV), tok),
            pl.BlockSpec((1, tm, D_ATTN), tok),
            pl.BlockSpec((D_MODEL, D_IN // 2), lambda b, j: (0, 1)),
            pl.BlockSpec(w_pw.shape, const2),
            pl.BlockSpec(w_ao.shape, const2),
            pl.BlockSpec(w_out.shape, const2),
            pl.BlockSpec(w_r.shape, const2),
        ],
        out_specs=[
            pl.BlockSpec((1, tm, D_MODEL), tok),
            pl.BlockSpec((1, tm, D_MODEL), tok),
            pl.BlockSpec((1, tm, N_EXPERTS), tok),
        ],
        out_shape=[
            jax.ShapeDtypeStruct((nreq, seq, D_MODEL), F32),
            jax.ShapeDtypeStruct((nreq, seq, D_MODEL), BF16),
            jax.ShapeDtypeStruct((nreq, seq, N_EXPERTS), F32),
        ],
        compiler_params=_params("arbitrary", "arbitrary"),
        name="merge",
    )(x, mods, g1, g2, cact, heads, w_g, w_pw, w_ao, w_out, w_r)


def _route_kernel(aff_ref, slot_t_ref, slot_r_ref, aff_r_ref, *, cap):
    a = aff_ref[0]
    seq = a.shape[0]
    one, zero = jnp.ones_like(a), jnp.zeros_like(a)

    def search(i, thr):
        cand = thr | jnp.left_shift(jnp.int32(1), 30 - i)
        cnt = jnp.sum(jnp.where(a >= pltpu.bitcast(cand, F32), one, zero), axis=0, keepdims=True)
        return jnp.where(cnt >= cap, cand, thr)

    thr = lax.fori_loop(0, 31, search, jnp.zeros((1, LANES), jnp.int32))
    gt = a >= pltpu.bitcast(thr + 1, F32)
    eq = (a >= pltpu.bitcast(thr, F32)) & jnp.logical_not(gt)
    need = cap - jnp.sum(jnp.where(gt, one, zero), axis=0, keepdims=True)
    ti = lax.broadcasted_iota(jnp.int32, (seq, seq), 0)
    tj = lax.broadcasted_iota(jnp.int32, (seq, seq), 1)
    before = jnp.where(tj < ti, 1.0, 0.0).astype(BF16)
    eq_rank = _dot(before, jnp.where(eq, one, zero).astype(BF16))
    sel = gt | (eq & (eq_rank < need))
    pos = _dot(before, jnp.where(sel, one, zero).astype(BF16))
    slot = jnp.where(sel, pos, -1.0)
    slot_t_ref[0] = slot
    slot_r_ref[0] = slot.T
    aff_r_ref[0] = a.T


def _route(aff_t, cap):
    ngroup, seq, _ = aff_t.shape
    tspec = pl.BlockSpec((1, seq, LANES), lambda g: (g, 0, 0))
    rspec = pl.BlockSpec((1, LANES, seq), lambda g: (g, 0, 0))
    return pl.pallas_call(
        functools.partial(_route_kernel, cap=cap),
        grid=(ngroup,),
        in_specs=[tspec],
        out_specs=[tspec, rspec, rspec],
        out_shape=[
            jax.ShapeDtypeStruct((ngroup, seq, LANES), F32),
            jax.ShapeDtypeStruct((ngroup, LANES, seq), F32),
            jax.ShapeDtypeStruct((ngroup, LANES, seq), F32),
        ],
        compiler_params=_params("arbitrary"),
        name="route",
    )(aff_t)


GATHER_ROWS = 512


def _gather_kernel(slot_ref, aff_ref, h_ref, x_ref, g_ref, p_ref, *, cap):
    seq = h_ref.shape[1]
    nslot = N_EXPERTS * cap
    sub = lax.broadcasted_iota(jnp.int32, (cap, seq), 0).astype(F32)
    one, zero = jnp.ones((cap, seq), F32), jnp.zeros((cap, seq), F32)
    for e in range(N_EXPERTS):
        row = jnp.broadcast_to(slot_ref[0, e:e + 1, :], (cap, seq))
        p_ref[e * cap:(e + 1) * cap, :] = jnp.where(row == sub, one, zero).astype(BF16)
    h = h_ref[0]
    per_chunk = GATHER_ROWS // cap
    for c in range(nslot // GATHER_ROWS):
        rows = slice(c * GATHER_ROWS, (c + 1) * GATHER_ROWS)
        x = _dot(p_ref[rows, :], h).astype(x_ref.dtype)
        for i in range(per_chunk):
            x_ref[c * per_chunk + i] = x[i * cap:(i + 1) * cap]
    a = aff_ref[0]
    a_hi = a.astype(BF16)
    r1 = a - a_hi.astype(F32)
    a_mid = r1.astype(BF16)
    a_lo = (r1 - a_mid.astype(F32)).astype(BF16)
    a3 = jnp.concatenate([a_hi, a_mid, a_lo], axis=0)
    g3 = lax.dot_general(a3, p_ref[...], _NT, preferred_element_type=F32)
    gfull = g3[0:N_EXPERTS] + g3[N_EXPERTS:2 * N_EXPERTS] + g3[2 * N_EXPERTS:3 * N_EXPERTS]
    ei = lax.broadcasted_iota(jnp.int32, (N_EXPERTS, nslot), 0)
    si = lax.broadcasted_iota(jnp.int32, (N_EXPERTS, nslot), 1)
    own = (si >= ei * cap) & (si < (ei + 1) * cap)
    g_ref[0] = jnp.sum(jnp.where(own, gfull, 0.0), axis=0, keepdims=True)


def _gather(slot_r, aff_r, h2, cap):
    nreq, seq, _ = h2.shape
    nslot = N_EXPERTS * cap
    grp = lambda b: (b // REQ_PER_GROUP, b % REQ_PER_GROUP, 0)
    x, g = pl.pallas_call(
        functools.partial(_gather_kernel, cap=cap),
        grid=(nreq,),
        in_specs=[
            pl.BlockSpec((1, N_EXPERTS, seq), grp),
            pl.BlockSpec((1, N_EXPERTS, seq), grp),
            pl.BlockSpec((1, seq, D_MODEL), lambda b: (b, 0, 0)),
        ],
        out_specs=[
            pl.BlockSpec((N_EXPERTS, cap, D_MODEL), lambda b: (0, b, 0)),
            pl.BlockSpec((1, 1, nslot), lambda b: (b, 0, 0)),
        ],
        out_shape=[
            jax.ShapeDtypeStruct((N_EXPERTS, nreq * cap, D_MODEL), BF16),
            jax.ShapeDtypeStruct((nreq, 1, nslot), F32),
        ],
        scratch_shapes=[pltpu.VMEM((nslot, seq), BF16)],
        compiler_params=_params("arbitrary"),
        name="gather",
    )(slot_r, aff_r, h2)
    return x, g


EXPERT_FCHUNK = 512
EXPERT_ROWS = 512
GATE_ROWS = 256


def _expert_kernel(xc_ref, xl_ref, g_ref, wg_ref, wu_ref, wd_ref, yc_ref, yl_ref,
                   acc_ref, wgb_ref, wub_ref, wdb_ref):
    f = pl.program_id(1)
    nf = pl.num_programs(1)
    half = xc_ref.shape[1]
    wgb_ref[...] = wg_ref[0].astype(BF16)
    wub_ref[...] = wu_ref[0].astype(BF16)
    wdb_ref[...] = wd_ref[0].astype(BF16)

    @pl.when(f == 0)
    def _():
        acc_ref[...] = jnp.zeros_like(acc_ref)

    for part, x_ref in enumerate((xc_ref, xl_ref)):
        def tile(i, carry, x_ref=x_ref, part=part):
            r0 = pl.multiple_of(i * EXPERT_ROWS, EXPERT_ROWS)
            x = x_ref[0, pl.ds(r0, EXPERT_ROWS), :]
            gq = _dot(x, wgb_ref[...])
            uq = _dot(x, wub_ref[...])
            hid = (gq * _sigmoid(gq) * uq).astype(BF16)
            acc_ref[pl.ds(part * half + r0, EXPERT_ROWS), :] += _dot(hid, wdb_ref[...])
            return carry
        lax.fori_loop(0, half // EXPERT_ROWS, tile, 0, unroll=True)

    @pl.when(f == nf - 1)
    def _():
        ri = lax.broadcasted_iota(jnp.int32, (GATE_ROWS, GATE_ROWS), 0)
        ci = lax.broadcasted_iota(jnp.int32, (GATE_ROWS, GATE_ROWS), 1)
        for part, y_ref in enumerate((yc_ref, yl_ref)):
            for i in range(half // GATE_ROWS):
                t = part * (half // GATE_ROWS) + i
                grow = jnp.broadcast_to(g_ref[0, t:t + 1, :], (GATE_ROWS, GATE_ROWS))
                gcol = jnp.sum(jnp.where(ri == ci, grow, 0.0), axis=-1, keepdims=True)
                rows = slice(i * GATE_ROWS, (i + 1) * GATE_ROWS)
                y = acc_ref[t * GATE_ROWS:(t + 1) * GATE_ROWS, :] * gcol
                y_ref[0, rows, :] = y.astype(y_ref.dtype)


def _experts(xc, xl, g, w_gate, w_up, w_down):
    ne, half, _ = xc.shape
    fc = EXPERT_FCHUNK
    xspec = pl.BlockSpec((1, half, D_MODEL), lambda e, f: (e, 0, 0))
    return pl.pallas_call(
        _expert_kernel,
        grid=(ne, D_EXPERT // fc),
        in_specs=[
            xspec, xspec,
            pl.BlockSpec((1,) + g.shape[1:], lambda e, f: (e, 0, 0)),
            pl.BlockSpec((1, D_MODEL, fc), lambda e, f: (e, 0, f)),
            pl.BlockSpec((1, D_MODEL, fc), lambda e, f: (e, 0, f)),
            pl.BlockSpec((1, fc, D_MODEL), lambda e, f: (e, f, 0)),
        ],
        out_specs=[xspec, xspec],
        out_shape=[jax.ShapeDtypeStruct(xc.shape, BF16), jax.ShapeDtypeStruct(xl.shape, BF16)],
        scratch_shapes=[
            pltpu.VMEM((2 * half, D_MODEL), F32),
            pltpu.VMEM((D_MODEL, fc), BF16),
            pltpu.VMEM((D_MODEL, fc), BF16),
            pltpu.VMEM((fc, D_MODEL), BF16),
        ],
        compiler_params=_params("arbitrary", "arbitrary"),
        name="experts",
    )(xc, xl, g, w_gate, w_up, w_down)


COMBINE_ROWS = 256


def _combine_kernel(x1_ref, mods_ref, slot_ref, y_ref, gf_ref, o_ref, *, cap):
    b = pl.program_id(0)
    nslot = N_EXPERTS * cap
    li = lax.broadcasted_iota(jnp.int32, (LANES, nslot), 0)
    si = lax.broadcasted_iota(jnp.int32, (LANES, nslot), 1)
    lane0 = (b % REQ_PER_GROUP) * N_EXPERTS
    expand = jnp.where(li == lane0 + si // cap, 1.0, 0.0).astype(BF16)
    slot_exp = _dot(slot_ref[0].astype(BF16), expand)
    want = (lax.broadcasted_iota(jnp.int32, slot_exp.shape, 1) & (cap - 1)).astype(F32)
    scat = jnp.where(slot_exp == want, 1.0, 0.0).astype(BF16)
    moe = _dot(scat, y_ref[...].reshape(nslot, D_MODEL))
    x2 = x1_ref[0] + mods_ref[0][5:6] * moe
    o_ref[0] = x2 * lax.rsqrt(jnp.mean(x2 * x2, axis=-1, keepdims=True) + EPS) * gf_ref[...]


def _combine(x1, mods, mod_row, slot_t, y, gf, cap):
    nreq, seq, _ = x1.shape
    nslot = N_EXPERTS * cap
    tm = COMBINE_ROWS
    tok = lambda b, j: (b, j, 0)
    return pl.pallas_call(
        functools.partial(_combine_kernel, cap=cap),
        grid=(nreq, seq // tm),
        in_specs=[
            pl.BlockSpec((1, tm, D_MODEL), tok),
            pl.BlockSpec((1, N_MOD, D_MODEL), lambda b, j: (mod_row(b), 0, 0)),
            pl.BlockSpec((1, tm, LANES), lambda b, j: (b // REQ_PER_GROUP, j, 0)),
            pl.BlockSpec((N_EXPERTS, cap, D_MODEL), lambda b, j: (0, b, 0)),
            pl.BlockSpec((1, D_MODEL), lambda b, j: (0, 0)),
        ],
        out_specs=pl.BlockSpec((1, tm, D_MODEL), tok),
        out_shape=jax.ShapeDtypeStruct(x1.shape, F32),
        compiler_params=_params("arbitrary", "arbitrary"),
        name="combine",
    )(x1, mods, slot_t, y, gf)


def _group_lanes(aff):
    nreq, seq, ne = aff.shape
    a = aff.reshape(nreq // REQ_PER_GROUP, REQ_PER_GROUP, seq, ne)
    return jnp.transpose(a, (0, 2, 1, 3)).reshape(nreq // REQ_PER_GROUP, seq, REQ_PER_GROUP * ne)


def kernel(x_prompt, x_sample, cache_ctx_k, cache_ctx_v, c, c_ctx, w_ada, b_ada, norm1_g, w_in,
           conv_dw_w, conv_dw_b, conv_ln_g, conv_ln_b, w_conv_pw, na_rpb, w_attn_o, w_out,
           norm2_g, w_router, w_gate_e, w_up_e, w_down_e, final_g):
    assert w_ada.shape[0] == 1, "single trunk layer"
    nctx, ctx_seq, _ = x_prompt.shape
    nlat, lat_seq, _ = x_sample.shape
    ctx_cap = EC_CAPACITY * ctx_seq // N_EXPERTS
    lat_cap = EC_CAPACITY * lat_seq // N_EXPERTS
    assert nctx * ctx_cap == nlat * lat_cap

    cond = jnp.zeros((2 * nlat, D_MODEL), F32).at[:nlat].set(c).at[nlat].set(c_ctx)
    mods = _ada(cond, w_ada[0], b_ada).reshape(2 * nlat, N_MOD, D_MODEL)
    ctx_row = lambda b: nlat
    lat_row = lambda b: b

    w_a = w_g = w_in[0].astype(BF16)
    w_pw = w_conv_pw[0].astype(BF16)
    w_ao = w_attn_o[0].astype(BF16)
    w_o = w_out[0].astype(BF16)
    w_r = w_router[0].astype(BF16)
    conv_args = (conv_dw_w[0], conv_dw_b, conv_ln_g, conv_ln_b)

    q_c, k_c, v_c, cact_c = _front(x_prompt, mods, ctx_row, norm1_g, w_a, *conv_args, F32)
    q_l, k_l, v_l, cact_l = _front(x_sample, mods, lat_row, norm1_g, w_a, *conv_args, BF16)

    heads_c = _ctx_attn(q_c, k_c, v_c)
    tab = _bias_table(na_rpb[0])
    past = cache_ctx_k.shape[2]
    feat_major = lambda t: jnp.transpose(t[:, 0], (0, 2, 3, 1)).reshape(nlat, D_ATTN, past)
    heads_l = _na_attn(q_l, k_l, v_l, feat_major(cache_ctx_k), feat_major(cache_ctx_v), tab)

    merge_w = (w_g, w_pw, w_ao, w_o, w_r)
    flat = lambda t: t.reshape(1, nctx * ctx_seq, t.shape[-1])
    x1_c, h2_c, aff_c = _merge(flat(x_prompt), mods, ctx_row, norm1_g, norm2_g, flat(cact_c),
                               flat(heads_c), *merge_w)
    x1_c, h2_c, aff_c = (t.reshape(nctx, ctx_seq, t.shape[-1]) for t in (x1_c, h2_c, aff_c))
    x1_l, h2_l, aff_l = _merge(x_sample, mods, lat_row, norm1_g, norm2_g, cact_l, heads_l, *merge_w)

    slot_t_c, slot_r_c, aff_r_c = _route(_group_lanes(aff_c), ctx_cap)
    slot_t_l, slot_r_l, aff_r_l = _route(_group_lanes(aff_l), lat_cap)

    xg_c, g_c = _gather(slot_r_c, aff_r_c, h2_c, ctx_cap)
    xg_l, g_l = _gather(slot_r_l, aff_r_l, h2_l, lat_cap)

    def gate_rows(g, nreq, cap):
        g = jnp.transpose(g.reshape(nreq, N_EXPERTS, cap), (1, 0, 2))
        return g.reshape(N_EXPERTS, nreq * cap // GATE_ROWS, GATE_ROWS)

    gates = jnp.concatenate([gate_rows(g_c, nctx, ctx_cap), gate_rows(g_l, nlat, lat_cap)], axis=1)
    y_c, y_l = _experts(xg_c, xg_l, gates, w_gate_e[0], w_up_e[0], w_down_e[0])

    gf = final_g.reshape(1, D_MODEL)
    y_prompt = _combine(x1_c, mods, ctx_row, slot_t_c, y_c, gf, ctx_cap)
    y_sample = _combine(x1_l, mods, lat_row, slot_t_l, y_l, gf, lat_cap)

    state_k = k_c.reshape(nctx, 1, ctx_seq, N_HEADS, HEAD_DIM)
    state_v = v_c.reshape(nctx, 1, ctx_seq, N_HEADS, HEAD_DIM)
    return (y_prompt, y_sample, state_k, state_v)
```

```python
import functools

import jax
import jax.numpy as jnp
from jax import lax
from jax.experimental import pallas as pl
from jax.experimental.pallas import tpu as pltpu

F32 = jnp.float32
BF16 = jnp.bfloat16

D_MODEL = 1024
D_CONV = 512
CONV_WIDTH = 31
N_HEADS = 8
HEAD_DIM = 64
D_ATTN = N_HEADS * HEAD_DIM
GRID_W = 64
NA_ROWS = 8
NA_COLS = 16
N_EXPERTS = 16
D_EXPERT = 2048
EC_CAPACITY = 2
N_MOD = 6
EPS = 1e-6
D_PROJ_A = 2 * D_CONV + 3 * D_ATTN
D_IN = D_PROJ_A + 2 * D_MODEL
LANES = 128
REQ_PER_GROUP = LANES // N_EXPERTS
MASK_NEG = -1e30
VMEM_LIMIT = 56 * 1024 * 1024

_NT = (((1,), (1,)), ((), ()))


def _params(*sem):
    return pltpu.CompilerParams(dimension_semantics=sem, vmem_limit_bytes=VMEM_LIMIT)


def _dot(a, b):
    return jnp.dot(a, b, preferred_element_type=F32)


def _sigmoid(x):
    return 1.0 / (1.0 + jnp.exp(-x))


def _split_bf16(x):
    hi = x.astype(BF16)
    lo = (x - hi.astype(F32)).astype(BF16)
    return hi, lo


def _rms_mod(x, g, shift, scale):
    xn = x * lax.rsqrt(jnp.mean(x * x, axis=-1, keepdims=True) + EPS) * g
    return xn * (1.0 + scale) + shift


def _ada_kernel(cond_ref, w_ref, b_ref, o_ref):
    c = cond_ref[...]
    a = c * _sigmoid(c)
    ah, al = _split_bf16(a)
    wh, wl = _split_bf16(w_ref[...])
    o_ref[...] = _dot(ah, wh) + _dot(al, wh) + _dot(ah, wl) + b_ref[...]


def _ada(cond, w_ada, b_ada):
    n = w_ada.shape[1]
    bn = 1024
    return pl.pallas_call(
        _ada_kernel,
        grid=(n // bn,),
        in_specs=[
            pl.BlockSpec(cond.shape, lambda j: (0, 0)),
            pl.BlockSpec((D_MODEL, bn), lambda j: (0, j)),
            pl.BlockSpec((1, bn), lambda j: (0, j)),
        ],
        out_specs=pl.BlockSpec((cond.shape[0], bn), lambda j: (0, j)),
        out_shape=jax.ShapeDtypeStruct((cond.shape[0], n), F32),
        compiler_params=_params("arbitrary"),
        name="ada",
    )(cond, w_ada, b_ada)


CONV_PAD = 16
FRONT_ROWS = 256
CONV_ROWS = 128
SUBLANES = 8
FRONT_CTX_PER_STEP = 4


def _front_kernel(x_ref, mods_ref, g1_ref, w_ref, dww_ref, dwb_ref, lng_ref, lnb_ref,
                  q_ref, k_ref, v_ref, c_ref, *gpad_refs, seq):
    nreq = x_ref.shape[0]
    m = mods_ref[0]
    shift, scale = m[0:1], m[1:2]
    zeros = jnp.zeros((CONV_PAD, D_CONV), F32)
    for gpad_ref in gpad_refs:
        gpad_ref[0:CONV_PAD, :] = zeros
        gpad_ref[CONV_PAD + seq:CONV_PAD + seq + CONV_PAD, :] = zeros

    def proj(r, i):
        rows = slice(i * FRONT_ROWS, (i + 1) * FRONT_ROWS)
        x = x_ref[r, rows, :]
        h = _rms_mod(x, g1_ref[...], shift, scale).astype(BF16)
        u = _dot(h, w_ref[...])
        glu = u[:, :D_CONV] * _sigmoid(u[:, D_CONV:2 * D_CONV])
        g0 = CONV_PAD + i * FRONT_ROWS
        gpad_refs[r][g0:g0 + FRONT_ROWS, :] = glu
        o = 2 * D_CONV
        q_ref[r, rows, :] = (u[:, o:o + D_ATTN] * (HEAD_DIM ** -0.5)).astype(q_ref.dtype)
        k_ref[r, rows, :] = u[:, o + D_ATTN:o + 2 * D_ATTN].astype(k_ref.dtype)
        v_ref[r, rows, :] = u[:, o + 2 * D_ATTN:o + 3 * D_ATTN].astype(v_ref.dtype)

    def conv(r, j):
        g0 = j * CONV_ROWS
        cols = []
        for cb in range(D_CONV // LANES):
            cs = slice(cb * LANES, (cb + 1) * LANES)
            win = gpad_refs[r][g0:g0 + CONV_ROWS + 2 * CONV_PAD, cs]
            acc = None
            for s in range(SUBLANES):
                nrow = CONV_ROWS + (SUBLANES if s else 0)
                z = None
                for a in range(2 * CONV_PAD // SUBLANES):
                    t = SUBLANES * a + s - (CONV_PAD - CONV_WIDTH // 2)
                    if 0 <= t < CONV_WIDTH:
                        term = win[SUBLANES * a:SUBLANES * a + nrow] * dww_ref[t:t + 1, cs]
                        z = term if z is None else z + term
                z = z[s:s + CONV_ROWS]
                acc = z if acc is None else acc + z
            cols.append(acc)
        y = jnp.concatenate(cols, axis=1) + dwb_ref[...]
        mu = jnp.mean(y, axis=-1, keepdims=True)
        yc = y - mu
        var = jnp.mean(yc * yc, axis=-1, keepdims=True)
        z = yc * lax.rsqrt(var + EPS) * lng_ref[...] + lnb_ref[...]
        c_ref[r, j * CONV_ROWS:(j + 1) * CONV_ROWS, :] = (z * _sigmoid(z)).astype(c_ref.dtype)

    chunks = [(r, i) for r in range(nreq) for i in range(seq // FRONT_ROWS)]
    blocks = [(r, j) for r in range(nreq) for j in range(seq // CONV_ROWS)]

    def ready(block, done):
        r, j = block
        last = min((j + 1) * CONV_ROWS + CONV_WIDTH // 2, seq) - 1
        return (r, last // FRONT_ROWS) in done

    done, pending = set(), list(blocks)
    proj(*chunks[0])
    for n, chunk in enumerate(chunks):
        done.add(chunk)
        if n + 1 < len(chunks):
            proj(*chunks[n + 1])
        now = [b for b in pending if ready(b, done)]
        pending = [b for b in pending if b not in now]
        for b in now:
            conv(*b)


def _front(x, mods, mod_row, g1, w_a, dww, dwb, lng, lnb, kv_dtype, per_step):
    nreq, seq, _ = x.shape
    const2 = lambda b: (0, 0)
    tok = lambda b: (b, 0, 0)
    out_tok = pl.BlockSpec((per_step, seq, D_ATTN), tok)
    return pl.pallas_call(
        functools.partial(_front_kernel, seq=seq),
        grid=(nreq // per_step,),
        in_specs=[
            pl.BlockSpec((per_step, seq, D_MODEL), tok),
            pl.BlockSpec((1, N_MOD, D_MODEL), lambda b: (mod_row(b * per_step), 0, 0)),
            pl.BlockSpec((1, D_MODEL), const2),
            pl.BlockSpec((D_MODEL, D_PROJ_A), const2),
            pl.BlockSpec((CONV_WIDTH, D_CONV), const2),
            pl.BlockSpec((1, D_CONV), const2),
            pl.BlockSpec((1, D_CONV), const2),
            pl.BlockSpec((1, D_CONV), const2),
        ],
        out_specs=[out_tok, out_tok, out_tok, out_tok],
        out_shape=[
            jax.ShapeDtypeStruct((nreq, seq, D_ATTN), BF16),
            jax.ShapeDtypeStruct((nreq, seq, D_ATTN), kv_dtype),
            jax.ShapeDtypeStruct((nreq, seq, D_ATTN), kv_dtype),
            jax.ShapeDtypeStruct((nreq, seq, D_CONV), BF16),
        ],
        scratch_shapes=[pltpu.VMEM((seq + 2 * CONV_PAD, D_CONV), F32)] * per_step,
        compiler_params=_params("arbitrary"),
        name="front",
    )(x, mods, g1, w_a, dww, dwb, lng, lnb)


def _pair_scores_q(q_pair):
    lane = lax.broadcasted_iota(jnp.int32, q_pair.shape, 1)
    zero = jnp.zeros_like(q_pair)
    return jnp.concatenate([jnp.where(lane < HEAD_DIM, q_pair, zero),
                            jnp.where(lane >= HEAD_DIM, q_pair, zero)], axis=0)


def _pair_merge(o2, t):
    lane = lax.broadcasted_iota(jnp.int32, (t, LANES), 1)
    return jnp.where(lane < HEAD_DIM, o2[:t], o2[t:])


def _ctx_attn_kernel(q_ref, k_ref, v_ref, o_ref):
    seq = q_ref.shape[1]
    pairs = range(N_HEADS // 2)
    sls = [slice(p * LANES, (p + 1) * LANES) for p in pairs]
    s = [lax.dot_general(_pair_scores_q(q_ref[0, :, sls[p]]), k_ref[0, :, sls[p]].astype(BF16), _NT,
                         preferred_element_type=F32) for p in pairs]
    e = [jnp.exp(s[p] - jnp.max(s[p], axis=-1, keepdims=True)) for p in pairs]
    den = [jnp.sum(e[p], axis=-1, keepdims=True) for p in pairs]
    for p in pairs:
        o2 = _dot(e[p].astype(BF16), v_ref[0, :, sls[p]].astype(BF16)) / den[p]
        o_ref[0, :, sls[p]] = _pair_merge(o2, seq).astype(o_ref.dtype)


def _ctx_attn(q, k, v):
    nreq, seq, _ = q.shape
    spec = pl.BlockSpec((1, seq, D_ATTN), lambda b: (b, 0, 0))
    return pl.pallas_call(
        _ctx_attn_kernel,
        grid=(nreq,),
        in_specs=[spec, spec, spec],
        out_specs=spec,
        out_shape=jax.ShapeDtypeStruct((nreq, seq, D_ATTN), BF16),
        compiler_params=_params("arbitrary"),
        name="ctx_attn",
    )(q, k, v)


N_DR = 2 * NA_ROWS - 1
N_DC = 2 * NA_COLS - 1


def _bias_kernel(rpb_ref, o_ref, t2_ref):
    h = pl.program_id(0)
    qi = lax.broadcasted_iota(jnp.int32, (GRID_W, LANES), 0)
    lane = lax.broadcasted_iota(jnp.int32, (GRID_W, LANES), 1)
    kc = lane & (GRID_W - 1)
    d = kc - qi
    cs = jnp.clip(qi - NA_COLS // 2, 0, GRID_W - NA_COLS)
    inside = (kc >= cs) & (kc < cs + NA_COLS)
    for dr in range(N_DR):
        t = jnp.full((GRID_W, LANES), MASK_NEG, F32)
        for j in range(N_DC):
            val = rpb_ref[h * (N_DR * N_DC) + dr * N_DC + j]
            t = jnp.where(d == j - (NA_COLS - 1), val, t)
        t2_ref[dr] = jnp.where(inside, t, MASK_NEG)
    for o in range(NA_ROWS):
        for jj in range(NA_ROWS // 2):
            o_ref[0, o, :, jj * LANES:(jj + 1) * LANES] = jnp.where(
                lane < GRID_W, t2_ref[o + 2 * jj], t2_ref[o + 2 * jj + 1])


def _bias_table(rpb):
    return pl.pallas_call(
        _bias_kernel,
        grid=(N_HEADS,),
        in_specs=[pl.BlockSpec(memory_space=pltpu.SMEM)],
        out_specs=pl.BlockSpec((1, NA_ROWS, GRID_W, NA_ROWS * GRID_W), lambda h: (h, 0, 0, 0)),
        out_shape=jax.ShapeDtypeStruct((N_HEADS, NA_ROWS, GRID_W, NA_ROWS * GRID_W), F32),
        scratch_shapes=[pltpu.VMEM((N_DR, GRID_W, LANES), F32)],
        compiler_params=_params("arbitrary"),
        name="na_bias",
    )(rpb.reshape(-1))


def _na_attn_kernel(q_ref, k_ref, v_ref, kct_ref, vct_ref, tab_ref, o_ref, kcb_ref, vcb_ref, *, rows):
    r = pl.program_id(1)

    @pl.when(r == 0)
    def _():
        kcb_ref[...] = kct_ref[0].astype(BF16)
        vcb_ref[...] = vct_ref[0].astype(BF16)

    rs = jnp.clip(r - NA_ROWS // 2, 0, rows - NA_ROWS)
    k0 = pl.multiple_of(rs * GRID_W, GRID_W)
    off = rs - r + (NA_ROWS - 1)
    nloc = NA_ROWS * GRID_W
    pairs = range(N_HEADS // 2)
    sls = [slice(p * LANES, (p + 1) * LANES) for p in pairs]
    s_loc, s_ctx = [], []
    for p in pairs:
        q2 = _pair_scores_q(q_ref[0, :, sls[p]])
        bias = jnp.concatenate([tab_ref[2 * p, off], tab_ref[2 * p + 1, off]], axis=0)
        kl = k_ref[0, pl.ds(k0, nloc), sls[p]]
        s_loc.append(lax.dot_general(q2, kl, _NT, preferred_element_type=F32) + bias)
        s_ctx.append(_dot(q2, kcb_ref[sls[p], :]))
    mx = [jnp.maximum(jnp.max(s_loc[p], axis=-1, keepdims=True),
                      jnp.max(s_ctx[p], axis=-1, keepdims=True)) for p in pairs]
    e_loc = [jnp.exp(s_loc[p] - mx[p]) for p in pairs]
    e_ctx = [jnp.exp(s_ctx[p] - mx[p]) for p in pairs]
    den = [jnp.sum(e_loc[p], axis=-1, keepdims=True) + jnp.sum(e_ctx[p], axis=-1, keepdims=True)
           for p in pairs]
    for p in pairs:
        vl = v_ref[0, pl.ds(k0, nloc), sls[p]]
        o_ctx = lax.dot_general(e_ctx[p].astype(BF16), vcb_ref[sls[p], :], _NT, preferred_element_type=F32)
        o2 = (_dot(e_loc[p].astype(BF16), vl) + o_ctx) / den[p]
        o_ref[0, :, sls[p]] = _pair_merge(o2, GRID_W).astype(o_ref.dtype)


def _na_attn(q, k, v, kc, vc, tab):
    nreq, seq, _ = q.shape
    rows = seq // GRID_W
    past = kc.shape[2]
    qspec = pl.BlockSpec((1, GRID_W, D_ATTN), lambda b, r: (b, r, 0))
    kvspec = pl.BlockSpec((1, seq, D_ATTN), lambda b, r: (b, 0, 0))
    cspec = pl.BlockSpec((1, D_ATTN, past), lambda b, r: (b, 0, 0))
    return pl.pallas_call(
        functools.partial(_na_attn_kernel, rows=rows),
        grid=(nreq, rows),
        in_specs=[qspec, kvspec, kvspec, cspec, cspec,
                  pl.BlockSpec(tab.shape, lambda b, r: (0, 0, 0, 0))],
        out_specs=qspec,
        out_shape=jax.ShapeDtypeStruct((nreq, seq, D_ATTN), BF16),
        scratch_shapes=[pltpu.VMEM((D_ATTN, past), BF16), pltpu.VMEM((D_ATTN, past), BF16)],
        compiler_params=_params("arbitrary", "arbitrary"),
        name="na_attn",
    )(q, k, v, kc, vc, tab)


MERGE_ROWS = 512
MERGE_SUB = 256


def _merge_kernel(x_ref, mods_ref, g1_ref, g2_ref, c_ref, a_ref, wg_ref, wpw_ref, wao_ref,
                  wout_ref, wr_ref, x1_ref, h2_ref, aff_ref):
    m = mods_ref[0]
    subs = range(MERGE_ROWS // MERGE_SUB)
    rows = [slice(i * MERGE_SUB, (i + 1) * MERGE_SUB) for i in subs]
    x = [x_ref[0, rows[i], :] for i in subs]
    h = [_rms_mod(x[i], g1_ref[...], m[0:1], m[1:2]).astype(BF16) for i in subs]
    gates = [_sigmoid(_dot(h[i], wg_ref[:, D_PROJ_A - D_IN // 2:])) for i in subs]
    conv_out = [_dot(c_ref[0, rows[i], :], wpw_ref[...]) for i in subs]
    attn_out = [_dot(a_ref[0, rows[i], :], wao_ref[...]) for i in subs]
    merged = [(gates[i][:, :D_MODEL] * conv_out[i] + gates[i][:, D_MODEL:] * attn_out[i]).astype(BF16)
              for i in subs]
    x1 = [x[i] + m[2:3] * _dot(merged[i], wout_ref[...]) for i in subs]
    h2 = [_rms_mod(x1[i], g2_ref[...], m[3:4], m[4:5]).astype(BF16) for i in subs]
    logits = [_dot(h2[i], wr_ref[...]) for i in subs]
    e = [jnp.exp(logits[i] - jnp.max(logits[i], axis=-1, keepdims=True)) for i in subs]
    for i in subs:
        x1_ref[0, rows[i], :] = x1[i]
        h2_ref[0, rows[i], :] = h2[i]
        aff_ref[0, rows[i], :] = e[i] / jnp.sum(e[i], axis=-1, keepdims=True)


def _merge(x, mods, mod_row, g1, g2, cact, heads, w_g, w_pw, w_ao, w_out, w_r):
    nreq, seq, _ = x.shape
    tm = MERGE_ROWS
    tok = lambda b, j: (b, j, 0)
    const2 = lambda b, j: (0, 0)
    return pl.pallas_call(
        _merge_kernel,
        grid=(nreq, seq // tm),
        in_specs=[
            pl.BlockSpec((1, tm, D_MODEL), tok),
            pl.BlockSpec((1, N_MOD, D_MODEL), lambda b, j: (mod_row(b), 0, 0)),
            pl.BlockSpec((1, D_MODEL), const2),
            pl.BlockSpec((1, D_MODEL), const2),
            pl.BlockSpec((1, tm, D_CONV), tok),
            pl.BlockSpec((1, tm, D_ATTN), tok),
            pl.BlockSpec((D_MODEL, D_IN // 2), lambda b, j: (0, 1)),
            pl.BlockSpec(w_pw.shape, const2),
            pl.BlockSpec(w_ao.shape, const2),
            pl.BlockSpec(w_out.shape, const2),
            pl.BlockSpec(w_r.shape, const2),
        ],
        out_specs=[
            pl.BlockSpec((1, tm, D_MODEL), tok),
            pl.BlockSpec((1, tm, D_MODEL), tok),
            pl.BlockSpec((1, tm, N_EXPERTS), tok),
        ],
        out_shape=[
            jax.ShapeDtypeStruct((nreq, seq, D_MODEL), F32),
            jax.ShapeDtypeStruct((nreq, seq, D_MODEL), BF16),
            jax.ShapeDtypeStruct((nreq, seq, N_EXPERTS), F32),
        ],
        compiler_params=_params("arbitrary", "arbitrary"),
        name="merge",
    )(x, mods, g1, g2, cact, heads, w_g, w_pw, w_ao, w_out, w_r)


def _route_kernel(aff_ref, slot_t_ref, slot_r_ref, aff_r_ref, *, cap):
    a = aff_ref[0]
    seq = a.shape[0]
    one, zero = jnp.ones_like(a), jnp.zeros_like(a)

    def search(i, thr):
        cand = thr | jnp.left_shift(jnp.int32(1), 30 - i)
        cnt = jnp.sum(jnp.where(a >= pltpu.bitcast(cand, F32), one, zero), axis=0, keepdims=True)
        return jnp.where(cnt >= cap, cand, thr)

    thr = lax.fori_loop(0, 31, search, jnp.zeros((1, LANES), jnp.int32))
    gt = a >= pltpu.bitcast(thr + 1, F32)
    eq = (a >= pltpu.bitcast(thr, F32)) & jnp.logical_not(gt)
    need = cap - jnp.sum(jnp.where(gt, one, zero), axis=0, keepdims=True)
    ti = lax.broadcasted_iota(jnp.int32, (seq, seq), 0)
    tj = lax.broadcasted_iota(jnp.int32, (seq, seq), 1)
    before = jnp.where(tj < ti, 1.0, 0.0).astype(BF16)
    eq_rank = _dot(before, jnp.where(eq, one, zero).astype(BF16))
    sel = gt | (eq & (eq_rank < need))
    pos = _dot(before, jnp.where(sel, one, zero).astype(BF16))
    slot = jnp.where(sel, pos, -1.0)
    slot_t_ref[0] = slot
    slot_r_ref[0] = slot.T
    aff_r_ref[0] = a.T


def _route(aff_t, cap):
    ngroup, seq, _ = aff_t.shape
    tspec = pl.BlockSpec((1, seq, LANES), lambda g: (g, 0, 0))
    rspec = pl.BlockSpec((1, LANES, seq), lambda g: (g, 0, 0))
    return pl.pallas_call(
        functools.partial(_route_kernel, cap=cap),
        grid=(ngroup,),
        in_specs=[tspec],
        out_specs=[tspec, rspec, rspec],
        out_shape=[
            jax.ShapeDtypeStruct((ngroup, seq, LANES), F32),
            jax.ShapeDtypeStruct((ngroup, LANES, seq), F32),
            jax.ShapeDtypeStruct((ngroup, LANES, seq), F32),
        ],
        compiler_params=_params("arbitrary"),
        name="route",
    )(aff_t)


GATHER_ROWS = 512


def _gather_kernel(slot_ref, aff_ref, h_ref, x_ref, g_ref, p_ref, *, cap):
    seq = h_ref.shape[1]
    nslot = N_EXPERTS * cap
    sub = lax.broadcasted_iota(jnp.int32, (cap, seq), 0).astype(F32)
    one, zero = jnp.ones((cap, seq), F32), jnp.zeros((cap, seq), F32)
    for e in range(N_EXPERTS):
        row = jnp.broadcast_to(slot_ref[0, e:e + 1, :], (cap, seq))
        p_ref[e * cap:(e + 1) * cap, :] = jnp.where(row == sub, one, zero).astype(BF16)
    h = h_ref[0]
    per_chunk = GATHER_ROWS // cap
    for c in range(nslot // GATHER_ROWS):
        rows = slice(c * GATHER_ROWS, (c + 1) * GATHER_ROWS)
        x = _dot(p_ref[rows, :], h).astype(x_ref.dtype)
        for i in range(per_chunk):
            x_ref[c * per_chunk + i] = x[i * cap:(i + 1) * cap]
    a = aff_ref[0]
    a_hi = a.astype(BF16)
    r1 = a - a_hi.astype(F32)
    a_mid = r1.astype(BF16)
    a_lo = (r1 - a_mid.astype(F32)).astype(BF16)
    a3 = jnp.concatenate([a_hi, a_mid, a_lo], axis=0)
    g3 = lax.dot_general(a3, p_ref[...], _NT, preferred_element_type=F32)
    gfull = g3[0:N_EXPERTS] + g3[N_EXPERTS:2 * N_EXPERTS] + g3[2 * N_EXPERTS:3 * N_EXPERTS]
    ei = lax.broadcasted_iota(jnp.int32, (N_EXPERTS, nslot), 0)
    si = lax.broadcasted_iota(jnp.int32, (N_EXPERTS, nslot), 1)
    own = (si >= ei * cap) & (si < (ei + 1) * cap)
    g_ref[0] = jnp.sum(jnp.where(own, gfull, 0.0), axis=0, keepdims=True)


def _gather(slot_r, aff_r, h2, cap):
    nreq, seq, _ = h2.shape
    nslot = N_EXPERTS * cap
    grp = lambda b: (b // REQ_PER_GROUP, b % REQ_PER_GROUP, 0)
    x, g = pl.pallas_call(
        functools.partial(_gather_kernel, cap=cap),
        grid=(nreq,),
        in_specs=[
            pl.BlockSpec((1, N_EXPERTS, seq), grp),
            pl.BlockSpec((1, N_EXPERTS, seq), grp),
            pl.BlockSpec((1, seq, D_MODEL), lambda b: (b, 0, 0)),
        ],
        out_specs=[
            pl.BlockSpec((N_EXPERTS, cap, D_MODEL), lambda b: (0, b, 0)),
            pl.BlockSpec((1, 1, nslot), lambda b: (b, 0, 0)),
        ],
        out_shape=[
            jax.ShapeDtypeStruct((N_EXPERTS, nreq * cap, D_MODEL), BF16),
            jax.ShapeDtypeStruct((nreq, 1, nslot), F32),
        ],
        scratch_shapes=[pltpu.VMEM((nslot, seq), BF16)],
        compiler_params=_params("arbitrary"),
        name="gather",
    )(slot_r, aff_r, h2)
    return x, g


EXPERT_FCHUNK = 512
EXPERT_ROWS = 512
GATE_ROWS = 256


def _expert_kernel(xc_ref, xl_ref, g_ref, wg_ref, wu_ref, wd_ref, yc_ref, yl_ref,
                   acc_ref):
    e = pl.program_id(0)
    f = pl.program_id(1)
    nf = pl.num_programs(1)
    half = xc_ref.shape[1]

    @pl.when((e == 0) & (f == 0))
    def _():
        acc_ref[...] = jnp.zeros_like(acc_ref)

    ri = lax.broadcasted_iota(jnp.int32, (GATE_ROWS, GATE_ROWS), 0)
    ci = lax.broadcasted_iota(jnp.int32, (GATE_ROWS, GATE_ROWS), 1)
    for part, (x_ref, y_ref) in enumerate(((xc_ref, yc_ref), (xl_ref, yl_ref))):
        for i in range(half // EXPERT_ROWS):
            x = x_ref[0, i * EXPERT_ROWS:(i + 1) * EXPERT_ROWS, :]
            gq = _dot(x, wg_ref[0].astype(BF16))
            uq = _dot(x, wu_ref[0].astype(BF16))
            hid = (gq * _sigmoid(gq) * uq).astype(BF16)
            r0 = part * half + i * EXPERT_ROWS
            prev = jnp.where(f == 0, 0.0, acc_ref[r0:r0 + EXPERT_ROWS, :])
            acc = prev + _dot(hid, wd_ref[0].astype(BF16))
            acc_ref[r0:r0 + EXPERT_ROWS, :] = acc
            for j in range(EXPERT_ROWS // GATE_ROWS):
                t = r0 // GATE_ROWS + j
                grow = jnp.broadcast_to(g_ref[0, t:t + 1, :], (GATE_ROWS, GATE_ROWS))
                gcol = jnp.sum(jnp.where(ri == ci, grow, 0.0), axis=-1, keepdims=True)
                y = acc[j * GATE_ROWS:(j + 1) * GATE_ROWS] * gcol
                y_ref[0, i * EXPERT_ROWS + j * GATE_ROWS:i * EXPERT_ROWS + (j + 1) * GATE_ROWS, :] = (
                    y.astype(y_ref.dtype))


def _experts(xc, xl, g, w_gate, w_up, w_down):
    ne, half, _ = xc.shape
    fc = EXPERT_FCHUNK
    xspec = pl.BlockSpec((1, half, D_MODEL), lambda e, f: (e, 0, 0))
    return pl.pallas_call(
        _expert_kernel,
        grid=(ne, D_EXPERT // fc),
        in_specs=[
            xspec, xspec,
            pl.BlockSpec((1,) + g.shape[1:], lambda e, f: (e, 0, 0)),
            pl.BlockSpec((1, D_MODEL, fc), lambda e, f: (e, 0, f)),
            pl.BlockSpec((1, D_MODEL, fc), lambda e, f: (e, 0, f)),
            pl.BlockSpec((1, fc, D_MODEL), lambda e, f: (e, f, 0)),
        ],
        out_specs=[xspec, xspec],
        out_shape=[jax.ShapeDtypeStruct(xc.shape, BF16), jax.ShapeDtypeStruct(xl.shape, BF16)],
        scratch_shapes=[pltpu.VMEM((2 * half, D_MODEL), F32)],
        compiler_params=_params("arbitrary", "arbitrary"),
        name="experts",
    )(xc, xl, g, w_gate, w_up, w_down)


COMBINE_ROWS = 256


def _combine_kernel(x1_ref, mods_ref, slot_ref, y_ref, gf_ref, o_ref, *, cap):
    b = pl.program_id(0)
    nslot = N_EXPERTS * cap
    li = lax.broadcasted_iota(jnp.int32, (LANES, nslot), 0)
    si = lax.broadcasted_iota(jnp.int32, (LANES, nslot), 1)
    lane0 = (b % REQ_PER_GROUP) * N_EXPERTS
    expand = jnp.where(li == lane0 + si // cap, 1.0, 0.0).astype(BF16)
    slot_exp = _dot(slot_ref[0].astype(BF16), expand)
    want = (lax.broadcasted_iota(jnp.int32, slot_exp.shape, 1) & (cap - 1)).astype(F32)
    scat = jnp.where(slot_exp == want, 1.0, 0.0).astype(BF16)
    moe = _dot(scat, y_ref[...].reshape(nslot, D_MODEL))
    x2 = x1_ref[0] + mods_ref[0][5:6] * moe
    o_ref[0] = x2 * lax.rsqrt(jnp.mean(x2 * x2, axis=-1, keepdims=True) + EPS) * gf_ref[...]


def _combine(x1, mods, mod_row, slot_t, y, gf, cap):
    nreq, seq, _ = x1.shape
    nslot = N_EXPERTS * cap
    tm = COMBINE_ROWS
    tok = lambda b, j: (b, j, 0)
    return pl.pallas_call(
        functools.partial(_combine_kernel, cap=cap),
        grid=(nreq, seq // tm),
        in_specs=[
            pl.BlockSpec((1, tm, D_MODEL), tok),
            pl.BlockSpec((1, N_MOD, D_MODEL), lambda b, j: (mod_row(b), 0, 0)),
            pl.BlockSpec((1, tm, LANES), lambda b, j: (b // REQ_PER_GROUP, j, 0)),
            pl.BlockSpec((N_EXPERTS, cap, D_MODEL), lambda b, j: (0, b, 0)),
            pl.BlockSpec((1, D_MODEL), lambda b, j: (0, 0)),
        ],
        out_specs=pl.BlockSpec((1, tm, D_MODEL), tok),
        out_shape=jax.ShapeDtypeStruct(x1.shape, F32),
        compiler_params=_params("arbitrary", "arbitrary"),
        name="combine",
    )(x1, mods, slot_t, y, gf)


def _group_lanes(aff):
    nreq, seq, ne = aff.shape
    a = aff.reshape(nreq // REQ_PER_GROUP, REQ_PER_GROUP, seq, ne)
    return jnp.transpose(a, (0, 2, 1, 3)).reshape(nreq // REQ_PER_GROUP, seq, REQ_PER_GROUP * ne)


def kernel(x_prompt, x_sample, cache_ctx_k, cache_ctx_v, c, c_ctx, w_ada, b_ada, norm1_g, w_in,
           conv_dw_w, conv_dw_b, conv_ln_g, conv_ln_b, w_conv_pw, na_rpb, w_attn_o, w_out,
           norm2_g, w_router, w_gate_e, w_up_e, w_down_e, final_g):
    assert w_ada.shape[0] == 1, "single trunk layer"
    nctx, ctx_seq, _ = x_prompt.shape
    nlat, lat_seq, _ = x_sample.shape
    ctx_cap = EC_CAPACITY * ctx_seq // N_EXPERTS
    lat_cap = EC_CAPACITY * lat_seq // N_EXPERTS
    assert nctx * ctx_cap == nlat * lat_cap

    cond = jnp.zeros((2 * nlat, D_MODEL), F32).at[:nlat].set(c).at[nlat].set(c_ctx)
    mods = _ada(cond, w_ada[0], b_ada).reshape(2 * nlat, N_MOD, D_MODEL)
    ctx_row = lambda b: nlat
    lat_row = lambda b: b

    w_a = w_g = w_in[0].astype(BF16)
    w_pw = w_conv_pw[0].astype(BF16)
    w_ao = w_attn_o[0].astype(BF16)
    w_o = w_out[0].astype(BF16)
    w_r = w_router[0].astype(BF16)
    conv_args = (conv_dw_w[0], conv_dw_b, conv_ln_g, conv_ln_b)

    q_c, k_c, v_c, cact_c = _front(x_prompt, mods, ctx_row, norm1_g, w_a, *conv_args, F32,
                                   FRONT_CTX_PER_STEP)
    q_l, k_l, v_l, cact_l = _front(x_sample, mods, lat_row, norm1_g, w_a, *conv_args, BF16, 1)

    heads_c = _ctx_attn(q_c, k_c, v_c)
    tab = _bias_table(na_rpb[0])
    past = cache_ctx_k.shape[2]
    feat_major = lambda t: jnp.transpose(t[:, 0], (0, 2, 3, 1)).reshape(nlat, D_ATTN, past)
    heads_l = _na_attn(q_l, k_l, v_l, feat_major(cache_ctx_k), feat_major(cache_ctx_v), tab)

    merge_w = (w_g, w_pw, w_ao, w_o, w_r)
    flat = lambda t: t.reshape(1, nctx * ctx_seq, t.shape[-1])
    x1_c, h2_c, aff_c = _merge(flat(x_prompt), mods, ctx_row, norm1_g, norm2_g, flat(cact_c),
                               flat(heads_c), *merge_w)
    x1_c, h2_c, aff_c = (t.reshape(nctx, ctx_seq, t.shape[-1]) for t in (x1_c, h2_c, aff_c))
    x1_l, h2_l, aff_l = _merge(x_sample, mods, lat_row, norm1_g, norm2_g, cact_l, heads_l, *merge_w)

    slot_t_c, slot_r_c, aff_r_c = _route(_group_lanes(aff_c), ctx_cap)
    slot_t_l, slot_r_l, aff_r_l = _route(_group_lanes(aff_l), lat_cap)

    xg_c, g_c = _gather(slot_r_c, aff_r_c, h2_c, ctx_cap)
    xg_l, g_l = _gather(slot_r_l, aff_r_l, h2_l, lat_cap)

    def gate_rows(g, nreq, cap):
        g = jnp.transpose(g.reshape(nreq, N_EXPERTS, cap), (1, 0, 2))
        return g.reshape(N_EXPERTS, nreq * cap // GATE_ROWS, GATE_ROWS)

    gates = jnp.concatenate([gate_rows(g_c, nctx, ctx_cap), gate_rows(g_l, nlat, lat_cap)], axis=1)
    y_c, y_l = _experts(xg_c, xg_l, gates, w_gate_e[0], w_up_e[0], w_down_e[0])

    gf = final_g.reshape(1, D_MODEL)
    y_prompt = _combine(x1_c, mods, ctx_row, slot_t_c, y_c, gf, ctx_cap)
    y_sample = _combine(x1_l, mods, lat_row, slot_t_l, y_l, gf, lat_cap)

    state_k = k_c.reshape(nctx, 1, ctx_seq, N_HEADS, HEAD_DIM)
    state_v = v_c.reshape(nctx, 1, ctx_seq, N_HEADS, HEAD_DIM)
    return (y_prompt, y_sample, state_k, state_v)
```

```python
import functools

import jax
import jax.numpy as jnp
from jax import lax
from jax.experimental import pallas as pl
from jax.experimental.pallas import tpu as pltpu

F32 = jnp.float32
BF16 = jnp.bfloat16

D_MODEL = 1024
D_CONV = 512
CONV_WIDTH = 31
N_HEADS = 8
HEAD_DIM = 64
D_ATTN = N_HEADS * HEAD_DIM
GRID_W = 64
NA_ROWS = 8
NA_COLS = 16
N_EXPERTS = 16
D_EXPERT = 2048
EC_CAPACITY = 2
N_MOD = 6
EPS = 1e-6
D_PROJ_A = 2 * D_CONV + 3 * D_ATTN
D_IN = D_PROJ_A + 2 * D_MODEL
LANES = 128
REQ_PER_GROUP = LANES // N_EXPERTS
MASK_NEG = -1e30
VMEM_LIMIT = 56 * 1024 * 1024

_NT = (((1,), (1,)), ((), ()))


def _params(*sem):
    return pltpu.CompilerParams(dimension_semantics=sem, vmem_limit_bytes=VMEM_LIMIT)


def _dot(a, b):
    return jnp.dot(a, b, preferred_element_type=F32)


def _sigmoid(x):
    return 1.0 / (1.0 + jnp.exp(-x))


def _split_bf16(x):
    hi = x.astype(BF16)
    lo = (x - hi.astype(F32)).astype(BF16)
    return hi, lo


def _rms_mod(x, g, shift, scale):
    xn = x * lax.rsqrt(jnp.mean(x * x, axis=-1, keepdims=True) + EPS) * g
    return xn * (1.0 + scale) + shift


def _ada_kernel(cond_ref, w_ref, b_ref, o_ref):
    c = cond_ref[...]
    a = c * _sigmoid(c)
    ah, al = _split_bf16(a)
    wh, wl = _split_bf16(w_ref[...])
    o_ref[...] = _dot(ah, wh) + _dot(al, wh) + _dot(ah, wl) + b_ref[...]


def _ada(cond, w_ada, b_ada):
    n = w_ada.shape[1]
    bn = 1024
    return pl.pallas_call(
        _ada_kernel,
        grid=(n // bn,),
        in_specs=[
            pl.BlockSpec(cond.shape, lambda j: (0, 0)),
            pl.BlockSpec((D_MODEL, bn), lambda j: (0, j)),
            pl.BlockSpec((1, bn), lambda j: (0, j)),
        ],
        out_specs=pl.BlockSpec((cond.shape[0], bn), lambda j: (0, j)),
        out_shape=jax.ShapeDtypeStruct((cond.shape[0], n), F32),
        compiler_params=_params("arbitrary"),
        name="ada",
    )(cond, w_ada, b_ada)


CONV_PAD = 16
FRONT_ROWS = 256
CONV_ROWS = 128
SUBLANES = 8
FRONT_CTX_PER_STEP = 4


def _front_kernel(x_ref, mods_ref, g1_ref, w_ref, dww_ref, dwb_ref, lng_ref, lnb_ref,
                  q_ref, k_ref, v_ref, c_ref, *gpad_refs, seq):
    nreq = x_ref.shape[0]
    m = mods_ref[0]
    shift, scale = m[0:1], m[1:2]
    zeros = jnp.zeros((CONV_PAD, D_CONV), F32)
    for gpad_ref in gpad_refs:
        gpad_ref[0:CONV_PAD, :] = zeros
        gpad_ref[CONV_PAD + seq:CONV_PAD + seq + CONV_PAD, :] = zeros

    def proj(r, i):
        rows = slice(i * FRONT_ROWS, (i + 1) * FRONT_ROWS)
        x = x_ref[r, rows, :]
        h = _rms_mod(x, g1_ref[...], shift, scale).astype(BF16)
        u = _dot(h, w_ref[...])
        glu = u[:, :D_CONV] * _sigmoid(u[:, D_CONV:2 * D_CONV])
        g0 = CONV_PAD + i * FRONT_ROWS
        gpad_refs[r][g0:g0 + FRONT_ROWS, :] = glu
        o = 2 * D_CONV
        q_ref[r, rows, :] = (u[:, o:o + D_ATTN] * (HEAD_DIM ** -0.5)).astype(q_ref.dtype)
        k_ref[r, rows, :] = u[:, o + D_ATTN:o + 2 * D_ATTN].astype(k_ref.dtype)
        v_ref[r, rows, :] = u[:, o + 2 * D_ATTN:o + 3 * D_ATTN].astype(v_ref.dtype)

    def conv(r, j):
        g0 = j * CONV_ROWS
        cols = []
        for cb in range(D_CONV // LANES):
            cs = slice(cb * LANES, (cb + 1) * LANES)
            win = gpad_refs[r][g0:g0 + CONV_ROWS + 2 * CONV_PAD, cs]
            acc = None
            for s in range(SUBLANES):
                nrow = CONV_ROWS + (SUBLANES if s else 0)
                z = None
                for a in range(2 * CONV_PAD // SUBLANES):
                    t = SUBLANES * a + s - (CONV_PAD - CONV_WIDTH // 2)
                    if 0 <= t < CONV_WIDTH:
                        term = win[SUBLANES * a:SUBLANES * a + nrow] * dww_ref[t:t + 1, cs]
                        z = term if z is None else z + term
                z = z[s:s + CONV_ROWS]
                acc = z if acc is None else acc + z
            cols.append(acc)
        y = jnp.concatenate(cols, axis=1) + dwb_ref[...]
        mu = jnp.mean(y, axis=-1, keepdims=True)
        yc = y - mu
        var = jnp.mean(yc * yc, axis=-1, keepdims=True)
        z = yc * lax.rsqrt(var + EPS) * lng_ref[...] + lnb_ref[...]
        c_ref[r, j * CONV_ROWS:(j + 1) * CONV_ROWS, :] = (z * _sigmoid(z)).astype(c_ref.dtype)

    chunks = [(r, i) for r in range(nreq) for i in range(seq // FRONT_ROWS)]
    blocks = [(r, j) for r in range(nreq) for j in range(seq // CONV_ROWS)]

    def ready(block, done):
        r, j = block
        last = min((j + 1) * CONV_ROWS + CONV_WIDTH // 2, seq) - 1
        return (r, last // FRONT_ROWS) in done

    done, pending = set(), list(blocks)
    proj(*chunks[0])
    for n, chunk in enumerate(chunks):
        done.add(chunk)
        if n + 1 < len(chunks):
            proj(*chunks[n + 1])
        now = [b for b in pending if ready(b, done)]
        pending = [b for b in pending if b not in now]
        for b in now:
            conv(*b)


def _front(x, mods, mod_row, g1, w_a, dww, dwb, lng, lnb, kv_dtype, per_step):
    nreq, seq, _ = x.shape
    const2 = lambda b: (0, 0)
    tok = lambda b: (b, 0, 0)
    out_tok = pl.BlockSpec((per_step, seq, D_ATTN), tok)
    return pl.pallas_call(
        functools.partial(_front_kernel, seq=seq),
        grid=(nreq // per_step,),
        in_specs=[
            pl.BlockSpec((per_step, seq, D_MODEL), tok),
            pl.BlockSpec((1, N_MOD, D_MODEL), lambda b: (mod_row(b * per_step), 0, 0)),
            pl.BlockSpec((1, D_MODEL), const2),
            pl.BlockSpec((D_MODEL, D_PROJ_A), const2),
            pl.BlockSpec((CONV_WIDTH, D_CONV), const2),
            pl.BlockSpec((1, D_CONV), const2),
            pl.BlockSpec((1, D_CONV), const2),
            pl.BlockSpec((1, D_CONV), const2),
        ],
        out_specs=[out_tok, out_tok, out_tok, out_tok],
        out_shape=[
            jax.ShapeDtypeStruct((nreq, seq, D_ATTN), BF16),
            jax.ShapeDtypeStruct((nreq, seq, D_ATTN), kv_dtype),
            jax.ShapeDtypeStruct((nreq, seq, D_ATTN), kv_dtype),
            jax.ShapeDtypeStruct((nreq, seq, D_CONV), BF16),
        ],
        scratch_shapes=[pltpu.VMEM((seq + 2 * CONV_PAD, D_CONV), F32)] * per_step,
        compiler_params=_params("arbitrary"),
        name="front",
    )(x, mods, g1, w_a, dww, dwb, lng, lnb)


def _pair_scores_q(q_pair):
    lane = lax.broadcasted_iota(jnp.int32, q_pair.shape, 1)
    zero = jnp.zeros_like(q_pair)
    return jnp.concatenate([jnp.where(lane < HEAD_DIM, q_pair, zero),
                            jnp.where(lane >= HEAD_DIM, q_pair, zero)], axis=0)


def _pair_merge(o2, t):
    lane = lax.broadcasted_iota(jnp.int32, (t, LANES), 1)
    return jnp.where(lane < HEAD_DIM, o2[:t], o2[t:])


def _ctx_attn_kernel(q_ref, k_ref, v_ref, o_ref):
    seq = q_ref.shape[1]
    pairs = range(N_HEADS // 2)
    sls = [slice(p * LANES, (p + 1) * LANES) for p in pairs]
    s = [lax.dot_general(_pair_scores_q(q_ref[0, :, sls[p]]), k_ref[0, :, sls[p]].astype(BF16), _NT,
                         preferred_element_type=F32) for p in pairs]
    e = [jnp.exp(s[p] - jnp.max(s[p], axis=-1, keepdims=True)) for p in pairs]
    den = [jnp.sum(e[p], axis=-1, keepdims=True) for p in pairs]
    for p in pairs:
        o2 = _dot(e[p].astype(BF16), v_ref[0, :, sls[p]].astype(BF16)) / den[p]
        o_ref[0, :, sls[p]] = _pair_merge(o2, seq).astype(o_ref.dtype)


def _ctx_attn(q, k, v):
    nreq, seq, _ = q.shape
    spec = pl.BlockSpec((1, seq, D_ATTN), lambda b: (b, 0, 0))
    return pl.pallas_call(
        _ctx_attn_kernel,
        grid=(nreq,),
        in_specs=[spec, spec, spec],
        out_specs=spec,
        out_shape=jax.ShapeDtypeStruct((nreq, seq, D_ATTN), BF16),
        compiler_params=_params("arbitrary"),
        name="ctx_attn",
    )(q, k, v)


N_DR = 2 * NA_ROWS - 1
N_DC = 2 * NA_COLS - 1


def _bias_kernel(rpb_ref, o_ref, t2_ref):
    h = pl.program_id(0)
    qi = lax.broadcasted_iota(jnp.int32, (GRID_W, LANES), 0)
    lane = lax.broadcasted_iota(jnp.int32, (GRID_W, LANES), 1)
    kc = lane & (GRID_W - 1)
    d = kc - qi
    cs = jnp.clip(qi - NA_COLS // 2, 0, GRID_W - NA_COLS)
    inside = (kc >= cs) & (kc < cs + NA_COLS)
    group = 5
    for dr0 in range(0, N_DR, group):
        ts = [jnp.full((GRID_W, LANES), MASK_NEG, F32)] * group
        for j in range(N_DC):
            hit = d == j - (NA_COLS - 1)
            ts = [jnp.where(hit, rpb_ref[h * (N_DR * N_DC) + (dr0 + g) * N_DC + j], ts[g])
                  for g in range(group)]
        for g in range(group):
            t2_ref[dr0 + g] = jnp.where(inside, ts[g], MASK_NEG)
    for o in range(NA_ROWS):
        for jj in range(NA_ROWS // 2):
            o_ref[0, o, :, jj * LANES:(jj + 1) * LANES] = jnp.where(
                lane < GRID_W, t2_ref[o + 2 * jj], t2_ref[o + 2 * jj + 1])


def _bias_table(rpb):
    return pl.pallas_call(
        _bias_kernel,
        grid=(N_HEADS,),
        in_specs=[pl.BlockSpec(memory_space=pltpu.SMEM)],
        out_specs=pl.BlockSpec((1, NA_ROWS, GRID_W, NA_ROWS * GRID_W), lambda h: (h, 0, 0, 0)),
        out_shape=jax.ShapeDtypeStruct((N_HEADS, NA_ROWS, GRID_W, NA_ROWS * GRID_W), F32),
        scratch_shapes=[pltpu.VMEM((N_DR, GRID_W, LANES), F32)],
        compiler_params=_params("arbitrary"),
        name="na_bias",
    )(rpb.reshape(-1))


NA_PAIR_GROUP = 4


def _na_attn_kernel(q_ref, k_ref, v_ref, kct_ref, vct_ref, tab_ref, o_ref, kcb_ref, vcb_ref, *, rows):
    r = pl.program_id(1)

    @pl.when(r == 0)
    def _():
        kcb_ref[...] = kct_ref[0].astype(BF16)
        vcb_ref[...] = vct_ref[0].astype(BF16)

    rs = jnp.clip(r - NA_ROWS // 2, 0, rows - NA_ROWS)
    k0 = pl.multiple_of(rs * GRID_W, GRID_W)
    off = rs - r + (NA_ROWS - 1)
    nloc = NA_ROWS * GRID_W
    sls = [slice(p * LANES, (p + 1) * LANES) for p in range(N_HEADS // 2)]

    def group(pairs):
        s_loc, s_ctx = {}, {}
        for p in pairs:
            q2 = _pair_scores_q(q_ref[0, :, sls[p]])
            bias = jnp.concatenate([tab_ref[2 * p, off], tab_ref[2 * p + 1, off]], axis=0)
            kl = k_ref[0, pl.ds(k0, nloc), sls[p]]
            s_loc[p] = lax.dot_general(q2, kl, _NT, preferred_element_type=F32) + bias
            s_ctx[p] = _dot(q2, kcb_ref[sls[p], :])
        mx = {p: jnp.maximum(jnp.max(s_loc[p], axis=-1, keepdims=True),
                             jnp.max(s_ctx[p], axis=-1, keepdims=True)) for p in pairs}
        e_loc = {p: jnp.exp(s_loc[p] - mx[p]) for p in pairs}
        e_ctx = {p: jnp.exp(s_ctx[p] - mx[p]) for p in pairs}
        den = {p: jnp.sum(e_loc[p], axis=-1, keepdims=True) + jnp.sum(e_ctx[p], axis=-1, keepdims=True)
               for p in pairs}
        for p in pairs:
            vl = v_ref[0, pl.ds(k0, nloc), sls[p]]
            o_ctx = lax.dot_general(e_ctx[p].astype(BF16), vcb_ref[sls[p], :], _NT,
                                    preferred_element_type=F32)
            o2 = (_dot(e_loc[p].astype(BF16), vl) + o_ctx) / den[p]
            o_ref[0, :, sls[p]] = _pair_merge(o2, GRID_W).astype(o_ref.dtype)

    for g in range(0, N_HEADS // 2, NA_PAIR_GROUP):
        group(range(g, g + NA_PAIR_GROUP))


def _na_attn(q, k, v, kc, vc, tab):
    nreq, seq, _ = q.shape
    rows = seq // GRID_W
    past = kc.shape[2]
    qspec = pl.BlockSpec((1, GRID_W, D_ATTN), lambda b, r: (b, r, 0))
    kvspec = pl.BlockSpec((1, seq, D_ATTN), lambda b, r: (b, 0, 0))
    cspec = pl.BlockSpec((1, D_ATTN, past), lambda b, r: (b, 0, 0))
    return pl.pallas_call(
        functools.partial(_na_attn_kernel, rows=rows),
        grid=(nreq, rows),
        in_specs=[qspec, kvspec, kvspec, cspec, cspec,
                  pl.BlockSpec(tab.shape, lambda b, r: (0, 0, 0, 0))],
        out_specs=qspec,
        out_shape=jax.ShapeDtypeStruct((nreq, seq, D_ATTN), BF16),
        scratch_shapes=[pltpu.VMEM((D_ATTN, past), BF16), pltpu.VMEM((D_ATTN, past), BF16)],
        compiler_params=_params("arbitrary", "arbitrary"),
        name="na_attn",
    )(q, k, v, kc, vc, tab)


MERGE_ROWS = 512
MERGE_SUB = 256


def _merge_kernel(x_ref, mods_ref, g1_ref, g2_ref, c_ref, a_ref, wg_ref, wpw_ref, wao_ref,
                  wout_ref, wr_ref, x1_ref, h2_ref, aff_ref):
    m = mods_ref[0]
    subs = range(MERGE_ROWS // MERGE_SUB)
    rows = [slice(i * MERGE_SUB, (i + 1) * MERGE_SUB) for i in subs]
    x = [x_ref[0, rows[i], :] for i in subs]
    h = [_rms_mod(x[i], g1_ref[...], m[0:1], m[1:2]).astype(BF16) for i in subs]
    gates = [_sigmoid(_dot(h[i], wg_ref[:, D_PROJ_A - D_IN // 2:])) for i in subs]
    conv_out = [_dot(c_ref[0, rows[i], :], wpw_ref[...]) for i in subs]
    attn_out = [_dot(a_ref[0, rows[i], :], wao_ref[...]) for i in subs]
    merged = [(gates[i][:, :D_MODEL] * conv_out[i] + gates[i][:, D_MODEL:] * attn_out[i]).astype(BF16)
              for i in subs]
    x1 = [x[i] + m[2:3] * _dot(merged[i], wout_ref[...]) for i in subs]
    h2 = [_rms_mod(x1[i], g2_ref[...], m[3:4], m[4:5]).astype(BF16) for i in subs]
    logits = [_dot(h2[i], wr_ref[...]) for i in subs]
    e = [jnp.exp(logits[i] - jnp.max(logits[i], axis=-1, keepdims=True)) for i in subs]
    for i in subs:
        x1_ref[0, rows[i], :] = x1[i]
        h2_ref[0, rows[i], :] = h2[i]
        aff_ref[0, rows[i], :] = e[i] / jnp.sum(e[i], axis=-1, keepdims=True)


def _merge(x, mods, mod_row, g1, g2, cact, heads, w_g, w_pw, w_ao, w_out, w_r):
    nreq, seq, _ = x.shape
    tm = MERGE_ROWS
    tok = lambda b, j: (b, j, 0)
    const2 = lambda b, j: (0, 0)
    return pl.pallas_call(
        _merge_kernel,
        grid=(nreq, seq // tm),
        in_specs=[
            pl.BlockSpec((1, tm, D_MODEL), tok),
            pl.BlockSpec((1, N_MOD, D_MODEL), lambda b, j: (mod_row(b), 0, 0)),
            pl.BlockSpec((1, D_MODEL), const2),
            pl.BlockSpec((1, D_MODEL), const2),
            pl.BlockSpec((1, tm, D_CONV), tok),
            pl.BlockSpec((1, tm, D_ATTN), tok),
            pl.BlockSpec((D_MODEL, D_IN // 2), lambda b, j: (0, 1)),
            pl.BlockSpec(w_pw.shape, const2),
            pl.BlockSpec(w_ao.shape, const2),
            pl.BlockSpec(w_out.shape, const2),
            pl.BlockSpec(w_r.shape, const2),
        ],
        out_specs=[
            pl.BlockSpec((1, tm, D_MODEL), tok),
            pl.BlockSpec((1, tm, D_MODEL), tok),
            pl.BlockSpec((1, tm, N_EXPERTS), tok),
        ],
        out_shape=[
            jax.ShapeDtypeStruct((nreq, seq, D_MODEL), F32),
            jax.ShapeDtypeStruct((nreq, seq, D_MODEL), BF16),
            jax.ShapeDtypeStruct((nreq, seq, N_EXPERTS), F32),
        ],
        compiler_params=_params("arbitrary", "arbitrary"),
        name="merge",
    )(x, mods, g1, g2, cact, heads, w_g, w_pw, w_ao, w_out, w_r)


def _route_kernel(aff_ref, slot_t_ref, slot_r_ref, aff_r_ref, *, cap):
    a = aff_ref[0]
    seq = a.shape[0]
    one, zero = jnp.ones_like(a), jnp.zeros_like(a)

    def search(i, thr):
        cand = thr | jnp.left_shift(jnp.int32(1), 30 - i)
        cnt = jnp.sum(jnp.where(a >= pltpu.bitcast(cand, F32), one, zero), axis=0, keepdims=True)
        return jnp.where(cnt >= cap, cand, thr)

    thr = lax.fori_loop(0, 31, search, jnp.zeros((1, LANES), jnp.int32))
    gt = a >= pltpu.bitcast(thr + 1, F32)
    eq = (a >= pltpu.bitcast(thr, F32)) & jnp.logical_not(gt)
    need = cap - jnp.sum(jnp.where(gt, one, zero), axis=0, keepdims=True)
    ti = lax.broadcasted_iota(jnp.int32, (seq, seq), 0)
    tj = lax.broadcasted_iota(jnp.int32, (seq, seq), 1)
    before = jnp.where(tj < ti, 1.0, 0.0).astype(BF16)
    eq_rank = _dot(before, jnp.where(eq, one, zero).astype(BF16))
    sel = gt | (eq & (eq_rank < need))
    pos = _dot(before, jnp.where(sel, one, zero).astype(BF16))
    slot = jnp.where(sel, pos, -1.0)
    slot_t_ref[0] = slot
    slot_r_ref[0] = slot.T
    aff_r_ref[0] = a.T


def _route(aff_t, cap):
    ngroup, seq, _ = aff_t.shape
    tspec = pl.BlockSpec((1, seq, LANES), lambda g: (g, 0, 0))
    rspec = pl.BlockSpec((1, LANES, seq), lambda g: (g, 0, 0))
    return pl.pallas_call(
        functools.partial(_route_kernel, cap=cap),
        grid=(ngroup,),
        in_specs=[tspec],
        out_specs=[tspec, rspec, rspec],
        out_shape=[
            jax.ShapeDtypeStruct((ngroup, seq, LANES), F32),
            jax.ShapeDtypeStruct((ngroup, LANES, seq), F32),
            jax.ShapeDtypeStruct((ngroup, LANES, seq), F32),
        ],
        compiler_params=_params("arbitrary"),
        name="route",
    )(aff_t)


GATHER_ROWS = 512


def _gather_kernel(slot_ref, aff_ref, h_ref, x_ref, g_ref, *p_refs, cap):
    per_step, seq, _ = h_ref.shape
    nslot = N_EXPERTS * cap
    sub = lax.broadcasted_iota(jnp.int32, (cap, seq), 0).astype(F32)
    one, zero = jnp.ones((cap, seq), F32), jnp.zeros((cap, seq), F32)
    ei = lax.broadcasted_iota(jnp.int32, (N_EXPERTS, nslot), 0)
    si = lax.broadcasted_iota(jnp.int32, (N_EXPERTS, nslot), 1)
    own = (si >= ei * cap) & (si < (ei + 1) * cap)
    per_chunk = GATHER_ROWS // cap
    for r, p_ref in enumerate(p_refs):
        for e in range(N_EXPERTS):
            row = jnp.broadcast_to(slot_ref[0, r * N_EXPERTS + e:r * N_EXPERTS + e + 1, :], (cap, seq))
            p_ref[e * cap:(e + 1) * cap, :] = jnp.where(row == sub, one, zero).astype(BF16)
        h = h_ref[r]
        for c in range(nslot // GATHER_ROWS):
            rows = slice(c * GATHER_ROWS, (c + 1) * GATHER_ROWS)
            x = _dot(p_ref[rows, :], h).astype(x_ref.dtype)
            for i in range(per_chunk):
                x_ref[c * per_chunk + i, r * cap:(r + 1) * cap, :] = x[i * cap:(i + 1) * cap]
        a = aff_ref[0, r * N_EXPERTS:(r + 1) * N_EXPERTS, :]
        a_hi = a.astype(BF16)
        r1 = a - a_hi.astype(F32)
        a_mid = r1.astype(BF16)
        a_lo = (r1 - a_mid.astype(F32)).astype(BF16)
        a3 = jnp.concatenate([a_hi, a_mid, a_lo], axis=0)
        g3 = lax.dot_general(a3, p_ref[...], _NT, preferred_element_type=F32)
        gfull = g3[0:N_EXPERTS] + g3[N_EXPERTS:2 * N_EXPERTS] + g3[2 * N_EXPERTS:3 * N_EXPERTS]
        g_ref[r] = jnp.sum(jnp.where(own, gfull, 0.0), axis=0, keepdims=True)


def _gather(slot_r, aff_r, h2, cap, per_step):
    nreq, seq, _ = h2.shape
    nslot = N_EXPERTS * cap
    steps_per_group = REQ_PER_GROUP // per_step
    grp = lambda s: (s // steps_per_group, s % steps_per_group, 0)
    x, g = pl.pallas_call(
        functools.partial(_gather_kernel, cap=cap),
        grid=(nreq // per_step,),
        in_specs=[
            pl.BlockSpec((1, per_step * N_EXPERTS, seq), grp),
            pl.BlockSpec((1, per_step * N_EXPERTS, seq), grp),
            pl.BlockSpec((per_step, seq, D_MODEL), lambda s: (s, 0, 0)),
        ],
        out_specs=[
            pl.BlockSpec((N_EXPERTS, per_step * cap, D_MODEL), lambda s: (0, s, 0)),
            pl.BlockSpec((per_step, 1, nslot), lambda s: (s, 0, 0)),
        ],
        out_shape=[
            jax.ShapeDtypeStruct((N_EXPERTS, nreq * cap, D_MODEL), BF16),
            jax.ShapeDtypeStruct((nreq, 1, nslot), F32),
        ],
        scratch_shapes=[pltpu.VMEM((nslot, seq), BF16)] * per_step,
        compiler_params=_params("arbitrary"),
        name="gather",
    )(slot_r, aff_r, h2)
    return x, g


EXPERT_FCHUNK = 512
EXPERT_ROWS = 512
GATE_ROWS = 256


def _expert_kernel(xc_ref, xl_ref, g_ref, wg_ref, wu_ref, wd_ref, yc_ref, yl_ref,
                   acc_ref):
    e = pl.program_id(0)
    f = pl.program_id(1)
    nf = pl.num_programs(1)
    half = xc_ref.shape[1]

    @pl.when((e == 0) & (f == 0))
    def _():
        acc_ref[...] = jnp.zeros_like(acc_ref)

    ri = lax.broadcasted_iota(jnp.int32, (GATE_ROWS, GATE_ROWS), 0)
    ci = lax.broadcasted_iota(jnp.int32, (GATE_ROWS, GATE_ROWS), 1)
    for part, (x_ref, y_ref) in enumerate(((xc_ref, yc_ref), (xl_ref, yl_ref))):
        for i in range(half // EXPERT_ROWS):
            x = x_ref[0, i * EXPERT_ROWS:(i + 1) * EXPERT_ROWS, :]
            gq = _dot(x, wg_ref[0].astype(BF16))
            uq = _dot(x, wu_ref[0].astype(BF16))
            hid = (gq * _sigmoid(gq) * uq).astype(BF16)
            r0 = part * half + i * EXPERT_ROWS
            prev = jnp.where(f == 0, 0.0, acc_ref[r0:r0 + EXPERT_ROWS, :])
            acc = prev + _dot(hid, wd_ref[0].astype(BF16))
            acc_ref[r0:r0 + EXPERT_ROWS, :] = acc
            for j in range(EXPERT_ROWS // GATE_ROWS):
                t = r0 // GATE_ROWS + j
                grow = jnp.broadcast_to(g_ref[0, t:t + 1, :], (GATE_ROWS, GATE_ROWS))
                gcol = jnp.sum(jnp.where(ri == ci, grow, 0.0), axis=-1, keepdims=True)
                y = acc[j * GATE_ROWS:(j + 1) * GATE_ROWS] * gcol
                y_ref[0, i * EXPERT_ROWS + j * GATE_ROWS:i * EXPERT_ROWS + (j + 1) * GATE_ROWS, :] = (
                    y.astype(y_ref.dtype))


def _experts(xc, xl, g, w_gate, w_up, w_down):
    ne, half, _ = xc.shape
    fc = EXPERT_FCHUNK
    xspec = pl.BlockSpec((1, half, D_MODEL), lambda e, f: (e, 0, 0))
    return pl.pallas_call(
        _expert_kernel,
        grid=(ne, D_EXPERT // fc),
        in_specs=[
            xspec, xspec,
            pl.BlockSpec((1,) + g.shape[1:], lambda e, f: (e, 0, 0)),
            pl.BlockSpec((1, D_MODEL, fc), lambda e, f: (e, 0, f)),
            pl.BlockSpec((1, D_MODEL, fc), lambda e, f: (e, 0, f)),
            pl.BlockSpec((1, fc, D_MODEL), lambda e, f: (e, f, 0)),
        ],
        out_specs=[xspec, xspec],
        out_shape=[jax.ShapeDtypeStruct(xc.shape, BF16), jax.ShapeDtypeStruct(xl.shape, BF16)],
        scratch_shapes=[pltpu.VMEM((2 * half, D_MODEL), F32)],
        compiler_params=_params("arbitrary", "arbitrary"),
        name="experts",
    )(xc, xl, g, w_gate, w_up, w_down)


COMBINE_ROWS = 256
MOE_CTX_PER_STEP = 4


def _combine_kernel(x1_ref, mods_ref, slot_ref, y_ref, gf_ref, o_ref, *, cap):
    per_step = x1_ref.shape[0]
    nslot = N_EXPERTS * cap
    li = lax.broadcasted_iota(jnp.int32, (LANES, nslot), 0)
    si = lax.broadcasted_iota(jnp.int32, (LANES, nslot), 1)
    for r in range(per_step):
        b = pl.program_id(0) * per_step + r
        lane0 = (b % REQ_PER_GROUP) * N_EXPERTS
        expand = jnp.where(li == lane0 + si // cap, 1.0, 0.0).astype(BF16)
        slot_exp = _dot(slot_ref[0].astype(BF16), expand)
        want = (lax.broadcasted_iota(jnp.int32, slot_exp.shape, 1) & (cap - 1)).astype(F32)
        scat = jnp.where(slot_exp == want, 1.0, 0.0).astype(BF16)
        moe = _dot(scat, y_ref[:, r * cap:(r + 1) * cap, :].reshape(nslot, D_MODEL))
        x2 = x1_ref[r] + mods_ref[0][5:6] * moe
        o_ref[r] = x2 * lax.rsqrt(jnp.mean(x2 * x2, axis=-1, keepdims=True) + EPS) * gf_ref[...]


def _combine(x1, mods, mod_row, slot_t, y, gf, cap, per_step):
    nreq, seq, _ = x1.shape
    tm = COMBINE_ROWS
    assert per_step == 1 or seq == tm
    tok = lambda s, j: (s, j, 0)
    return pl.pallas_call(
        functools.partial(_combine_kernel, cap=cap),
        grid=(nreq // per_step, seq // tm),
        in_specs=[
            pl.BlockSpec((per_step, tm, D_MODEL), tok),
            pl.BlockSpec((1, N_MOD, D_MODEL), lambda s, j: (mod_row(s * per_step), 0, 0)),
            pl.BlockSpec((1, tm, LANES), lambda s, j: (s * per_step // REQ_PER_GROUP, j, 0)),
            pl.BlockSpec((N_EXPERTS, per_step * cap, D_MODEL), lambda s, j: (0, s, 0)),
            pl.BlockSpec((1, D_MODEL), lambda s, j: (0, 0)),
        ],
        out_specs=pl.BlockSpec((per_step, tm, D_MODEL), tok),
        out_shape=jax.ShapeDtypeStruct(x1.shape, F32),
        compiler_params=_params("arbitrary", "arbitrary"),
        name="combine",
    )(x1, mods, slot_t, y, gf)


def _group_lanes(aff):
    nreq, seq, ne = aff.shape
    a = aff.reshape(nreq // REQ_PER_GROUP, REQ_PER_GROUP, seq, ne)
    return jnp.transpose(a, (0, 2, 1, 3)).reshape(nreq // REQ_PER_GROUP, seq, REQ_PER_GROUP * ne)


def kernel(x_prompt, x_sample, cache_ctx_k, cache_ctx_v, c, c_ctx, w_ada, b_ada, norm1_g, w_in,
           conv_dw_w, conv_dw_b, conv_ln_g, conv_ln_b, w_conv_pw, na_rpb, w_attn_o, w_out,
           norm2_g, w_router, w_gate_e, w_up_e, w_down_e, final_g):
    assert w_ada.shape[0] == 1, "single trunk layer"
    nctx, ctx_seq, _ = x_prompt.shape
    nlat, lat_seq, _ = x_sample.shape
    ctx_cap = EC_CAPACITY * ctx_seq // N_EXPERTS
    lat_cap = EC_CAPACITY * lat_seq // N_EXPERTS
    assert nctx * ctx_cap == nlat * lat_cap

    cond = jnp.zeros((2 * nlat, D_MODEL), F32).at[:nlat].set(c).at[nlat].set(c_ctx)
    mods = _ada(cond, w_ada[0], b_ada).reshape(2 * nlat, N_MOD, D_MODEL)
    ctx_row = lambda b: nlat
    lat_row = lambda b: b

    w_a = w_g = w_in[0].astype(BF16)
    w_pw = w_conv_pw[0].astype(BF16)
    w_ao = w_attn_o[0].astype(BF16)
    w_o = w_out[0].astype(BF16)
    w_r = w_router[0].astype(BF16)
    conv_args = (conv_dw_w[0], conv_dw_b, conv_ln_g, conv_ln_b)

    q_c, k_c, v_c, cact_c = _front(x_prompt, mods, ctx_row, norm1_g, w_a, *conv_args, F32,
                                   FRONT_CTX_PER_STEP)
    q_l, k_l, v_l, cact_l = _front(x_sample, mods, lat_row, norm1_g, w_a, *conv_args, BF16, 1)

    heads_c = _ctx_attn(q_c, k_c, v_c)
    tab = _bias_table(na_rpb[0])
    past = cache_ctx_k.shape[2]
    feat_major = lambda t: jnp.transpose(t[:, 0], (0, 2, 3, 1)).reshape(nlat, D_ATTN, past)
    heads_l = _na_attn(q_l, k_l, v_l, feat_major(cache_ctx_k), feat_major(cache_ctx_v), tab)

    merge_w = (w_g, w_pw, w_ao, w_o, w_r)
    flat = lambda t: t.reshape(1, nctx * ctx_seq, t.shape[-1])
    x1_c, h2_c, aff_c = _merge(flat(x_prompt), mods, ctx_row, norm1_g, norm2_g, flat(cact_c),
                               flat(heads_c), *merge_w)
    x1_c, h2_c, aff_c = (t.reshape(nctx, ctx_seq, t.shape[-1]) for t in (x1_c, h2_c, aff_c))
    x1_l, h2_l, aff_l = _merge(x_sample, mods, lat_row, norm1_g, norm2_g, cact_l, heads_l, *merge_w)

    slot_t_c, slot_r_c, aff_r_c = _route(_group_lanes(aff_c), ctx_cap)
    slot_t_l, slot_r_l, aff_r_l = _route(_group_lanes(aff_l), lat_cap)

    xg_c, g_c = _gather(slot_r_c, aff_r_c, h2_c, ctx_cap, MOE_CTX_PER_STEP)
    xg_l, g_l = _gather(slot_r_l, aff_r_l, h2_l, lat_cap, 1)

    def gate_rows(g, nreq, cap):
        g = jnp.transpose(g.reshape(nreq, N_EXPERTS, cap), (1, 0, 2))
        return g.reshape(N_EXPERTS, nreq * cap // GATE_ROWS, GATE_ROWS)

    gates = jnp.concatenate([gate_rows(g_c, nctx, ctx_cap), gate_rows(g_l, nlat, lat_cap)], axis=1)
    y_c, y_l = _experts(xg_c, xg_l, gates, w_gate_e[0], w_up_e[0], w_down_e[0])

    gf = final_g.reshape(1, D_MODEL)
    y_prompt = _combine(x1_c, mods, ctx_row, slot_t_c, y_c, gf, ctx_cap, MOE_CTX_PER_STEP)
    y_sample = _combine(x1_l, mods, lat_row, slot_t_l, y_l, gf, lat_cap, 1)

    state_k = k_c.reshape(nctx, 1, ctx_seq, N_HEADS, HEAD_DIM)
    state_v = v_c.reshape(nctx, 1, ctx_seq, N_HEADS, HEAD_DIM)
    return (y_prompt, y_sample, state_k, state_v)
```

```python
import functools

import jax
import jax.numpy as jnp
from jax import lax
from jax.experimental import pallas as pl
from jax.experimental.pallas import tpu as pltpu

F32 = jnp.float32
BF16 = jnp.bfloat16

D_MODEL = 1024
D_CONV = 512
CONV_WIDTH = 31
N_HEADS = 8
HEAD_DIM = 64
D_ATTN = N_HEADS * HEAD_DIM
GRID_W = 64
NA_ROWS = 8
NA_COLS = 16
N_EXPERTS = 16
D_EXPERT = 2048
EC_CAPACITY = 2
N_MOD = 6
EPS = 1e-6
D_PROJ_A = 2 * D_CONV + 3 * D_ATTN
D_IN = D_PROJ_A + 2 * D_MODEL
LANES = 128
REQ_PER_GROUP = LANES // N_EXPERTS
MASK_NEG = -1e30
VMEM_LIMIT = 56 * 1024 * 1024

_NT = (((1,), (1,)), ((), ()))


def _params(*sem):
    return pltpu.CompilerParams(dimension_semantics=sem, vmem_limit_bytes=VMEM_LIMIT)


def _dot(a, b):
    return jnp.dot(a, b, preferred_element_type=F32)


def _sigmoid(x):
    return 1.0 / (1.0 + jnp.exp(-x))


def _split_bf16(x):
    hi = x.astype(BF16)
    lo = (x - hi.astype(F32)).astype(BF16)
    return hi, lo


def _rms_mod(x, g, shift, scale):
    xn = x * lax.rsqrt(jnp.mean(x * x, axis=-1, keepdims=True) + EPS) * g
    return xn * (1.0 + scale) + shift


def _ada_kernel(cond_ref, w_ref, b_ref, o_ref):
    c = cond_ref[...]
    a = c * _sigmoid(c)
    ah, al = _split_bf16(a)
    wh, wl = _split_bf16(w_ref[...])
    o_ref[...] = _dot(ah, wh) + _dot(al, wh) + _dot(ah, wl) + b_ref[...]


def _ada(cond, w_ada, b_ada):
    n = w_ada.shape[1]
    bn = 1024
    return pl.pallas_call(
        _ada_kernel,
        grid=(n // bn,),
        in_specs=[
            pl.BlockSpec(cond.shape, lambda j: (0, 0)),
            pl.BlockSpec((D_MODEL, bn), lambda j: (0, j)),
            pl.BlockSpec((1, bn), lambda j: (0, j)),
        ],
        out_specs=pl.BlockSpec((cond.shape[0], bn), lambda j: (0, j)),
        out_shape=jax.ShapeDtypeStruct((cond.shape[0], n), F32),
        compiler_params=_params("arbitrary"),
        name="ada",
    )(cond, w_ada, b_ada)


CONV_PAD = 16
FRONT_ROWS = 256
CONV_ROWS = 128
SUBLANES = 8
FRONT_CTX_PER_STEP = 4


def _front_kernel(x_ref, mods_ref, g1_ref, w_ref, dww_ref, dwb_ref, lng_ref, lnb_ref,
                  q_ref, k_ref, v_ref, c_ref, *gpad_refs, seq):
    nreq = x_ref.shape[0]
    m = mods_ref[0]
    shift, scale = m[0:1], m[1:2]
    zeros = jnp.zeros((CONV_PAD, D_CONV), F32)
    for gpad_ref in gpad_refs:
        gpad_ref[0:CONV_PAD, :] = zeros
        gpad_ref[CONV_PAD + seq:CONV_PAD + seq + CONV_PAD, :] = zeros

    def proj(r, i):
        rows = slice(i * FRONT_ROWS, (i + 1) * FRONT_ROWS)
        x = x_ref[r, rows, :]
        h = _rms_mod(x, g1_ref[...], shift, scale).astype(BF16)
        u = _dot(h, w_ref[...].astype(BF16))
        glu = u[:, :D_CONV] * _sigmoid(u[:, D_CONV:2 * D_CONV])
        g0 = CONV_PAD + i * FRONT_ROWS
        gpad_refs[r][g0:g0 + FRONT_ROWS, :] = glu
        o = 2 * D_CONV
        q_ref[r, rows, :] = (u[:, o:o + D_ATTN] * (HEAD_DIM ** -0.5)).astype(q_ref.dtype)
        k_ref[r, rows, :] = u[:, o + D_ATTN:o + 2 * D_ATTN].astype(k_ref.dtype)
        v_ref[r, rows, :] = u[:, o + 2 * D_ATTN:o + 3 * D_ATTN].astype(v_ref.dtype)

    def conv(r, j):
        g0 = j * CONV_ROWS
        cols = []
        for cb in range(D_CONV // LANES):
            cs = slice(cb * LANES, (cb + 1) * LANES)
            win = gpad_refs[r][g0:g0 + CONV_ROWS + 2 * CONV_PAD, cs]
            acc = None
            for s in range(SUBLANES):
                nrow = CONV_ROWS + (SUBLANES if s else 0)
                z = None
                for a in range(2 * CONV_PAD // SUBLANES):
                    t = SUBLANES * a + s - (CONV_PAD - CONV_WIDTH // 2)
                    if 0 <= t < CONV_WIDTH:
                        term = win[SUBLANES * a:SUBLANES * a + nrow] * dww_ref[t:t + 1, cs]
                        z = term if z is None else z + term
                z = z[s:s + CONV_ROWS]
                acc = z if acc is None else acc + z
            cols.append(acc)
        y = jnp.concatenate(cols, axis=1) + dwb_ref[...]
        mu = jnp.mean(y, axis=-1, keepdims=True)
        yc = y - mu
        var = jnp.mean(yc * yc, axis=-1, keepdims=True)
        z = yc * lax.rsqrt(var + EPS) * lng_ref[...] + lnb_ref[...]
        c_ref[r, j * CONV_ROWS:(j + 1) * CONV_ROWS, :] = (z * _sigmoid(z)).astype(c_ref.dtype)

    chunks = [(r, i) for r in range(nreq) for i in range(seq // FRONT_ROWS)]
    blocks = [(r, j) for r in range(nreq) for j in range(seq // CONV_ROWS)]

    def ready(block, done):
        r, j = block
        last = min((j + 1) * CONV_ROWS + CONV_WIDTH // 2, seq) - 1
        return (r, last // FRONT_ROWS) in done

    done, pending = set(), list(blocks)
    proj(*chunks[0])
    for n, chunk in enumerate(chunks):
        done.add(chunk)
        if n + 1 < len(chunks):
            proj(*chunks[n + 1])
        now = [b for b in pending if ready(b, done)]
        pending = [b for b in pending if b not in now]
        for b in now:
            conv(*b)


def _front(x, mods, mod_row, g1, w_a, dww, dwb, lng, lnb, kv_dtype, per_step):
    nreq, seq, _ = x.shape
    const2 = lambda b: (0, 0)
    tok = lambda b: (b, 0, 0)
    out_tok = pl.BlockSpec((per_step, seq, D_ATTN), tok)
    return pl.pallas_call(
        functools.partial(_front_kernel, seq=seq),
        grid=(nreq // per_step,),
        in_specs=[
            pl.BlockSpec((per_step, seq, D_MODEL), tok),
            pl.BlockSpec((1, N_MOD, D_MODEL), lambda b: (mod_row(b * per_step), 0, 0)),
            pl.BlockSpec((1, D_MODEL), const2),
            pl.BlockSpec((D_MODEL, D_PROJ_A), const2, pipeline_mode=pl.Buffered(1)),
            pl.BlockSpec((CONV_WIDTH, D_CONV), const2),
            pl.BlockSpec((1, D_CONV), const2),
            pl.BlockSpec((1, D_CONV), const2),
            pl.BlockSpec((1, D_CONV), const2),
        ],
        out_specs=[out_tok, out_tok, out_tok, out_tok],
        out_shape=[
            jax.ShapeDtypeStruct((nreq, seq, D_ATTN), BF16),
            jax.ShapeDtypeStruct((nreq, seq, D_ATTN), kv_dtype),
            jax.ShapeDtypeStruct((nreq, seq, D_ATTN), kv_dtype),
            jax.ShapeDtypeStruct((nreq, seq, D_CONV), BF16),
        ],
        scratch_shapes=[pltpu.VMEM((seq + 2 * CONV_PAD, D_CONV), F32)] * per_step,
        compiler_params=_params("arbitrary"),
        name="front",
    )(x, mods, g1, w_a, dww, dwb, lng, lnb)


def _pair_scores_q(q_pair):
    lane = lax.broadcasted_iota(jnp.int32, q_pair.shape, 1)
    zero = jnp.zeros_like(q_pair)
    return jnp.concatenate([jnp.where(lane < HEAD_DIM, q_pair, zero),
                            jnp.where(lane >= HEAD_DIM, q_pair, zero)], axis=0)


def _pair_merge(o2, t):
    lane = lax.broadcasted_iota(jnp.int32, (t, LANES), 1)
    return jnp.where(lane < HEAD_DIM, o2[:t], o2[t:])


def _ctx_attn_kernel(q_ref, k_ref, v_ref, o_ref):
    seq = q_ref.shape[1]
    pairs = range(N_HEADS // 2)
    sls = [slice(p * LANES, (p + 1) * LANES) for p in pairs]
    s = [lax.dot_general(_pair_scores_q(q_ref[0, :, sls[p]]), k_ref[0, :, sls[p]].astype(BF16), _NT,
                         preferred_element_type=F32) for p in pairs]
    e = [jnp.exp(s[p] - jnp.max(s[p], axis=-1, keepdims=True)) for p in pairs]
    den = [jnp.sum(e[p], axis=-1, keepdims=True) for p in pairs]
    for p in pairs:
        o2 = _dot(e[p].astype(BF16), v_ref[0, :, sls[p]].astype(BF16)) / den[p]
        o_ref[0, :, sls[p]] = _pair_merge(o2, seq).astype(o_ref.dtype)


def _ctx_attn(q, k, v):
    nreq, seq, _ = q.shape
    spec = pl.BlockSpec((1, seq, D_ATTN), lambda b: (b, 0, 0))
    return pl.pallas_call(
        _ctx_attn_kernel,
        grid=(nreq,),
        in_specs=[spec, spec, spec],
        out_specs=spec,
        out_shape=jax.ShapeDtypeStruct((nreq, seq, D_ATTN), BF16),
        compiler_params=_params("arbitrary"),
        name="ctx_attn",
    )(q, k, v)


N_DR = 2 * NA_ROWS - 1
N_DC = 2 * NA_COLS - 1


def _bias_kernel(rpb_ref, o_ref, t2_ref):
    h = pl.program_id(0)
    qi = lax.broadcasted_iota(jnp.int32, (GRID_W, LANES), 0)
    lane = lax.broadcasted_iota(jnp.int32, (GRID_W, LANES), 1)
    kc = lane & (GRID_W - 1)
    d = kc - qi
    cs = jnp.clip(qi - NA_COLS // 2, 0, GRID_W - NA_COLS)
    inside = (kc >= cs) & (kc < cs + NA_COLS)
    group = 5
    for dr0 in range(0, N_DR, group):
        ts = [jnp.full((GRID_W, LANES), MASK_NEG, F32)] * group
        for j in range(N_DC):
            hit = d == j - (NA_COLS - 1)
            ts = [jnp.where(hit, rpb_ref[h * (N_DR * N_DC) + (dr0 + g) * N_DC + j], ts[g])
                  for g in range(group)]
        for g in range(group):
            t2_ref[dr0 + g] = jnp.where(inside, ts[g], MASK_NEG)
    for o in range(NA_ROWS):
        for jj in range(NA_ROWS // 2):
            o_ref[0, o, :, jj * LANES:(jj + 1) * LANES] = jnp.where(
                lane < GRID_W, t2_ref[o + 2 * jj], t2_ref[o + 2 * jj + 1])


def _bias_table(rpb):
    return pl.pallas_call(
        _bias_kernel,
        grid=(N_HEADS,),
        in_specs=[pl.BlockSpec(memory_space=pltpu.SMEM)],
        out_specs=pl.BlockSpec((1, NA_ROWS, GRID_W, NA_ROWS * GRID_W), lambda h: (h, 0, 0, 0)),
        out_shape=jax.ShapeDtypeStruct((N_HEADS, NA_ROWS, GRID_W, NA_ROWS * GRID_W), F32),
        scratch_shapes=[pltpu.VMEM((N_DR, GRID_W, LANES), F32)],
        compiler_params=_params("arbitrary"),
        name="na_bias",
    )(rpb.reshape(-1))


NA_PAIR_GROUP = 4


def _na_attn_kernel(q_ref, k_ref, v_ref, kct_ref, vct_ref, tab_ref, o_ref, kcb_ref, vcb_ref, *, rows):
    r = pl.program_id(1)

    @pl.when(r == 0)
    def _():
        kcb_ref[...] = kct_ref[0].astype(BF16)
        vcb_ref[...] = vct_ref[0].astype(BF16)

    rs = jnp.clip(r - NA_ROWS // 2, 0, rows - NA_ROWS)
    k0 = pl.multiple_of(rs * GRID_W, GRID_W)
    off = rs - r + (NA_ROWS - 1)
    nloc = NA_ROWS * GRID_W
    sls = [slice(p * LANES, (p + 1) * LANES) for p in range(N_HEADS // 2)]

    def group(pairs):
        s_loc, s_ctx = {}, {}
        for p in pairs:
            q2 = _pair_scores_q(q_ref[0, :, sls[p]])
            bias = jnp.concatenate([tab_ref[2 * p, off], tab_ref[2 * p + 1, off]], axis=0)
            kl = k_ref[0, pl.ds(k0, nloc), sls[p]]
            s_loc[p] = lax.dot_general(q2, kl, _NT, preferred_element_type=F32) + bias
            s_ctx[p] = _dot(q2, kcb_ref[sls[p], :])
        mx = {p: jnp.maximum(jnp.max(s_loc[p], axis=-1, keepdims=True),
                             jnp.max(s_ctx[p], axis=-1, keepdims=True)) for p in pairs}
        e_loc = {p: jnp.exp(s_loc[p] - mx[p]) for p in pairs}
        e_ctx = {p: jnp.exp(s_ctx[p] - mx[p]) for p in pairs}
        den = {p: jnp.sum(e_loc[p], axis=-1, keepdims=True) + jnp.sum(e_ctx[p], axis=-1, keepdims=True)
               for p in pairs}
        for p in pairs:
            vl = v_ref[0, pl.ds(k0, nloc), sls[p]]
            o_ctx = lax.dot_general(e_ctx[p].astype(BF16), vcb_ref[sls[p], :], _NT,
                                    preferred_element_type=F32)
            o2 = (_dot(e_loc[p].astype(BF16), vl) + o_ctx) / den[p]
            o_ref[0, :, sls[p]] = _pair_merge(o2, GRID_W).astype(o_ref.dtype)

    for g in range(0, N_HEADS // 2, NA_PAIR_GROUP):
        group(range(g, g + NA_PAIR_GROUP))


def _na_attn(q, k, v, kc, vc, tab):
    nreq, seq, _ = q.shape
    rows = seq // GRID_W
    past = kc.shape[2]
    qspec = pl.BlockSpec((1, GRID_W, D_ATTN), lambda b, r: (b, r, 0))
    kvspec = pl.BlockSpec((1, seq, D_ATTN), lambda b, r: (b, 0, 0))
    cspec = pl.BlockSpec((1, D_ATTN, past), lambda b, r: (b, 0, 0))
    return pl.pallas_call(
        functools.partial(_na_attn_kernel, rows=rows),
        grid=(nreq, rows),
        in_specs=[qspec, kvspec, kvspec, cspec, cspec,
                  pl.BlockSpec(tab.shape, lambda b, r: (0, 0, 0, 0))],
        out_specs=qspec,
        out_shape=jax.ShapeDtypeStruct((nreq, seq, D_ATTN), BF16),
        scratch_shapes=[pltpu.VMEM((D_ATTN, past), BF16), pltpu.VMEM((D_ATTN, past), BF16)],
        compiler_params=_params("arbitrary", "arbitrary"),
        name="na_attn",
    )(q, k, v, kc, vc, tab)


MERGE_ROWS = 512
MERGE_SUB = 256


def _merge_kernel(x_ref, mods_ref, g1_ref, g2_ref, c_ref, a_ref, wg_ref, wpw_ref, wao_ref,
                  wout_ref, wr_ref, x1_ref, h2_ref, aff_ref):
    m = mods_ref[0]
    subs = range(MERGE_ROWS // MERGE_SUB)
    rows = [slice(i * MERGE_SUB, (i + 1) * MERGE_SUB) for i in subs]
    x = [x_ref[0, rows[i], :] for i in subs]
    h = [_rms_mod(x[i], g1_ref[...], m[0:1], m[1:2]).astype(BF16) for i in subs]
    gates = [_sigmoid(_dot(h[i], wg_ref[:, D_PROJ_A - D_IN // 2:].astype(BF16))) for i in subs]
    conv_out = [_dot(c_ref[0, rows[i], :], wpw_ref[...].astype(BF16)) for i in subs]
    attn_out = [_dot(a_ref[0, rows[i], :], wao_ref[...].astype(BF16)) for i in subs]
    merged = [(gates[i][:, :D_MODEL] * conv_out[i] + gates[i][:, D_MODEL:] * attn_out[i]).astype(BF16)
              for i in subs]
    x1 = [x[i] + m[2:3] * _dot(merged[i], wout_ref[...].astype(BF16)) for i in subs]
    h2 = [_rms_mod(x1[i], g2_ref[...], m[3:4], m[4:5]).astype(BF16) for i in subs]
    logits = [_dot(h2[i], wr_ref[...].astype(BF16)) for i in subs]
    e = [jnp.exp(logits[i] - jnp.max(logits[i], axis=-1, keepdims=True)) for i in subs]
    for i in subs:
        x1_ref[0, rows[i], :] = x1[i]
        h2_ref[0, rows[i], :] = h2[i]
        aff_ref[0, rows[i], :] = e[i] / jnp.sum(e[i], axis=-1, keepdims=True)


def _merge(x, mods, mod_row, g1, g2, cact, heads, w_g, w_pw, w_ao, w_out, w_r):
    nreq, seq, _ = x.shape
    tm = MERGE_ROWS
    tok = lambda b, j: (b, j, 0)
    const2 = lambda b, j: (0, 0)
    once = pl.Buffered(1)
    return pl.pallas_call(
        _merge_kernel,
        grid=(nreq, seq // tm),
        in_specs=[
            pl.BlockSpec((1, tm, D_MODEL), tok),
            pl.BlockSpec((1, N_MOD, D_MODEL), lambda b, j: (mod_row(b), 0, 0)),
            pl.BlockSpec((1, D_MODEL), const2),
            pl.BlockSpec((1, D_MODEL), const2),
            pl.BlockSpec((1, tm, D_CONV), tok),
            pl.BlockSpec((1, tm, D_ATTN), tok),
            pl.BlockSpec((D_MODEL, D_IN // 2), lambda b, j: (0, 1), pipeline_mode=once),
            pl.BlockSpec(w_pw.shape, const2, pipeline_mode=once),
            pl.BlockSpec(w_ao.shape, const2, pipeline_mode=once),
            pl.BlockSpec(w_out.shape, const2, pipeline_mode=once),
            pl.BlockSpec(w_r.shape, const2, pipeline_mode=once),
        ],
        out_specs=[
            pl.BlockSpec((1, tm, D_MODEL), tok),
            pl.BlockSpec((1, tm, D_MODEL), tok),
            pl.BlockSpec((1, tm, N_EXPERTS), tok),
        ],
        out_shape=[
            jax.ShapeDtypeStruct((nreq, seq, D_MODEL), F32),
            jax.ShapeDtypeStruct((nreq, seq, D_MODEL), BF16),
            jax.ShapeDtypeStruct((nreq, seq, N_EXPERTS), F32),
        ],
        compiler_params=_params("arbitrary", "arbitrary"),
        name="merge",
    )(x, mods, g1, g2, cact, heads, w_g, w_pw, w_ao, w_out, w_r)


def _route_kernel(aff_ref, slot_t_ref, slot_r_ref, aff_r_ref, *, cap):
    a = aff_ref[0]
    seq = a.shape[0]
    one, zero = jnp.ones_like(a), jnp.zeros_like(a)

    def search(i, thr):
        cand = thr | jnp.left_shift(jnp.int32(1), 30 - i)
        cnt = jnp.sum(jnp.where(a >= pltpu.bitcast(cand, F32), one, zero), axis=0, keepdims=True)
        return jnp.where(cnt >= cap, cand, thr)

    thr = lax.fori_loop(0, 31, search, jnp.zeros((1, LANES), jnp.int32))
    gt = a >= pltpu.bitcast(thr + 1, F32)
    eq = (a >= pltpu.bitcast(thr, F32)) & jnp.logical_not(gt)
    need = cap - jnp.sum(jnp.where(gt, one, zero), axis=0, keepdims=True)
    ti = lax.broadcasted_iota(jnp.int32, (seq, seq), 0)
    tj = lax.broadcasted_iota(jnp.int32, (seq, seq), 1)
    before = jnp.where(tj < ti, 1.0, 0.0).astype(BF16)
    eq_rank = _dot(before, jnp.where(eq, one, zero).astype(BF16))
    sel = gt | (eq & (eq_rank < need))
    pos = _dot(before, jnp.where(sel, one, zero).astype(BF16))
    slot = jnp.where(sel, pos, -1.0)
    slot_t_ref[0] = slot
    slot_r_ref[0] = slot.T
    aff_r_ref[0] = a.T


def _route(aff_t, cap):
    ngroup, seq, _ = aff_t.shape
    tspec = pl.BlockSpec((1, seq, LANES), lambda g: (g, 0, 0))
    rspec = pl.BlockSpec((1, LANES, seq), lambda g: (g, 0, 0))
    return pl.pallas_call(
        functools.partial(_route_kernel, cap=cap),
        grid=(ngroup,),
        in_specs=[tspec],
        out_specs=[tspec, rspec, rspec],
        out_shape=[
            jax.ShapeDtypeStruct((ngroup, seq, LANES), F32),
            jax.ShapeDtypeStruct((ngroup, LANES, seq), F32),
            jax.ShapeDtypeStruct((ngroup, LANES, seq), F32),
        ],
        compiler_params=_params("arbitrary"),
        name="route",
    )(aff_t)


GATHER_ROWS = 512


def _gather_kernel(slot_ref, aff_ref, h_ref, x_ref, g_ref, *p_refs, cap):
    per_step, seq, _ = h_ref.shape
    nslot = N_EXPERTS * cap
    sub = lax.broadcasted_iota(jnp.int32, (cap, seq), 0).astype(F32)
    one, zero = jnp.ones((cap, seq), F32), jnp.zeros((cap, seq), F32)
    ei = lax.broadcasted_iota(jnp.int32, (N_EXPERTS, nslot), 0)
    si = lax.broadcasted_iota(jnp.int32, (N_EXPERTS, nslot), 1)
    own = (si >= ei * cap) & (si < (ei + 1) * cap)
    per_chunk = GATHER_ROWS // cap
    for r, p_ref in enumerate(p_refs):
        for e in range(N_EXPERTS):
            row = jnp.broadcast_to(slot_ref[0, r * N_EXPERTS + e:r * N_EXPERTS + e + 1, :], (cap, seq))
            p_ref[e * cap:(e + 1) * cap, :] = jnp.where(row == sub, one, zero).astype(BF16)
        h = h_ref[r]
        for c in range(nslot // GATHER_ROWS):
            rows = slice(c * GATHER_ROWS, (c + 1) * GATHER_ROWS)
            x = _dot(p_ref[rows, :], h).astype(x_ref.dtype)
            for i in range(per_chunk):
                x_ref[c * per_chunk + i, r * cap:(r + 1) * cap, :] = x[i * cap:(i + 1) * cap]
        a = aff_ref[0, r * N_EXPERTS:(r + 1) * N_EXPERTS, :]
        a_hi = a.astype(BF16)
        r1 = a - a_hi.astype(F32)
        a_mid = r1.astype(BF16)
        a_lo = (r1 - a_mid.astype(F32)).astype(BF16)
        a3 = jnp.concatenate([a_hi, a_mid, a_lo], axis=0)
        g3 = lax.dot_general(a3, p_ref[...], _NT, preferred_element_type=F32)
        gfull = g3[0:N_EXPERTS] + g3[N_EXPERTS:2 * N_EXPERTS] + g3[2 * N_EXPERTS:3 * N_EXPERTS]
        g_ref[r] = jnp.sum(jnp.where(own, gfull, 0.0), axis=0, keepdims=True)


def _gather(slot_r, aff_r, h2, cap, per_step):
    nreq, seq, _ = h2.shape
    nslot = N_EXPERTS * cap
    steps_per_group = REQ_PER_GROUP // per_step
    grp = lambda s: (s // steps_per_group, s % steps_per_group, 0)
    x, g = pl.pallas_call(
        functools.partial(_gather_kernel, cap=cap),
        grid=(nreq // per_step,),
        in_specs=[
            pl.BlockSpec((1, per_step * N_EXPERTS, seq), grp),
            pl.BlockSpec((1, per_step * N_EXPERTS, seq), grp),
            pl.BlockSpec((per_step, seq, D_MODEL), lambda s: (s, 0, 0)),
        ],
        out_specs=[
            pl.BlockSpec((N_EXPERTS, per_step * cap, D_MODEL), lambda s: (0, s, 0)),
            pl.BlockSpec((per_step, 1, nslot), lambda s: (s, 0, 0)),
        ],
        out_shape=[
            jax.ShapeDtypeStruct((N_EXPERTS, nreq * cap, D_MODEL), BF16),
            jax.ShapeDtypeStruct((nreq, 1, nslot), F32),
        ],
        scratch_shapes=[pltpu.VMEM((nslot, seq), BF16)] * per_step,
        compiler_params=_params("arbitrary"),
        name="gather",
    )(slot_r, aff_r, h2)
    return x, g


EXPERT_FCHUNK = 512
EXPERT_ROWS = 512
GATE_ROWS = 256


def _expert_kernel(xc_ref, xl_ref, g_ref, wg_ref, wu_ref, wd_ref, yc_ref, yl_ref,
                   acc_ref):
    e = pl.program_id(0)
    f = pl.program_id(1)
    nf = pl.num_programs(1)
    half = xc_ref.shape[1]

    @pl.when((e == 0) & (f == 0))
    def _():
        acc_ref[...] = jnp.zeros_like(acc_ref)

    ri = lax.broadcasted_iota(jnp.int32, (GATE_ROWS, GATE_ROWS), 0)
    ci = lax.broadcasted_iota(jnp.int32, (GATE_ROWS, GATE_ROWS), 1)
    for part, (x_ref, y_ref) in enumerate(((xc_ref, yc_ref), (xl_ref, yl_ref))):
        for i in range(half // EXPERT_ROWS):
            x = x_ref[0, i * EXPERT_ROWS:(i + 1) * EXPERT_ROWS, :]
            gq = _dot(x, wg_ref[0].astype(BF16))
            uq = _dot(x, wu_ref[0].astype(BF16))
            hid = (gq * _sigmoid(gq) * uq).astype(BF16)
            r0 = part * half + i * EXPERT_ROWS
            prev = jnp.where(f == 0, 0.0, acc_ref[r0:r0 + EXPERT_ROWS, :])
            acc = prev + _dot(hid, wd_ref[0].astype(BF16))
            acc_ref[r0:r0 + EXPERT_ROWS, :] = acc
            for j in range(EXPERT_ROWS // GATE_ROWS):
                t = r0 // GATE_ROWS + j
                grow = jnp.broadcast_to(g_ref[0, t:t + 1, :], (GATE_ROWS, GATE_ROWS))
                gcol = jnp.sum(jnp.where(ri == ci, grow, 0.0), axis=-1, keepdims=True)
                y = acc[j * GATE_ROWS:(j + 1) * GATE_ROWS] * gcol
                y_ref[0, i * EXPERT_ROWS + j * GATE_ROWS:i * EXPERT_ROWS + (j + 1) * GATE_ROWS, :] = (
                    y.astype(y_ref.dtype))


def _experts(xc, xl, g, w_gate, w_up, w_down):
    ne, half, _ = xc.shape
    fc = EXPERT_FCHUNK
    xspec = pl.BlockSpec((1, half, D_MODEL), lambda e, f: (e, 0, 0))
    return pl.pallas_call(
        _expert_kernel,
        grid=(ne, D_EXPERT // fc),
        in_specs=[
            xspec, xspec,
            pl.BlockSpec((1,) + g.shape[1:], lambda e, f: (e, 0, 0)),
            pl.BlockSpec((1, D_MODEL, fc), lambda e, f: (e, 0, f)),
            pl.BlockSpec((1, D_MODEL, fc), lambda e, f: (e, 0, f)),
            pl.BlockSpec((1, fc, D_MODEL), lambda e, f: (e, f, 0)),
        ],
        out_specs=[xspec, xspec],
        out_shape=[jax.ShapeDtypeStruct(xc.shape, BF16), jax.ShapeDtypeStruct(xl.shape, BF16)],
        scratch_shapes=[pltpu.VMEM((2 * half, D_MODEL), F32)],
        compiler_params=_params("arbitrary", "arbitrary"),
        name="experts",
    )(xc, xl, g, w_gate, w_up, w_down)


COMBINE_ROWS = 256
MOE_CTX_PER_STEP = 4


def _combine_kernel(x1_ref, mods_ref, slot_ref, y_ref, gf_ref, o_ref, *, cap):
    per_step = x1_ref.shape[0]
    nslot = N_EXPERTS * cap
    li = lax.broadcasted_iota(jnp.int32, (LANES, nslot), 0)
    si = lax.broadcasted_iota(jnp.int32, (LANES, nslot), 1)
    for r in range(per_step):
        b = pl.program_id(0) * per_step + r
        lane0 = (b % REQ_PER_GROUP) * N_EXPERTS
        expand = jnp.where(li == lane0 + si // cap, 1.0, 0.0).astype(BF16)
        slot_exp = _dot(slot_ref[0].astype(BF16), expand)
        want = (lax.broadcasted_iota(jnp.int32, slot_exp.shape, 1) & (cap - 1)).astype(F32)
        scat = jnp.where(slot_exp == want, 1.0, 0.0).astype(BF16)
        moe = _dot(scat, y_ref[:, r * cap:(r + 1) * cap, :].reshape(nslot, D_MODEL))
        x2 = x1_ref[r] + mods_ref[0][5:6] * moe
        o_ref[r] = x2 * lax.rsqrt(jnp.mean(x2 * x2, axis=-1, keepdims=True) + EPS) * gf_ref[...]


def _combine(x1, mods, mod_row, slot_t, y, gf, cap, per_step):
    nreq, seq, _ = x1.shape
    tm = COMBINE_ROWS
    assert per_step == 1 or seq == tm
    tok = lambda s, j: (s, j, 0)
    return pl.pallas_call(
        functools.partial(_combine_kernel, cap=cap),
        grid=(nreq // per_step, seq // tm),
        in_specs=[
            pl.BlockSpec((per_step, tm, D_MODEL), tok),
            pl.BlockSpec((1, N_MOD, D_MODEL), lambda s, j: (mod_row(s * per_step), 0, 0)),
            pl.BlockSpec((1, tm, LANES), lambda s, j: (s * per_step // REQ_PER_GROUP, j, 0)),
            pl.BlockSpec((N_EXPERTS, per_step * cap, D_MODEL), lambda s, j: (0, s, 0)),
            pl.BlockSpec((1, D_MODEL), lambda s, j: (0, 0)),
        ],
        out_specs=pl.BlockSpec((per_step, tm, D_MODEL), tok),
        out_shape=jax.ShapeDtypeStruct(x1.shape, F32),
        compiler_params=_params("arbitrary", "arbitrary"),
        name="combine",
    )(x1, mods, slot_t, y, gf)


def _group_lanes(aff):
    nreq, seq, ne = aff.shape
    a = aff.reshape(nreq // REQ_PER_GROUP, REQ_PER_GROUP, seq, ne)
    return jnp.transpose(a, (0, 2, 1, 3)).reshape(nreq // REQ_PER_GROUP, seq, REQ_PER_GROUP * ne)


def kernel(x_prompt, x_sample, cache_ctx_k, cache_ctx_v, c, c_ctx, w_ada, b_ada, norm1_g, w_in,
           conv_dw_w, conv_dw_b, conv_ln_g, conv_ln_b, w_conv_pw, na_rpb, w_attn_o, w_out,
           norm2_g, w_router, w_gate_e, w_up_e, w_down_e, final_g):
    assert w_ada.shape[0] == 1, "single trunk layer"
    nctx, ctx_seq, _ = x_prompt.shape
    nlat, lat_seq, _ = x_sample.shape
    ctx_cap = EC_CAPACITY * ctx_seq // N_EXPERTS
    lat_cap = EC_CAPACITY * lat_seq // N_EXPERTS
    assert nctx * ctx_cap == nlat * lat_cap

    cond = jnp.zeros((2 * nlat, D_MODEL), F32).at[:nlat].set(c).at[nlat].set(c_ctx)
    mods = _ada(cond, w_ada[0], b_ada).reshape(2 * nlat, N_MOD, D_MODEL)
    ctx_row = lambda b: nlat
    lat_row = lambda b: b

    w_a = w_g = w_in[0]
    w_pw, w_ao, w_o, w_r = w_conv_pw[0], w_attn_o[0], w_out[0], w_router[0]
    conv_args = (conv_dw_w[0], conv_dw_b, conv_ln_g, conv_ln_b)

    q_c, k_c, v_c, cact_c = _front(x_prompt, mods, ctx_row, norm1_g, w_a, *conv_args, F32,
                                   FRONT_CTX_PER_STEP)
    q_l, k_l, v_l, cact_l = _front(x_sample, mods, lat_row, norm1_g, w_a, *conv_args, BF16, 1)

    heads_c = _ctx_attn(q_c, k_c, v_c)
    tab = _bias_table(na_rpb[0])
    past = cache_ctx_k.shape[2]
    feat_major = lambda t: jnp.transpose(t[:, 0], (0, 2, 3, 1)).reshape(nlat, D_ATTN, past)
    heads_l = _na_attn(q_l, k_l, v_l, feat_major(cache_ctx_k), feat_major(cache_ctx_v), tab)

    merge_w = (w_g, w_pw, w_ao, w_o, w_r)
    flat = lambda t: t.reshape(1, nctx * ctx_seq, t.shape[-1])
    x1_c, h2_c, aff_c = _merge(flat(x_prompt), mods, ctx_row, norm1_g, norm2_g, flat(cact_c),
                               flat(heads_c), *merge_w)
    x1_c, h2_c, aff_c = (t.reshape(nctx, ctx_seq, t.shape[-1]) for t in (x1_c, h2_c, aff_c))
    x1_l, h2_l, aff_l = _merge(x_sample, mods, lat_row, norm1_g, norm2_g, cact_l, heads_l, *merge_w)

    slot_t_c, slot_r_c, aff_r_c = _route(_group_lanes(aff_c), ctx_cap)
    slot_t_l, slot_r_l, aff_r_l = _route(_group_lanes(aff_l), lat_cap)

    xg_c, g_c = _gather(slot_r_c, aff_r_c, h2_c, ctx_cap, MOE_CTX_PER_STEP)
    xg_l, g_l = _gather(slot_r_l, aff_r_l, h2_l, lat_cap, 1)

    def gate_rows(g, nreq, cap):
        g = jnp.transpose(g.reshape(nreq, N_EXPERTS, cap), (1, 0, 2))
        return g.reshape(N_EXPERTS, nreq * cap // GATE_ROWS, GATE_ROWS)

    gates = jnp.concatenate([gate_rows(g_c, nctx, ctx_cap), gate_rows(g_l, nlat, lat_cap)], axis=1)
    y_c, y_l = _experts(xg_c, xg_l, gates, w_gate_e[0], w_up_e[0], w_down_e[0])

    gf = final_g.reshape(1, D_MODEL)
    y_prompt = _combine(x1_c, mods, ctx_row, slot_t_c, y_c, gf, ctx_cap, MOE_CTX_PER_STEP)
    y_sample = _combine(x1_l, mods, lat_row, slot_t_l, y_l, gf, lat_cap, 1)

    state_k = k_c.reshape(nctx, 1, ctx_seq, N_HEADS, HEAD_DIM)
    state_v = v_c.reshape(nctx, 1, ctx_seq, N_HEADS, HEAD_DIM)
    return (y_prompt, y_sample, state_k, state_v)
```

```python
import functools

import jax
import jax.numpy as jnp
from jax import lax
from jax.experimental import pallas as pl
from jax.experimental.pallas import tpu as pltpu

F32 = jnp.float32
BF16 = jnp.bfloat16

D_MODEL = 1024
D_CONV = 512
CONV_WIDTH = 31
N_HEADS = 8
HEAD_DIM = 64
D_ATTN = N_HEADS * HEAD_DIM
GRID_W = 64
NA_ROWS = 8
NA_COLS = 16
N_EXPERTS = 16
D_EXPERT = 2048
EC_CAPACITY = 2
N_MOD = 6
EPS = 1e-6
D_PROJ_A = 2 * D_CONV + 3 * D_ATTN
D_IN = D_PROJ_A + 2 * D_MODEL
LANES = 128
REQ_PER_GROUP = LANES // N_EXPERTS
MASK_NEG = -1e30
VMEM_LIMIT = 56 * 1024 * 1024

_NT = (((1,), (1,)), ((), ()))


def _params(*sem):
    return pltpu.CompilerParams(dimension_semantics=sem, vmem_limit_bytes=VMEM_LIMIT)


def _dot(a, b):
    return jnp.dot(a, b, preferred_element_type=F32)


def _sigmoid(x):
    return 1.0 / (1.0 + jnp.exp(-x))


def _split_bf16(x):
    hi = x.astype(BF16)
    lo = (x - hi.astype(F32)).astype(BF16)
    return hi, lo


def _rms_mod(x, g, shift, scale):
    xn = x * lax.rsqrt(jnp.mean(x * x, axis=-1, keepdims=True) + EPS) * g
    return xn * (1.0 + scale) + shift


def _ada_kernel(cond_ref, w_ref, b_ref, o_ref):
    c = cond_ref[...]
    a = c * _sigmoid(c)
    ah, al = _split_bf16(a)
    wh, wl = _split_bf16(w_ref[...])
    o_ref[...] = _dot(ah, wh) + _dot(al, wh) + _dot(ah, wl) + b_ref[...]


def _ada(cond, w_ada, b_ada):
    n = w_ada.shape[1]
    bn = 1024
    return pl.pallas_call(
        _ada_kernel,
        grid=(n // bn,),
        in_specs=[
            pl.BlockSpec(cond.shape, lambda j: (0, 0)),
            pl.BlockSpec((D_MODEL, bn), lambda j: (0, j)),
            pl.BlockSpec((1, bn), lambda j: (0, j)),
        ],
        out_specs=pl.BlockSpec((cond.shape[0], bn), lambda j: (0, j)),
        out_shape=jax.ShapeDtypeStruct((cond.shape[0], n), F32),
        compiler_params=_params("arbitrary"),
        name="ada",
    )(cond, w_ada, b_ada)


CONV_PAD = 16
FRONT_ROWS = 256
CONV_ROWS = 128
SUBLANES = 8
FRONT_CTX_PER_STEP = 4


def _front_kernel(x_ref, mods_ref, g1_ref, w_ref, dww_ref, dwb_ref, lng_ref, lnb_ref,
                  q_ref, k_ref, v_ref, c_ref, *gpad_refs, seq):
    nreq = x_ref.shape[0]
    m = mods_ref[0]
    shift, scale = m[0:1], m[1:2]
    zeros = jnp.zeros((CONV_PAD, D_CONV), F32)
    for gpad_ref in gpad_refs:
        gpad_ref[0:CONV_PAD, :] = zeros
        gpad_ref[CONV_PAD + seq:CONV_PAD + seq + CONV_PAD, :] = zeros

    def proj(r, i):
        rows = slice(i * FRONT_ROWS, (i + 1) * FRONT_ROWS)
        x = x_ref[r, rows, :]
        h = _rms_mod(x, g1_ref[...], shift, scale).astype(BF16)
        u = _dot(h, w_ref[...])
        glu = u[:, :D_CONV] * _sigmoid(u[:, D_CONV:2 * D_CONV])
        g0 = CONV_PAD + i * FRONT_ROWS
        gpad_refs[r][g0:g0 + FRONT_ROWS, :] = glu
        o = 2 * D_CONV
        q_ref[r, rows, :] = (u[:, o:o + D_ATTN] * (HEAD_DIM ** -0.5)).astype(q_ref.dtype)
        k_ref[r, rows, :] = u[:, o + D_ATTN:o + 2 * D_ATTN].astype(k_ref.dtype)
        v_ref[r, rows, :] = u[:, o + 2 * D_ATTN:o + 3 * D_ATTN].astype(v_ref.dtype)

    def conv(r, j):
        g0 = j * CONV_ROWS
        cols = []
        for cb in range(D_CONV // LANES):
            cs = slice(cb * LANES, (cb + 1) * LANES)
            win = gpad_refs[r][g0:g0 + CONV_ROWS + 2 * CONV_PAD, cs]
            acc = None
            for s in range(SUBLANES):
                nrow = CONV_ROWS + (SUBLANES if s else 0)
                z = None
                for a in range(2 * CONV_PAD // SUBLANES):
                    t = SUBLANES * a + s - (CONV_PAD - CONV_WIDTH // 2)
                    if 0 <= t < CONV_WIDTH:
                        term = win[SUBLANES * a:SUBLANES * a + nrow] * dww_ref[t:t + 1, cs]
                        z = term if z is None else z + term
                z = z[s:s + CONV_ROWS]
                acc = z if acc is None else acc + z
            cols.append(acc)
        y = jnp.concatenate(cols, axis=1) + dwb_ref[...]
        mu = jnp.mean(y, axis=-1, keepdims=True)
        yc = y - mu
        var = jnp.mean(yc * yc, axis=-1, keepdims=True)
        z = yc * lax.rsqrt(var + EPS) * lng_ref[...] + lnb_ref[...]
        c_ref[r, j * CONV_ROWS:(j + 1) * CONV_ROWS, :] = (z * _sigmoid(z)).astype(c_ref.dtype)

    chunks = [(r, i) for r in range(nreq) for i in range(seq // FRONT_ROWS)]
    blocks = [(r, j) for r in range(nreq) for j in range(seq // CONV_ROWS)]

    def ready(block, done):
        r, j = block
        last = min((j + 1) * CONV_ROWS + CONV_WIDTH // 2, seq) - 1
        return (r, last // FRONT_ROWS) in done

    done, pending = set(), list(blocks)
    proj(*chunks[0])
    for n, chunk in enumerate(chunks):
        done.add(chunk)
        if n + 1 < len(chunks):
            proj(*chunks[n + 1])
        now = [b for b in pending if ready(b, done)]
        pending = [b for b in pending if b not in now]
        for b in now:
            conv(*b)


def _front(x, mods, mod_row, g1, w_a, dww, dwb, lng, lnb, kv_dtype, per_step):
    nreq, seq, _ = x.shape
    const2 = lambda b: (0, 0)
    tok = lambda b: (b, 0, 0)
    out_tok = pl.BlockSpec((per_step, seq, D_ATTN), tok)
    return pl.pallas_call(
        functools.partial(_front_kernel, seq=seq),
        grid=(nreq // per_step,),
        in_specs=[
            pl.BlockSpec((per_step, seq, D_MODEL), tok),
            pl.BlockSpec((1, N_MOD, D_MODEL), lambda b: (mod_row(b * per_step), 0, 0)),
            pl.BlockSpec((1, D_MODEL), const2),
            pl.BlockSpec((D_MODEL, D_PROJ_A), const2),
            pl.BlockSpec((CONV_WIDTH, D_CONV), const2),
            pl.BlockSpec((1, D_CONV), const2),
            pl.BlockSpec((1, D_CONV), const2),
            pl.BlockSpec((1, D_CONV), const2),
        ],
        out_specs=[out_tok, out_tok, out_tok, out_tok],
        out_shape=[
            jax.ShapeDtypeStruct((nreq, seq, D_ATTN), BF16),
            jax.ShapeDtypeStruct((nreq, seq, D_ATTN), kv_dtype),
            jax.ShapeDtypeStruct((nreq, seq, D_ATTN), kv_dtype),
            jax.ShapeDtypeStruct((nreq, seq, D_CONV), BF16),
        ],
        scratch_shapes=[pltpu.VMEM((seq + 2 * CONV_PAD, D_CONV), F32)] * per_step,
        compiler_params=_params("arbitrary"),
        name="front",
    )(x, mods, g1, w_a, dww, dwb, lng, lnb)


def _pair_scores_q(q_pair):
    lane = lax.broadcasted_iota(jnp.int32, q_pair.shape, 1)
    zero = jnp.zeros_like(q_pair)
    return jnp.concatenate([jnp.where(lane < HEAD_DIM, q_pair, zero),
                            jnp.where(lane >= HEAD_DIM, q_pair, zero)], axis=0)


def _pair_merge(o2, t):
    lane = lax.broadcasted_iota(jnp.int32, (t, LANES), 1)
    return jnp.where(lane < HEAD_DIM, o2[:t], o2[t:])


CTX_ATTN_PER_STEP = 2


def _ctx_attn_kernel(q_ref, k_ref, v_ref, o_ref):
    nreq, seq, _ = q_ref.shape
    sls = [slice(p * LANES, (p + 1) * LANES) for p in range(N_HEADS // 2)]
    chains = [(r, p) for r in range(nreq) for p in range(N_HEADS // 2)]
    s = {(r, p): lax.dot_general(_pair_scores_q(q_ref[r, :, sls[p]]), k_ref[r, :, sls[p]].astype(BF16),
                                 _NT, preferred_element_type=F32) for r, p in chains}
    e = {c: jnp.exp(s[c] - jnp.max(s[c], axis=-1, keepdims=True)) for c in chains}
    den = {c: jnp.sum(e[c], axis=-1, keepdims=True) for c in chains}
    for r, p in chains:
        o2 = _dot(e[r, p].astype(BF16), v_ref[r, :, sls[p]].astype(BF16)) / den[r, p]
        o_ref[r, :, sls[p]] = _pair_merge(o2, seq).astype(o_ref.dtype)


def _ctx_attn(q, k, v):
    nreq, seq, _ = q.shape
    spec = pl.BlockSpec((CTX_ATTN_PER_STEP, seq, D_ATTN), lambda b: (b, 0, 0))
    return pl.pallas_call(
        _ctx_attn_kernel,
        grid=(nreq // CTX_ATTN_PER_STEP,),
        in_specs=[spec, spec, spec],
        out_specs=spec,
        out_shape=jax.ShapeDtypeStruct((nreq, seq, D_ATTN), BF16),
        compiler_params=_params("arbitrary"),
        name="ctx_attn",
    )(q, k, v)


N_DR = 2 * NA_ROWS - 1
N_DC = 2 * NA_COLS - 1


def _bias_kernel(rpb_ref, o_ref, t2_ref):
    h = pl.program_id(0)
    qi = lax.broadcasted_iota(jnp.int32, (GRID_W, LANES), 0)
    lane = lax.broadcasted_iota(jnp.int32, (GRID_W, LANES), 1)
    kc = lane & (GRID_W - 1)
    d = kc - qi
    cs = jnp.clip(qi - NA_COLS // 2, 0, GRID_W - NA_COLS)
    inside = (kc >= cs) & (kc < cs + NA_COLS)
    group = 5
    for dr0 in range(0, N_DR, group):
        ts = [jnp.full((GRID_W, LANES), MASK_NEG, F32)] * group
        for j in range(N_DC):
            hit = d == j - (NA_COLS - 1)
            ts = [jnp.where(hit, rpb_ref[h * (N_DR * N_DC) + (dr0 + g) * N_DC + j], ts[g])
                  for g in range(group)]
        for g in range(group):
            t2_ref[dr0 + g] = jnp.where(inside, ts[g], MASK_NEG)
    for o in range(NA_ROWS):
        for jj in range(NA_ROWS // 2):
            o_ref[0, o, :, jj * LANES:(jj + 1) * LANES] = jnp.where(
                lane < GRID_W, t2_ref[o + 2 * jj], t2_ref[o + 2 * jj + 1])


def _bias_table(rpb):
    return pl.pallas_call(
        _bias_kernel,
        grid=(N_HEADS,),
        in_specs=[pl.BlockSpec(memory_space=pltpu.SMEM)],
        out_specs=pl.BlockSpec((1, NA_ROWS, GRID_W, NA_ROWS * GRID_W), lambda h: (h, 0, 0, 0)),
        out_shape=jax.ShapeDtypeStruct((N_HEADS, NA_ROWS, GRID_W, NA_ROWS * GRID_W), F32),
        scratch_shapes=[pltpu.VMEM((N_DR, GRID_W, LANES), F32)],
        compiler_params=_params("arbitrary"),
        name="na_bias",
    )(rpb.reshape(-1))


NA_STEP_ROWS = 4


def _na_attn_kernel(q_ref, k_ref, v_ref, kct_ref, vct_ref, tab_ref, o_ref, kcb_ref, vcb_ref, *, rows):
    r = pl.program_id(1)

    @pl.when(r == 0)
    def _():
        kcb_ref[...] = kct_ref[0].astype(BF16)
        vcb_ref[...] = vct_ref[0].astype(BF16)

    nloc = NA_ROWS * GRID_W
    sls = [slice(p * LANES, (p + 1) * LANES) for p in range(N_HEADS // 2)]
    k0, off = [], []
    for i in range(NA_STEP_ROWS):
        qrow = r * NA_STEP_ROWS + i
        rs = jnp.clip(qrow - NA_ROWS // 2, 0, rows - NA_ROWS)
        k0.append(pl.multiple_of(rs * GRID_W, GRID_W))
        off.append(rs - qrow + (NA_ROWS - 1))
    chains = [(i, p) for i in range(NA_STEP_ROWS) for p in range(N_HEADS // 2)]
    qrows = [slice(i * GRID_W, (i + 1) * GRID_W) for i in range(NA_STEP_ROWS)]
    s_loc, s_ctx = {}, {}
    for i, p in chains:
        q2 = _pair_scores_q(q_ref[0, qrows[i], sls[p]])
        bias = jnp.concatenate([tab_ref[2 * p, off[i]], tab_ref[2 * p + 1, off[i]]], axis=0)
        kl = k_ref[0, pl.ds(k0[i], nloc), sls[p]]
        s_loc[i, p] = lax.dot_general(q2, kl, _NT, preferred_element_type=F32) + bias
        s_ctx[i, p] = _dot(q2, kcb_ref[sls[p], :])
    mx = {c: jnp.maximum(jnp.max(s_loc[c], axis=-1, keepdims=True),
                         jnp.max(s_ctx[c], axis=-1, keepdims=True)) for c in chains}
    e_loc = {c: jnp.exp(s_loc[c] - mx[c]) for c in chains}
    e_ctx = {c: jnp.exp(s_ctx[c] - mx[c]) for c in chains}
    den = {c: jnp.sum(e_loc[c], axis=-1, keepdims=True) + jnp.sum(e_ctx[c], axis=-1, keepdims=True)
           for c in chains}
    for i, p in chains:
        vl = v_ref[0, pl.ds(k0[i], nloc), sls[p]]
        o_ctx = lax.dot_general(e_ctx[i, p].astype(BF16), vcb_ref[sls[p], :], _NT,
                                preferred_element_type=F32)
        o2 = (_dot(e_loc[i, p].astype(BF16), vl) + o_ctx) / den[i, p]
        o_ref[0, qrows[i], sls[p]] = _pair_merge(o2, GRID_W).astype(o_ref.dtype)


def _na_attn(q, k, v, kc, vc, tab):
    nreq, seq, _ = q.shape
    rows = seq // GRID_W
    past = kc.shape[2]
    qspec = pl.BlockSpec((1, NA_STEP_ROWS * GRID_W, D_ATTN), lambda b, r: (b, r, 0))
    kvspec = pl.BlockSpec((1, seq, D_ATTN), lambda b, r: (b, 0, 0))
    cspec = pl.BlockSpec((1, D_ATTN, past), lambda b, r: (b, 0, 0))
    return pl.pallas_call(
        functools.partial(_na_attn_kernel, rows=rows),
        grid=(nreq, rows // NA_STEP_ROWS),
        in_specs=[qspec, kvspec, kvspec, cspec, cspec,
                  pl.BlockSpec(tab.shape, lambda b, r: (0, 0, 0, 0))],
        out_specs=qspec,
        out_shape=jax.ShapeDtypeStruct((nreq, seq, D_ATTN), BF16),
        scratch_shapes=[pltpu.VMEM((D_ATTN, past), BF16), pltpu.VMEM((D_ATTN, past), BF16)],
        compiler_params=_params("arbitrary", "arbitrary"),
        name="na_attn",
    )(q, k, v, kc, vc, tab)


MERGE_ROWS = 512
MERGE_SUB = 256


def _merge_kernel(x_ref, mods_ref, g1_ref, g2_ref, c_ref, a_ref, wg_ref, wpw_ref, wao_ref,
                  wout_ref, wr_ref, x1_ref, h2_ref, aff_ref):
    m = mods_ref[0]
    subs = range(MERGE_ROWS // MERGE_SUB)
    rows = [slice(i * MERGE_SUB, (i + 1) * MERGE_SUB) for i in subs]
    x = [x_ref[0, rows[i], :] for i in subs]
    h = [_rms_mod(x[i], g1_ref[...], m[0:1], m[1:2]).astype(BF16) for i in subs]
    gates = [_sigmoid(_dot(h[i], wg_ref[:, D_PROJ_A - D_IN // 2:])) for i in subs]
    conv_out = [_dot(c_ref[0, rows[i], :], wpw_ref[...]) for i in subs]
    attn_out = [_dot(a_ref[0, rows[i], :], wao_ref[...]) for i in subs]
    merged = [(gates[i][:, :D_MODEL] * conv_out[i] + gates[i][:, D_MODEL:] * attn_out[i]).astype(BF16)
              for i in subs]
    x1 = [x[i] + m[2:3] * _dot(merged[i], wout_ref[...]) for i in subs]
    h2 = [_rms_mod(x1[i], g2_ref[...], m[3:4], m[4:5]).astype(BF16) for i in subs]
    logits = [_dot(h2[i], wr_ref[...]) for i in subs]
    e = [jnp.exp(logits[i] - jnp.max(logits[i], axis=-1, keepdims=True)) for i in subs]
    for i in subs:
        x1_ref[0, rows[i], :] = x1[i]
        h2_ref[0, rows[i], :] = h2[i]
        aff_ref[0, rows[i], :] = e[i] / jnp.sum(e[i], axis=-1, keepdims=True)


def _merge(x, mods, mod_row, g1, g2, cact, heads, w_g, w_pw, w_ao, w_out, w_r):
    nreq, seq, _ = x.shape
    tm = MERGE_ROWS
    tok = lambda b, j: (b, j, 0)
    const2 = lambda b, j: (0, 0)
    return pl.pallas_call(
        _merge_kernel,
        grid=(nreq, seq // tm),
        in_specs=[
            pl.BlockSpec((1, tm, D_MODEL), tok),
            pl.BlockSpec((1, N_MOD, D_MODEL), lambda b, j: (mod_row(b), 0, 0)),
            pl.BlockSpec((1, D_MODEL), const2),
            pl.BlockSpec((1, D_MODEL), const2),
            pl.BlockSpec((1, tm, D_CONV), tok),
            pl.BlockSpec((1, tm, D_ATTN), tok),
            pl.BlockSpec((D_MODEL, D_IN // 2), lambda b, j: (0, 1)),
            pl.BlockSpec(w_pw.shape, const2),
            pl.BlockSpec(w_ao.shape, const2),
            pl.BlockSpec(w_out.shape, const2),
            pl.BlockSpec(w_r.shape, const2),
        ],
        out_specs=[
            pl.BlockSpec((1, tm, D_MODEL), tok),
            pl.BlockSpec((1, tm, D_MODEL), tok),
            pl.BlockSpec((1, tm, N_EXPERTS), tok),
        ],
        out_shape=[
            jax.ShapeDtypeStruct((nreq, seq, D_MODEL), F32),
            jax.ShapeDtypeStruct((nreq, seq, D_MODEL), BF16),
            jax.ShapeDtypeStruct((nreq, seq, N_EXPERTS), F32),
        ],
        compiler_params=_params("arbitrary", "arbitrary"),
        name="merge",
    )(x, mods, g1, g2, cact, heads, w_g, w_pw, w_ao, w_out, w_r)


def _route_kernel(aff_ref, slot_t_ref, slot_r_ref, aff_r_ref, *, cap):
    a = aff_ref[0]
    seq = a.shape[0]
    one, zero = jnp.ones_like(a), jnp.zeros_like(a)

    def search(i, thr):
        cand = thr | jnp.left_shift(jnp.int32(1), 30 - i)
        cnt = jnp.sum(jnp.where(a >= pltpu.bitcast(cand, F32), one, zero), axis=0, keepdims=True)
        return jnp.where(cnt >= cap, cand, thr)

    thr = lax.fori_loop(0, 31, search, jnp.zeros((1, LANES), jnp.int32))
    gt = a >= pltpu.bitcast(thr + 1, F32)
    eq = (a >= pltpu.bitcast(thr, F32)) & jnp.logical_not(gt)
    need = cap - jnp.sum(jnp.where(gt, one, zero), axis=0, keepdims=True)
    ti = lax.broadcasted_iota(jnp.int32, (seq, seq), 0)
    tj = lax.broadcasted_iota(jnp.int32, (seq, seq), 1)
    before = jnp.where(tj < ti, 1.0, 0.0).astype(BF16)
    eq_rank = _dot(before, jnp.where(eq, one, zero).astype(BF16))
    sel = gt | (eq & (eq_rank < need))
    pos = _dot(before, jnp.where(sel, one, zero).astype(BF16))
    slot = jnp.where(sel, pos, -1.0)
    slot_t_ref[0] = slot
    slot_r_ref[0] = slot.T
    aff_r_ref[0] = a.T


def _route(aff_t, cap):
    ngroup, seq, _ = aff_t.shape
    tspec = pl.BlockSpec((1, seq, LANES), lambda g: (g, 0, 0))
    rspec = pl.BlockSpec((1, LANES, seq), lambda g: (g, 0, 0))
    return pl.pallas_call(
        functools.partial(_route_kernel, cap=cap),
        grid=(ngroup,),
        in_specs=[tspec],
        out_specs=[tspec, rspec, rspec],
        out_shape=[
            jax.ShapeDtypeStruct((ngroup, seq, LANES), F32),
            jax.ShapeDtypeStruct((ngroup, LANES, seq), F32),
            jax.ShapeDtypeStruct((ngroup, LANES, seq), F32),
        ],
        compiler_params=_params("arbitrary"),
        name="route",
    )(aff_t)


GATHER_ROWS = 512


def _gather_kernel(slot_ref, aff_ref, h_ref, x_ref, g_ref, *p_refs, cap):
    per_step, seq, _ = h_ref.shape
    nslot = N_EXPERTS * cap
    sub = lax.broadcasted_iota(jnp.int32, (cap, seq), 0).astype(F32)
    one, zero = jnp.ones((cap, seq), F32), jnp.zeros((cap, seq), F32)
    ei = lax.broadcasted_iota(jnp.int32, (N_EXPERTS, nslot), 0)
    si = lax.broadcasted_iota(jnp.int32, (N_EXPERTS, nslot), 1)
    own = (si >= ei * cap) & (si < (ei + 1) * cap)
    per_chunk = GATHER_ROWS // cap
    for r, p_ref in enumerate(p_refs):
        for e in range(N_EXPERTS):
            row = jnp.broadcast_to(slot_ref[0, r * N_EXPERTS + e:r * N_EXPERTS + e + 1, :], (cap, seq))
            p_ref[e * cap:(e + 1) * cap, :] = jnp.where(row == sub, one, zero).astype(BF16)
        h = h_ref[r]
        for c in range(nslot // GATHER_ROWS):
            rows = slice(c * GATHER_ROWS, (c + 1) * GATHER_ROWS)
            x = _dot(p_ref[rows, :], h).astype(x_ref.dtype)
            for i in range(per_chunk):
                x_ref[c * per_chunk + i, r * cap:(r + 1) * cap, :] = x[i * cap:(i + 1) * cap]
        a = aff_ref[0, r * N_EXPERTS:(r + 1) * N_EXPERTS, :]
        a_hi = a.astype(BF16)
        r1 = a - a_hi.astype(F32)
        a_mid = r1.astype(BF16)
        a_lo = (r1 - a_mid.astype(F32)).astype(BF16)
        a3 = jnp.concatenate([a_hi, a_mid, a_lo], axis=0)
        g3 = lax.dot_general(a3, p_ref[...], _NT, preferred_element_type=F32)
        gfull = g3[0:N_EXPERTS] + g3[N_EXPERTS:2 * N_EXPERTS] + g3[2 * N_EXPERTS:3 * N_EXPERTS]
        g_ref[r] = jnp.sum(jnp.where(own, gfull, 0.0), axis=0, keepdims=True)


def _gather(slot_r, aff_r, h2, cap, per_step):
    nreq, seq, _ = h2.shape
    nslot = N_EXPERTS * cap
    steps_per_group = REQ_PER_GROUP // per_step
    grp = lambda s: (s // steps_per_group, s % steps_per_group, 0)
    x, g = pl.pallas_call(
        functools.partial(_gather_kernel, cap=cap),
        grid=(nreq // per_step,),
        in_specs=[
            pl.BlockSpec((1, per_step * N_EXPERTS, seq), grp),
            pl.BlockSpec((1, per_step * N_EXPERTS, seq), grp),
            pl.BlockSpec((per_step, seq, D_MODEL), lambda s: (s, 0, 0)),
        ],
        out_specs=[
            pl.BlockSpec((N_EXPERTS, per_step * cap, D_MODEL), lambda s: (0, s, 0)),
            pl.BlockSpec((per_step, 1, nslot), lambda s: (s, 0, 0)),
        ],
        out_shape=[
            jax.ShapeDtypeStruct((N_EXPERTS, nreq * cap, D_MODEL), BF16),
            jax.ShapeDtypeStruct((nreq, 1, nslot), F32),
        ],
        scratch_shapes=[pltpu.VMEM((nslot, seq), BF16)] * per_step,
        compiler_params=_params("arbitrary"),
        name="gather",
    )(slot_r, aff_r, h2)
    return x, g


EXPERT_FCHUNK = 512
EXPERT_ROWS = 512
GATE_ROWS = 256


def _expert_kernel(xc_ref, xl_ref, g_ref, wg_ref, wu_ref, wd_ref, yc_ref, yl_ref,
                   acc_ref):
    e = pl.program_id(0)
    f = pl.program_id(1)
    nf = pl.num_programs(1)
    half = xc_ref.shape[1]

    @pl.when((e == 0) & (f == 0))
    def _():
        acc_ref[...] = jnp.zeros_like(acc_ref)

    ri = lax.broadcasted_iota(jnp.int32, (GATE_ROWS, GATE_ROWS), 0)
    ci = lax.broadcasted_iota(jnp.int32, (GATE_ROWS, GATE_ROWS), 1)
    for part, (x_ref, y_ref) in enumerate(((xc_ref, yc_ref), (xl_ref, yl_ref))):
        for i in range(half // EXPERT_ROWS):
            x = x_ref[0, i * EXPERT_ROWS:(i + 1) * EXPERT_ROWS, :]
            gq = _dot(x, wg_ref[0].astype(BF16))
            uq = _dot(x, wu_ref[0].astype(BF16))
            hid = (gq * _sigmoid(gq) * uq).astype(BF16)
            r0 = part * half + i * EXPERT_ROWS
            prev = jnp.where(f == 0, 0.0, acc_ref[r0:r0 + EXPERT_ROWS, :])
            acc = prev + _dot(hid, wd_ref[0].astype(BF16))
            acc_ref[r0:r0 + EXPERT_ROWS, :] = acc
            for j in range(EXPERT_ROWS // GATE_ROWS):
                t = r0 // GATE_ROWS + j
                grow = jnp.broadcast_to(g_ref[0, t:t + 1, :], (GATE_ROWS, GATE_ROWS))
                gcol = jnp.sum(jnp.where(ri == ci, grow, 0.0), axis=-1, keepdims=True)
                y = acc[j * GATE_ROWS:(j + 1) * GATE_ROWS] * gcol
                y_ref[0, i * EXPERT_ROWS + j * GATE_ROWS:i * EXPERT_ROWS + (j + 1) * GATE_ROWS, :] = (
                    y.astype(y_ref.dtype))


def _experts(xc, xl, g, w_gate, w_up, w_down):
    ne, half, _ = xc.shape
    fc = EXPERT_FCHUNK
    xspec = pl.BlockSpec((1, half, D_MODEL), lambda e, f: (e, 0, 0))
    return pl.pallas_call(
        _expert_kernel,
        grid=(ne, D_EXPERT // fc),
        in_specs=[
            xspec, xspec,
            pl.BlockSpec((1,) + g.shape[1:], lambda e, f: (e, 0, 0)),
            pl.BlockSpec((1, D_MODEL, fc), lambda e, f: (e, 0, f)),
            pl.BlockSpec((1, D_MODEL, fc), lambda e, f: (e, 0, f)),
            pl.BlockSpec((1, fc, D_MODEL), lambda e, f: (e, f, 0)),
        ],
        out_specs=[xspec, xspec],
        out_shape=[jax.ShapeDtypeStruct(xc.shape, BF16), jax.ShapeDtypeStruct(xl.shape, BF16)],
        scratch_shapes=[pltpu.VMEM((2 * half, D_MODEL), F32)],
        compiler_params=_params("arbitrary", "arbitrary"),
        name="experts",
    )(xc, xl, g, w_gate, w_up, w_down)


COMBINE_ROWS = 256
MOE_CTX_PER_STEP = 4


def _combine_kernel(x1_ref, mods_ref, slot_ref, y_ref, gf_ref, o_ref, *, cap):
    per_step = x1_ref.shape[0]
    nslot = N_EXPERTS * cap
    li = lax.broadcasted_iota(jnp.int32, (LANES, nslot), 0)
    si = lax.broadcasted_iota(jnp.int32, (LANES, nslot), 1)
    for r in range(per_step):
        b = pl.program_id(0) * per_step + r
        lane0 = (b % REQ_PER_GROUP) * N_EXPERTS
        expand = jnp.where(li == lane0 + si // cap, 1.0, 0.0).astype(BF16)
        slot_exp = _dot(slot_ref[0].astype(BF16), expand)
        want = (lax.broadcasted_iota(jnp.int32, slot_exp.shape, 1) & (cap - 1)).astype(F32)
        scat = jnp.where(slot_exp == want, 1.0, 0.0).astype(BF16)
        moe = _dot(scat, y_ref[:, r * cap:(r + 1) * cap, :].reshape(nslot, D_MODEL))
        x2 = x1_ref[r] + mods_ref[0][5:6] * moe
        o_ref[r] = x2 * lax.rsqrt(jnp.mean(x2 * x2, axis=-1, keepdims=True) + EPS) * gf_ref[...]


def _combine(x1, mods, mod_row, slot_t, y, gf, cap, per_step):
    nreq, seq, _ = x1.shape
    tm = COMBINE_ROWS
    assert per_step == 1 or seq == tm
    tok = lambda s, j: (s, j, 0)
    return pl.pallas_call(
        functools.partial(_combine_kernel, cap=cap),
        grid=(nreq // per_step, seq // tm),
        in_specs=[
            pl.BlockSpec((per_step, tm, D_MODEL), tok),
            pl.BlockSpec((1, N_MOD, D_MODEL), lambda s, j: (mod_row(s * per_step), 0, 0)),
            pl.BlockSpec((1, tm, LANES), lambda s, j: (s * per_step // REQ_PER_GROUP, j, 0)),
            pl.BlockSpec((N_EXPERTS, per_step * cap, D_MODEL), lambda s, j: (0, s, 0)),
            pl.BlockSpec((1, D_MODEL), lambda s, j: (0, 0)),
        ],
        out_specs=pl.BlockSpec((per_step, tm, D_MODEL), tok),
        out_shape=jax.ShapeDtypeStruct(x1.shape, F32),
        compiler_params=_params("arbitrary", "arbitrary"),
        name="combine",
    )(x1, mods, slot_t, y, gf)


def _group_lanes(aff):
    nreq, seq, ne = aff.shape
    a = aff.reshape(nreq // REQ_PER_GROUP, REQ_PER_GROUP, seq, ne)
    return jnp.transpose(a, (0, 2, 1, 3)).reshape(nreq // REQ_PER_GROUP, seq, REQ_PER_GROUP * ne)


def kernel(x_prompt, x_sample, cache_ctx_k, cache_ctx_v, c, c_ctx, w_ada, b_ada, norm1_g, w_in,
           conv_dw_w, conv_dw_b, conv_ln_g, conv_ln_b, w_conv_pw, na_rpb, w_attn_o, w_out,
           norm2_g, w_router, w_gate_e, w_up_e, w_down_e, final_g):
    assert w_ada.shape[0] == 1, "single trunk layer"
    nctx, ctx_seq, _ = x_prompt.shape
    nlat, lat_seq, _ = x_sample.shape
    ctx_cap = EC_CAPACITY * ctx_seq // N_EXPERTS
    lat_cap = EC_CAPACITY * lat_seq // N_EXPERTS
    assert nctx * ctx_cap == nlat * lat_cap

    cond = jnp.zeros((2 * nlat, D_MODEL), F32).at[:nlat].set(c).at[nlat].set(c_ctx)
    mods = _ada(cond, w_ada[0], b_ada).reshape(2 * nlat, N_MOD, D_MODEL)
    ctx_row = lambda b: nlat
    lat_row = lambda b: b

    w_a = w_g = w_in[0].astype(BF16)
    w_pw = w_conv_pw[0].astype(BF16)
    w_ao = w_attn_o[0].astype(BF16)
    w_o = w_out[0].astype(BF16)
    w_r = w_router[0].astype(BF16)
    conv_args = (conv_dw_w[0], conv_dw_b, conv_ln_g, conv_ln_b)

    q_c, k_c, v_c, cact_c = _front(x_prompt, mods, ctx_row, norm1_g, w_a, *conv_args, F32,
                                   FRONT_CTX_PER_STEP)
    q_l, k_l, v_l, cact_l = _front(x_sample, mods, lat_row, norm1_g, w_a, *conv_args, BF16, 1)

    heads_c = _ctx_attn(q_c, k_c, v_c)
    tab = _bias_table(na_rpb[0])
    past = cache_ctx_k.shape[2]
    feat_major = lambda t: jnp.transpose(t[:, 0], (0, 2, 3, 1)).reshape(nlat, D_ATTN, past)
    heads_l = _na_attn(q_l, k_l, v_l, feat_major(cache_ctx_k), feat_major(cache_ctx_v), tab)

    merge_w = (w_g, w_pw, w_ao, w_o, w_r)
    flat = lambda t: t.reshape(1, nctx * ctx_seq, t.shape[-1])
    x1_c, h2_c, aff_c = _merge(flat(x_prompt), mods, ctx_row, norm1_g, norm2_g, flat(cact_c),
                               flat(heads_c), *merge_w)
    x1_c, h2_c, aff_c = (t.reshape(nctx, ctx_seq, t.shape[-1]) for t in (x1_c, h2_c, aff_c))
    x1_l, h2_l, aff_l = _merge(x_sample, mods, lat_row, norm1_g, norm2_g, cact_l, heads_l, *merge_w)

    slot_t_c, slot_r_c, aff_r_c = _route(_group_lanes(aff_c), ctx_cap)
    slot_t_l, slot_r_l, aff_r_l = _route(_group_lanes(aff_l), lat_cap)

    xg_c, g_c = _gather(slot_r_c, aff_r_c, h2_c, ctx_cap, MOE_CTX_PER_STEP)
    xg_l, g_l = _gather(slot_r_l, aff_r_l, h2_l, lat_cap, 1)

    def gate_rows(g, nreq, cap):
        g = jnp.transpose(g.reshape(nreq, N_EXPERTS, cap), (1, 0, 2))
        return g.reshape(N_EXPERTS, nreq * cap // GATE_ROWS, GATE_ROWS)

    gates = jnp.concatenate([gate_rows(g_c, nctx, ctx_cap), gate_rows(g_l, nlat, lat_cap)], axis=1)
    y_c, y_l = _experts(xg_c, xg_l, gates, w_gate_e[0], w_up_e[0], w_down_e[0])

    gf = final_g.reshape(1, D_MODEL)
    y_prompt = _combine(x1_c, mods, ctx_row, slot_t_c, y_c, gf, ctx_cap, MOE_CTX_PER_STEP)
    y_sample = _combine(x1_l, mods, lat_row, slot_t_l, y_l, gf, lat_cap, 1)

    state_k = k_c.reshape(nctx, 1, ctx_seq, N_HEADS, HEAD_DIM)
    state_v = v_c.reshape(nctx, 1, ctx_seq, N_HEADS, HEAD_DIM)
    return (y_prompt, y_sample, state_k, state_v)
```

```python
import functools

import jax
import jax.numpy as jnp
from jax import lax
from jax.experimental import pallas as pl
from jax.experimental.pallas import tpu as pltpu

F32 = jnp.float32
BF16 = jnp.bfloat16

D_MODEL = 1024
D_CONV = 512
CONV_WIDTH = 31
N_HEADS = 8
HEAD_DIM = 64
D_ATTN = N_HEADS * HEAD_DIM
GRID_W = 64
NA_ROWS = 8
NA_COLS = 16
N_EXPERTS = 16
D_EXPERT = 2048
EC_CAPACITY = 2
N_MOD = 6
EPS = 1e-6
D_PROJ_A = 2 * D_CONV + 3 * D_ATTN
D_IN = D_PROJ_A + 2 * D_MODEL
LANES = 128
REQ_PER_GROUP = LANES // N_EXPERTS
MASK_NEG = -1e30
VMEM_LIMIT = 56 * 1024 * 1024

_NT = (((1,), (1,)), ((), ()))


def _params(*sem):
    return pltpu.CompilerParams(dimension_semantics=sem, vmem_limit_bytes=VMEM_LIMIT)


def _dot(a, b):
    return jnp.dot(a, b, preferred_element_type=F32)


def _sigmoid(x):
    return 1.0 / (1.0 + jnp.exp(-x))


def _split_bf16(x):
    hi = x.astype(BF16)
    lo = (x - hi.astype(F32)).astype(BF16)
    return hi, lo


def _rms_mod(x, g, shift, scale):
    xn = x * lax.rsqrt(jnp.mean(x * x, axis=-1, keepdims=True) + EPS) * g
    return xn * (1.0 + scale) + shift


def _ada_kernel(cond_ref, w_ref, b_ref, o_ref):
    c = cond_ref[...]
    a = c * _sigmoid(c)
    ah, al = _split_bf16(a)
    wh, wl = _split_bf16(w_ref[...])
    o_ref[...] = _dot(ah, wh) + _dot(al, wh) + _dot(ah, wl) + b_ref[...]


def _ada(cond, w_ada, b_ada):
    n = w_ada.shape[1]
    bn = 1024
    return pl.pallas_call(
        _ada_kernel,
        grid=(n // bn,),
        in_specs=[
            pl.BlockSpec(cond.shape, lambda j: (0, 0)),
            pl.BlockSpec((D_MODEL, bn), lambda j: (0, j)),
            pl.BlockSpec((1, bn), lambda j: (0, j)),
        ],
        out_specs=pl.BlockSpec((cond.shape[0], bn), lambda j: (0, j)),
        out_shape=jax.ShapeDtypeStruct((cond.shape[0], n), F32),
        compiler_params=_params("arbitrary"),
        name="ada",
    )(cond, w_ada, b_ada)


CONV_PAD = 16
FRONT_ROWS = 256
CONV_ROWS = 128
SUBLANES = 8
FRONT_CTX_PER_STEP = 4


def _front_kernel(x_ref, mods_ref, g1_ref, w_ref, dww_ref, dwb_ref, lng_ref, lnb_ref,
                  q_ref, k_ref, v_ref, c_ref, *gpad_refs, seq):
    nreq = x_ref.shape[0]
    m = mods_ref[0]
    shift, scale = m[0:1], m[1:2]
    zeros = jnp.zeros((CONV_PAD, D_CONV), F32)
    for gpad_ref in gpad_refs:
        gpad_ref[0:CONV_PAD, :] = zeros
        gpad_ref[CONV_PAD + seq:CONV_PAD + seq + CONV_PAD, :] = zeros

    def proj(r, i):
        rows = slice(i * FRONT_ROWS, (i + 1) * FRONT_ROWS)
        x = x_ref[r, rows, :]
        h = _rms_mod(x, g1_ref[...], shift, scale).astype(BF16)
        u = _dot(h, w_ref[...])
        glu = u[:, :D_CONV] * _sigmoid(u[:, D_CONV:2 * D_CONV])
        g0 = CONV_PAD + i * FRONT_ROWS
        gpad_refs[r][g0:g0 + FRONT_ROWS, :] = glu
        o = 2 * D_CONV
        q_ref[r, rows, :] = (u[:, o:o + D_ATTN] * (HEAD_DIM ** -0.5)).astype(q_ref.dtype)
        k_ref[r, rows, :] = u[:, o + D_ATTN:o + 2 * D_ATTN].astype(k_ref.dtype)
        v_ref[r, rows, :] = u[:, o + 2 * D_ATTN:o + 3 * D_ATTN].astype(v_ref.dtype)

    def conv(r, j):
        g0 = j * CONV_ROWS
        cols = []
        for cb in range(D_CONV // LANES):
            cs = slice(cb * LANES, (cb + 1) * LANES)
            win = gpad_refs[r][g0:g0 + CONV_ROWS + 2 * CONV_PAD, cs]
            acc = None
            for s in range(SUBLANES):
                nrow = CONV_ROWS + (SUBLANES if s else 0)
                z = None
                for a in range(2 * CONV_PAD // SUBLANES):
                    t = SUBLANES * a + s - (CONV_PAD - CONV_WIDTH // 2)
                    if 0 <= t < CONV_WIDTH:
                        term = win[SUBLANES * a:SUBLANES * a + nrow] * dww_ref[t:t + 1, cs]
                        z = term if z is None else z + term
                z = z[s:s + CONV_ROWS]
                acc = z if acc is None else acc + z
            cols.append(acc)
        y = jnp.concatenate(cols, axis=1) + dwb_ref[...]
        mu = jnp.mean(y, axis=-1, keepdims=True)
        yc = y - mu
        var = jnp.mean(yc * yc, axis=-1, keepdims=True)
        z = yc * lax.rsqrt(var + EPS) * lng_ref[...] + lnb_ref[...]
        c_ref[r, j * CONV_ROWS:(j + 1) * CONV_ROWS, :] = (z * _sigmoid(z)).astype(c_ref.dtype)

    chunks = [(r, i) for r in range(nreq) for i in range(seq // FRONT_ROWS)]
    blocks = [(r, j) for r in range(nreq) for j in range(seq // CONV_ROWS)]

    def ready(block, done):
        r, j = block
        last = min((j + 1) * CONV_ROWS + CONV_WIDTH // 2, seq) - 1
        return (r, last // FRONT_ROWS) in done

    done, pending = set(), list(blocks)
    proj(*chunks[0])
    for n, chunk in enumerate(chunks):
        done.add(chunk)
        if n + 1 < len(chunks):
            proj(*chunks[n + 1])
        now = [b for b in pending if ready(b, done)]
        pending = [b for b in pending if b not in now]
        for b in now:
            conv(*b)


def _front(x, mods, mod_row, g1, w_a, dww, dwb, lng, lnb, kv_dtype, per_step):
    nreq, seq, _ = x.shape
    const2 = lambda b: (0, 0)
    tok = lambda b: (b, 0, 0)
    out_tok = pl.BlockSpec((per_step, seq, D_ATTN), tok)
    return pl.pallas_call(
        functools.partial(_front_kernel, seq=seq),
        grid=(nreq // per_step,),
        in_specs=[
            pl.BlockSpec((per_step, seq, D_MODEL), tok),
            pl.BlockSpec((1, N_MOD, D_MODEL), lambda b: (mod_row(b * per_step), 0, 0)),
            pl.BlockSpec((1, D_MODEL), const2),
            pl.BlockSpec((D_MODEL, D_PROJ_A), const2),
            pl.BlockSpec((CONV_WIDTH, D_CONV), const2),
            pl.BlockSpec((1, D_CONV), const2),
            pl.BlockSpec((1, D_CONV), const2),
            pl.BlockSpec((1, D_CONV), const2),
        ],
        out_specs=[out_tok, out_tok, out_tok, out_tok],
        out_shape=[
            jax.ShapeDtypeStruct((nreq, seq, D_ATTN), BF16),
            jax.ShapeDtypeStruct((nreq, seq, D_ATTN), kv_dtype),
            jax.ShapeDtypeStruct((nreq, seq, D_ATTN), kv_dtype),
            jax.ShapeDtypeStruct((nreq, seq, D_CONV), BF16),
        ],
        scratch_shapes=[pltpu.VMEM((seq + 2 * CONV_PAD, D_CONV), F32)] * per_step,
        compiler_params=_params("arbitrary"),
        name="front",
    )(x, mods, g1, w_a, dww, dwb, lng, lnb)


def _pair_scores_q(q_pair):
    lane = lax.broadcasted_iota(jnp.int32, q_pair.shape, 1)
    zero = jnp.zeros_like(q_pair)
    return jnp.concatenate([jnp.where(lane < HEAD_DIM, q_pair, zero),
                            jnp.where(lane >= HEAD_DIM, q_pair, zero)], axis=0)


def _pair_merge(o2, t):
    lane = lax.broadcasted_iota(jnp.int32, (t, LANES), 1)
    return jnp.where(lane < HEAD_DIM, o2[:t], o2[t:])


CTX_ATTN_PER_STEP = 4


def _ctx_attn_kernel(q_ref, k_ref, v_ref, o_ref):
    nreq, seq, _ = q_ref.shape
    sls = [slice(p * LANES, (p + 1) * LANES) for p in range(N_HEADS // 2)]
    chains = [(r, p) for r in range(nreq) for p in range(N_HEADS // 2)]
    s = {(r, p): lax.dot_general(_pair_scores_q(q_ref[r, :, sls[p]]), k_ref[r, :, sls[p]].astype(BF16),
                                 _NT, preferred_element_type=F32) for r, p in chains}
    e = {c: jnp.exp(s[c] - jnp.max(s[c], axis=-1, keepdims=True)) for c in chains}
    den = {c: jnp.sum(e[c], axis=-1, keepdims=True) for c in chains}
    for r, p in chains:
        o2 = _dot(e[r, p].astype(BF16), v_ref[r, :, sls[p]].astype(BF16)) / den[r, p]
        o_ref[r, :, sls[p]] = _pair_merge(o2, seq).astype(o_ref.dtype)


def _ctx_attn(q, k, v):
    nreq, seq, _ = q.shape
    spec = pl.BlockSpec((CTX_ATTN_PER_STEP, seq, D_ATTN), lambda b: (b, 0, 0))
    return pl.pallas_call(
        _ctx_attn_kernel,
        grid=(nreq // CTX_ATTN_PER_STEP,),
        in_specs=[spec, spec, spec],
        out_specs=spec,
        out_shape=jax.ShapeDtypeStruct((nreq, seq, D_ATTN), BF16),
        compiler_params=_params("arbitrary"),
        name="ctx_attn",
    )(q, k, v)


N_DR = 2 * NA_ROWS - 1
N_DC = 2 * NA_COLS - 1


def _bias_kernel(rpb_ref, o_ref, t2_ref):
    h = pl.program_id(0)
    qi = lax.broadcasted_iota(jnp.int32, (GRID_W, LANES), 0)
    lane = lax.broadcasted_iota(jnp.int32, (GRID_W, LANES), 1)
    kc = lane & (GRID_W - 1)
    d = kc - qi
    cs = jnp.clip(qi - NA_COLS // 2, 0, GRID_W - NA_COLS)
    inside = (kc >= cs) & (kc < cs + NA_COLS)
    group = 5
    for dr0 in range(0, N_DR, group):
        ts = [jnp.full((GRID_W, LANES), MASK_NEG, F32)] * group
        for j in range(N_DC):
            hit = d == j - (NA_COLS - 1)
            ts = [jnp.where(hit, rpb_ref[h * (N_DR * N_DC) + (dr0 + g) * N_DC + j], ts[g])
                  for g in range(group)]
        for g in range(group):
            t2_ref[dr0 + g] = jnp.where(inside, ts[g], MASK_NEG)
    for o in range(NA_ROWS):
        for jj in range(NA_ROWS // 2):
            o_ref[0, o, :, jj * LANES:(jj + 1) * LANES] = jnp.where(
                lane < GRID_W, t2_ref[o + 2 * jj], t2_ref[o + 2 * jj + 1])


def _bias_table(rpb):
    return pl.pallas_call(
        _bias_kernel,
        grid=(N_HEADS,),
        in_specs=[pl.BlockSpec(memory_space=pltpu.SMEM)],
        out_specs=pl.BlockSpec((1, NA_ROWS, GRID_W, NA_ROWS * GRID_W), lambda h: (h, 0, 0, 0)),
        out_shape=jax.ShapeDtypeStruct((N_HEADS, NA_ROWS, GRID_W, NA_ROWS * GRID_W), F32),
        scratch_shapes=[pltpu.VMEM((N_DR, GRID_W, LANES), F32)],
        compiler_params=_params("arbitrary"),
        name="na_bias",
    )(rpb.reshape(-1))


NA_STEP_ROWS = 4


def _na_attn_kernel(q_ref, k_ref, v_ref, kct_ref, vct_ref, tab_ref, o_ref, kcb_ref, vcb_ref, *, rows):
    r = pl.program_id(1)

    @pl.when(r == 0)
    def _():
        kcb_ref[...] = kct_ref[0].astype(BF16)
        vcb_ref[...] = vct_ref[0].astype(BF16)

    nloc = NA_ROWS * GRID_W
    sls = [slice(p * LANES, (p + 1) * LANES) for p in range(N_HEADS // 2)]
    k0, off = [], []
    for i in range(NA_STEP_ROWS):
        qrow = r * NA_STEP_ROWS + i
        rs = jnp.clip(qrow - NA_ROWS // 2, 0, rows - NA_ROWS)
        k0.append(pl.multiple_of(rs * GRID_W, GRID_W))
        off.append(rs - qrow + (NA_ROWS - 1))
    chains = [(i, p) for i in range(NA_STEP_ROWS) for p in range(N_HEADS // 2)]
    qrows = [slice(i * GRID_W, (i + 1) * GRID_W) for i in range(NA_STEP_ROWS)]
    s_loc, s_ctx = {}, {}
    for i, p in chains:
        q2 = _pair_scores_q(q_ref[0, qrows[i], sls[p]])
        bias = jnp.concatenate([tab_ref[2 * p, off[i]], tab_ref[2 * p + 1, off[i]]], axis=0)
        kl = k_ref[0, pl.ds(k0[i], nloc), sls[p]]
        s_loc[i, p] = lax.dot_general(q2, kl, _NT, preferred_element_type=F32) + bias
        s_ctx[i, p] = _dot(q2, kcb_ref[sls[p], :])
    mx = {c: jnp.maximum(jnp.max(s_loc[c], axis=-1, keepdims=True),
                         jnp.max(s_ctx[c], axis=-1, keepdims=True)) for c in chains}
    e_loc = {c: jnp.exp(s_loc[c] - mx[c]) for c in chains}
    e_ctx = {c: jnp.exp(s_ctx[c] - mx[c]) for c in chains}
    den = {c: jnp.sum(e_loc[c], axis=-1, keepdims=True) + jnp.sum(e_ctx[c], axis=-1, keepdims=True)
           for c in chains}
    for i, p in chains:
        vl = v_ref[0, pl.ds(k0[i], nloc), sls[p]]
        o_ctx = lax.dot_general(e_ctx[i, p].astype(BF16), vcb_ref[sls[p], :], _NT,
                                preferred_element_type=F32)
        o2 = (_dot(e_loc[i, p].astype(BF16), vl) + o_ctx) / den[i, p]
        o_ref[0, qrows[i], sls[p]] = _pair_merge(o2, GRID_W).astype(o_ref.dtype)


def _na_attn(q, k, v, kc, vc, tab):
    nreq, seq, _ = q.shape
    rows = seq // GRID_W
    past = kc.shape[2]
    qspec = pl.BlockSpec((1, NA_STEP_ROWS * GRID_W, D_ATTN), lambda b, r: (b, r, 0))
    kvspec = pl.BlockSpec((1, seq, D_ATTN), lambda b, r: (b, 0, 0))
    cspec = pl.BlockSpec((1, D_ATTN, past), lambda b, r: (b, 0, 0))
    return pl.pallas_call(
        functools.partial(_na_attn_kernel, rows=rows),
        grid=(nreq, rows // NA_STEP_ROWS),
        in_specs=[qspec, kvspec, kvspec, cspec, cspec,
                  pl.BlockSpec(tab.shape, lambda b, r: (0, 0, 0, 0))],
        out_specs=qspec,
        out_shape=jax.ShapeDtypeStruct((nreq, seq, D_ATTN), BF16),
        scratch_shapes=[pltpu.VMEM((D_ATTN, past), BF16), pltpu.VMEM((D_ATTN, past), BF16)],
        compiler_params=_params("arbitrary", "arbitrary"),
        name="na_attn",
    )(q, k, v, kc, vc, tab)


MERGE_ROWS = 512
MERGE_SUB = 256


def _merge_kernel(x_ref, mods_ref, g1_ref, g2_ref, c_ref, a_ref, wg_ref, wpw_ref, wao_ref,
                  wout_ref, wr_ref, x1_ref, h2_ref, aff_ref):
    m = mods_ref[0]
    subs = range(MERGE_ROWS // MERGE_SUB)
    rows = [slice(i * MERGE_SUB, (i + 1) * MERGE_SUB) for i in subs]
    x = [x_ref[0, rows[i], :] for i in subs]
    h = [_rms_mod(x[i], g1_ref[...], m[0:1], m[1:2]).astype(BF16) for i in subs]
    gates = [_sigmoid(_dot(h[i], wg_ref[:, D_PROJ_A - D_IN // 2:])) for i in subs]
    conv_out = [_dot(c_ref[0, rows[i], :], wpw_ref[...]) for i in subs]
    attn_out = [_dot(a_ref[0, rows[i], :], wao_ref[...]) for i in subs]
    merged = [(gates[i][:, :D_MODEL] * conv_out[i] + gates[i][:, D_MODEL:] * attn_out[i]).astype(BF16)
              for i in subs]
    x1 = [x[i] + m[2:3] * _dot(merged[i], wout_ref[...]) for i in subs]
    h2 = [_rms_mod(x1[i], g2_ref[...], m[3:4], m[4:5]).astype(BF16) for i in subs]
    logits = [_dot(h2[i], wr_ref[...]) for i in subs]
    e = [jnp.exp(logits[i] - jnp.max(logits[i], axis=-1, keepdims=True)) for i in subs]
    for i in subs:
        x1_ref[0, rows[i], :] = x1[i]
        h2_ref[0, rows[i], :] = h2[i]
        aff_ref[0, rows[i], :] = e[i] / jnp.sum(e[i], axis=-1, keepdims=True)


def _merge(x, mods, mod_row, g1, g2, cact, heads, w_g, w_pw, w_ao, w_out, w_r):
    nreq, seq, _ = x.shape
    tm = MERGE_ROWS
    tok = lambda b, j: (b, j, 0)
    const2 = lambda b, j: (0, 0)
    return pl.pallas_call(
        _merge_kernel,
        grid=(nreq, seq // tm),
        in_specs=[
            pl.BlockSpec((1, tm, D_MODEL), tok),
            pl.BlockSpec((1, N_MOD, D_MODEL), lambda b, j: (mod_row(b), 0, 0)),
            pl.BlockSpec((1, D_MODEL), const2),
            pl.BlockSpec((1, D_MODEL), const2),
            pl.BlockSpec((1, tm, D_CONV), tok),
            pl.BlockSpec((1, tm, D_ATTN), tok),
            pl.BlockSpec((D_MODEL, D_IN // 2), lambda b, j: (0, 1)),
            pl.BlockSpec(w_pw.shape, const2),
            pl.BlockSpec(w_ao.shape, const2),
            pl.BlockSpec(w_out.shape, const2),
            pl.BlockSpec(w_r.shape, const2),
        ],
        out_specs=[
            pl.BlockSpec((1, tm, D_MODEL), tok),
            pl.BlockSpec((1, tm, D_MODEL), tok),
            pl.BlockSpec((1, tm, N_EXPERTS), tok),
        ],
        out_shape=[
            jax.ShapeDtypeStruct((nreq, seq, D_MODEL), F32),
            jax.ShapeDtypeStruct((nreq, seq, D_MODEL), BF16),
            jax.ShapeDtypeStruct((nreq, seq, N_EXPERTS), F32),
        ],
        compiler_params=_params("arbitrary", "arbitrary"),
        name="merge",
    )(x, mods, g1, g2, cact, heads, w_g, w_pw, w_ao, w_out, w_r)


def _route_kernel(aff_ref, slot_t_ref, slot_r_ref, aff_r_ref, *, cap):
    a = aff_ref[0]
    seq = a.shape[0]
    one, zero = jnp.ones_like(a), jnp.zeros_like(a)

    def search(i, thr):
        cand = thr | jnp.left_shift(jnp.int32(1), 30 - i)
        cnt = jnp.sum(jnp.where(a >= pltpu.bitcast(cand, F32), one, zero), axis=0, keepdims=True)
        return jnp.where(cnt >= cap, cand, thr)

    thr = lax.fori_loop(0, 31, search, jnp.zeros((1, LANES), jnp.int32))
    gt = a >= pltpu.bitcast(thr + 1, F32)
    eq = (a >= pltpu.bitcast(thr, F32)) & jnp.logical_not(gt)
    need = cap - jnp.sum(jnp.where(gt, one, zero), axis=0, keepdims=True)
    ti = lax.broadcasted_iota(jnp.int32, (seq, seq), 0)
    tj = lax.broadcasted_iota(jnp.int32, (seq, seq), 1)
    before = jnp.where(tj < ti, 1.0, 0.0).astype(BF16)
    eq_rank = _dot(before, jnp.where(eq, one, zero).astype(BF16))
    sel = gt | (eq & (eq_rank < need))
    pos = _dot(before, jnp.where(sel, one, zero).astype(BF16))
    slot = jnp.where(sel, pos, -1.0)
    slot_t_ref[0] = slot
    slot_r_ref[0] = slot.T
    aff_r_ref[0] = a.T


def _route(aff_t, cap):
    ngroup, seq, _ = aff_t.shape
    tspec = pl.BlockSpec((1, seq, LANES), lambda g: (g, 0, 0))
    rspec = pl.BlockSpec((1, LANES, seq), lambda g: (g, 0, 0))
    return pl.pallas_call(
        functools.partial(_route_kernel, cap=cap),
        grid=(ngroup,),
        in_specs=[tspec],
        out_specs=[tspec, rspec, rspec],
        out_shape=[
            jax.ShapeDtypeStruct((ngroup, seq, LANES), F32),
            jax.ShapeDtypeStruct((ngroup, LANES, seq), F32),
            jax.ShapeDtypeStruct((ngroup, LANES, seq), F32),
        ],
        compiler_params=_params("arbitrary"),
        name="route",
    )(aff_t)


GATHER_ROWS = 512


def _gather_kernel(slot_ref, aff_ref, h_ref, x_ref, g_ref, *, cap):
    per_step, seq, _ = h_ref.shape
    nslot = N_EXPERTS * cap
    sub = lax.broadcasted_iota(jnp.int32, (cap, seq), 0).astype(F32)
    one, zero = jnp.ones((cap, seq), F32), jnp.zeros((cap, seq), F32)
    ei = lax.broadcasted_iota(jnp.int32, (N_EXPERTS, nslot), 0)
    si = lax.broadcasted_iota(jnp.int32, (N_EXPERTS, nslot), 1)
    own = (si >= ei * cap) & (si < (ei + 1) * cap)
    per_chunk = GATHER_ROWS // cap
    for r in range(per_step):
        h = h_ref[r]
        a = aff_ref[0, r * N_EXPERTS:(r + 1) * N_EXPERTS, :]
        a_hi = a.astype(BF16)
        r1 = a - a_hi.astype(F32)
        a_mid = r1.astype(BF16)
        a_lo = (r1 - a_mid.astype(F32)).astype(BF16)
        a3 = jnp.concatenate([a_hi, a_mid, a_lo], axis=0)
        g3 = []
        for c in range(nslot // GATHER_ROWS):
            hot = []
            for e in range(c * per_chunk, (c + 1) * per_chunk):
                row = jnp.broadcast_to(slot_ref[0, r * N_EXPERTS + e:r * N_EXPERTS + e + 1, :], (cap, seq))
                hot.append(jnp.where(row == sub, one, zero).astype(BF16))
            hot = jnp.concatenate(hot, axis=0)
            x = _dot(hot, h).astype(x_ref.dtype)
            for i in range(per_chunk):
                x_ref[c * per_chunk + i, r * cap:(r + 1) * cap, :] = x[i * cap:(i + 1) * cap]
            g3.append(lax.dot_general(a3, hot, _NT, preferred_element_type=F32))
        g3 = jnp.concatenate(g3, axis=1)
        gfull = g3[0:N_EXPERTS] + g3[N_EXPERTS:2 * N_EXPERTS] + g3[2 * N_EXPERTS:3 * N_EXPERTS]
        g_ref[r] = jnp.sum(jnp.where(own, gfull, 0.0), axis=0, keepdims=True)


def _gather(slot_r, aff_r, h2, cap, per_step):
    nreq, seq, _ = h2.shape
    nslot = N_EXPERTS * cap
    steps_per_group = REQ_PER_GROUP // per_step
    grp = lambda s: (s // steps_per_group, s % steps_per_group, 0)
    x, g = pl.pallas_call(
        functools.partial(_gather_kernel, cap=cap),
        grid=(nreq // per_step,),
        in_specs=[
            pl.BlockSpec((1, per_step * N_EXPERTS, seq), grp),
            pl.BlockSpec((1, per_step * N_EXPERTS, seq), grp),
            pl.BlockSpec((per_step, seq, D_MODEL), lambda s: (s, 0, 0)),
        ],
        out_specs=[
            pl.BlockSpec((N_EXPERTS, per_step * cap, D_MODEL), lambda s: (0, s, 0)),
            pl.BlockSpec((per_step, 1, nslot), lambda s: (s, 0, 0)),
        ],
        out_shape=[
            jax.ShapeDtypeStruct((N_EXPERTS, nreq * cap, D_MODEL), BF16),
            jax.ShapeDtypeStruct((nreq, 1, nslot), F32),
        ],
        compiler_params=_params("arbitrary"),
        name="gather",
    )(slot_r, aff_r, h2)
    return x, g


EXPERT_FCHUNK = 512
EXPERT_ROWS = 512
GATE_ROWS = 256


def _expert_kernel(xc_ref, xl_ref, g_ref, wg_ref, wu_ref, wd_ref, yc_ref, yl_ref,
                   acc_ref):
    e = pl.program_id(0)
    f = pl.program_id(1)
    nf = pl.num_programs(1)
    half = xc_ref.shape[1]

    @pl.when((e == 0) & (f == 0))
    def _():
        acc_ref[...] = jnp.zeros_like(acc_ref)

    ri = lax.broadcasted_iota(jnp.int32, (GATE_ROWS, GATE_ROWS), 0)
    ci = lax.broadcasted_iota(jnp.int32, (GATE_ROWS, GATE_ROWS), 1)
    for part, (x_ref, y_ref) in enumerate(((xc_ref, yc_ref), (xl_ref, yl_ref))):
        for i in range(half // EXPERT_ROWS):
            x = x_ref[0, i * EXPERT_ROWS:(i + 1) * EXPERT_ROWS, :]
            gq = _dot(x, wg_ref[0].astype(BF16))
            uq = _dot(x, wu_ref[0].astype(BF16))
            hid = (gq * _sigmoid(gq) * uq).astype(BF16)
            r0 = part * half + i * EXPERT_ROWS
            prev = jnp.where(f == 0, 0.0, acc_ref[r0:r0 + EXPERT_ROWS, :])
            acc = prev + _dot(hid, wd_ref[0].astype(BF16))
            acc_ref[r0:r0 + EXPERT_ROWS, :] = acc
            for j in range(EXPERT_ROWS // GATE_ROWS):
                t = r0 // GATE_ROWS + j
                grow = jnp.broadcast_to(g_ref[0, t:t + 1, :], (GATE_ROWS, GATE_ROWS))
                gcol = jnp.sum(jnp.where(ri == ci, grow, 0.0), axis=-1, keepdims=True)
                y = acc[j * GATE_ROWS:(j + 1) * GATE_ROWS] * gcol
                y_ref[0, i * EXPERT_ROWS + j * GATE_ROWS:i * EXPERT_ROWS + (j + 1) * GATE_ROWS, :] = (
                    y.astype(y_ref.dtype))


def _experts(xc, xl, g, w_gate, w_up, w_down):
    ne, half, _ = xc.shape
    fc = EXPERT_FCHUNK
    xspec = pl.BlockSpec((1, half, D_MODEL), lambda e, f: (e, 0, 0))
    return pl.pallas_call(
        _expert_kernel,
        grid=(ne, D_EXPERT // fc),
        in_specs=[
            xspec, xspec,
            pl.BlockSpec((1,) + g.shape[1:], lambda e, f: (e, 0, 0)),
            pl.BlockSpec((1, D_MODEL, fc), lambda e, f: (e, 0, f)),
            pl.BlockSpec((1, D_MODEL, fc), lambda e, f: (e, 0, f)),
            pl.BlockSpec((1, fc, D_MODEL), lambda e, f: (e, f, 0)),
        ],
        out_specs=[xspec, xspec],
        out_shape=[jax.ShapeDtypeStruct(xc.shape, BF16), jax.ShapeDtypeStruct(xl.shape, BF16)],
        scratch_shapes=[pltpu.VMEM((2 * half, D_MODEL), F32)],
        compiler_params=_params("arbitrary", "arbitrary"),
        name="experts",
    )(xc, xl, g, w_gate, w_up, w_down)


COMBINE_ROWS = 256
COMBINE_LAT_ROWS = 512
MOE_CTX_PER_STEP = 4


def _combine_kernel(x1_ref, mods_ref, slot_ref, y_ref, gf_ref, o_ref, *, cap):
    per_step, tm, _ = x1_ref.shape
    nslot = N_EXPERTS * cap
    li = lax.broadcasted_iota(jnp.int32, (LANES, nslot), 0)
    si = lax.broadcasted_iota(jnp.int32, (LANES, nslot), 1)
    want = (lax.broadcasted_iota(jnp.int32, (COMBINE_ROWS, nslot), 1) & (cap - 1)).astype(F32)
    chains = [(r, t) for r in range(per_step) for t in range(tm // COMBINE_ROWS)]
    rows = {t: slice(t * COMBINE_ROWS, (t + 1) * COMBINE_ROWS) for _, t in chains}
    expand = {}
    for r in range(per_step):
        lane0 = ((pl.program_id(0) * per_step + r) % REQ_PER_GROUP) * N_EXPERTS
        expand[r] = jnp.where(li == lane0 + si // cap, 1.0, 0.0).astype(BF16)
    slot_exp = {(r, t): _dot(slot_ref[0, rows[t], :].astype(BF16), expand[r]) for r, t in chains}
    scat = {c: jnp.where(slot_exp[c] == want, 1.0, 0.0).astype(BF16) for c in chains}
    moe = {(r, t): _dot(scat[r, t], y_ref[:, r * cap:(r + 1) * cap, :].reshape(nslot, D_MODEL))
           for r, t in chains}
    for r, t in chains:
        x2 = x1_ref[r, rows[t], :] + mods_ref[0][5:6] * moe[r, t]
        o_ref[r, rows[t], :] = (x2 * lax.rsqrt(jnp.mean(x2 * x2, axis=-1, keepdims=True) + EPS)
                                * gf_ref[...])


def _combine(x1, mods, mod_row, slot_t, y, gf, cap, per_step, tm):
    nreq, seq, _ = x1.shape
    assert per_step == 1 or seq == tm
    tok = lambda s, j: (s, j, 0)
    return pl.pallas_call(
        functools.partial(_combine_kernel, cap=cap),
        grid=(nreq // per_step, seq // tm),
        in_specs=[
            pl.BlockSpec((per_step, tm, D_MODEL), tok),
            pl.BlockSpec((1, N_MOD, D_MODEL), lambda s, j: (mod_row(s * per_step), 0, 0)),
            pl.BlockSpec((1, tm, LANES), lambda s, j: (s * per_step // REQ_PER_GROUP, j, 0)),
            pl.BlockSpec((N_EXPERTS, per_step * cap, D_MODEL), lambda s, j: (0, s, 0)),
            pl.BlockSpec((1, D_MODEL), lambda s, j: (0, 0)),
        ],
        out_specs=pl.BlockSpec((per_step, tm, D_MODEL), tok),
        out_shape=jax.ShapeDtypeStruct(x1.shape, F32),
        compiler_params=_params("arbitrary", "arbitrary"),
        name="combine",
    )(x1, mods, slot_t, y, gf)


def _group_lanes(aff):
    nreq, seq, ne = aff.shape
    a = aff.reshape(nreq // REQ_PER_GROUP, REQ_PER_GROUP, seq, ne)
    return jnp.transpose(a, (0, 2, 1, 3)).reshape(nreq // REQ_PER_GROUP, seq, REQ_PER_GROUP * ne)


def kernel(x_prompt, x_sample, cache_ctx_k, cache_ctx_v, c, c_ctx, w_ada, b_ada, norm1_g, w_in,
           conv_dw_w, conv_dw_b, conv_ln_g, conv_ln_b, w_conv_pw, na_rpb, w_attn_o, w_out,
           norm2_g, w_router, w_gate_e, w_up_e, w_down_e, final_g):
    assert w_ada.shape[0] == 1, "single trunk layer"
    nctx, ctx_seq, _ = x_prompt.shape
    nlat, lat_seq, _ = x_sample.shape
    ctx_cap = EC_CAPACITY * ctx_seq // N_EXPERTS
    lat_cap = EC_CAPACITY * lat_seq // N_EXPERTS
    assert nctx * ctx_cap == nlat * lat_cap

    cond = jnp.zeros((2 * nlat, D_MODEL), F32).at[:nlat].set(c).at[nlat].set(c_ctx)
    mods = _ada(cond, w_ada[0], b_ada).reshape(2 * nlat, N_MOD, D_MODEL)
    ctx_row = lambda b: nlat
    lat_row = lambda b: b

    w_a = w_g = w_in[0].astype(BF16)
    w_pw = w_conv_pw[0].astype(BF16)
    w_ao = w_attn_o[0].astype(BF16)
    w_o = w_out[0].astype(BF16)
    w_r = w_router[0].astype(BF16)
    conv_args = (conv_dw_w[0], conv_dw_b, conv_ln_g, conv_ln_b)

    q_c, k_c, v_c, cact_c = _front(x_prompt, mods, ctx_row, norm1_g, w_a, *conv_args, F32,
                                   FRONT_CTX_PER_STEP)
    q_l, k_l, v_l, cact_l = _front(x_sample, mods, lat_row, norm1_g, w_a, *conv_args, BF16, 1)

    heads_c = _ctx_attn(q_c, k_c, v_c)
    tab = _bias_table(na_rpb[0])
    past = cache_ctx_k.shape[2]
    feat_major = lambda t: jnp.transpose(t[:, 0], (0, 2, 3, 1)).reshape(nlat, D_ATTN, past)
    heads_l = _na_attn(q_l, k_l, v_l, feat_major(cache_ctx_k), feat_major(cache_ctx_v), tab)

    merge_w = (w_g, w_pw, w_ao, w_o, w_r)
    flat = lambda t: t.reshape(1, nctx * ctx_seq, t.shape[-1])
    x1_c, h2_c, aff_c = _merge(flat(x_prompt), mods, ctx_row, norm1_g, norm2_g, flat(cact_c),
                               flat(heads_c), *merge_w)
    x1_c, h2_c, aff_c = (t.reshape(nctx, ctx_seq, t.shape[-1]) for t in (x1_c, h2_c, aff_c))
    x1_l, h2_l, aff_l = _merge(x_sample, mods, lat_row, norm1_g, norm2_g, cact_l, heads_l, *merge_w)

    slot_t_c, slot_r_c, aff_r_c = _route(_group_lanes(aff_c), ctx_cap)
    slot_t_l, slot_r_l, aff_r_l = _route(_group_lanes(aff_l), lat_cap)

    xg_c, g_c = _gather(slot_r_c, aff_r_c, h2_c, ctx_cap, MOE_CTX_PER_STEP)
    xg_l, g_l = _gather(slot_r_l, aff_r_l, h2_l, lat_cap, 1)

    def gate_rows(g, nreq, cap):
        g = jnp.transpose(g.reshape(nreq, N_EXPERTS, cap), (1, 0, 2))
        return g.reshape(N_EXPERTS, nreq * cap // GATE_ROWS, GATE_ROWS)

    gates = jnp.concatenate([gate_rows(g_c, nctx, ctx_cap), gate_rows(g_l, nlat, lat_cap)], axis=1)
    y_c, y_l = _experts(xg_c, xg_l, gates, w_gate_e[0], w_up_e[0], w_down_e[0])

    gf = final_g.reshape(1, D_MODEL)
    y_prompt = _combine(x1_c, mods, ctx_row, slot_t_c, y_c, gf, ctx_cap, MOE_CTX_PER_STEP, ctx_seq)
    y_sample = _combine(x1_l, mods, lat_row, slot_t_l, y_l, gf, lat_cap, 1, COMBINE_LAT_ROWS)

    state_k = k_c.reshape(nctx, 1, ctx_seq, N_HEADS, HEAD_DIM)
    state_v = v_c.reshape(nctx, 1, ctx_seq, N_HEADS, HEAD_DIM)
    return (y_prompt, y_sample, state_k, state_v)
```

```python
import functools

import jax
import jax.numpy as jnp
from jax import lax
from jax.experimental import pallas as pl
from jax.experimental.pallas import tpu as pltpu

F32 = jnp.float32
BF16 = jnp.bfloat16

D_MODEL = 1024
D_CONV = 512
CONV_WIDTH = 31
N_HEADS = 8
HEAD_DIM = 64
D_ATTN = N_HEADS * HEAD_DIM
GRID_W = 64
NA_ROWS = 8
NA_COLS = 16
N_EXPERTS = 16
D_EXPERT = 2048
EC_CAPACITY = 2
N_MOD = 6
EPS = 1e-6
D_PROJ_A = 2 * D_CONV + 3 * D_ATTN
D_IN = D_PROJ_A + 2 * D_MODEL
LANES = 128
REQ_PER_GROUP = LANES // N_EXPERTS
MASK_NEG = -1e30
VMEM_LIMIT = 56 * 1024 * 1024

_NT = (((1,), (1,)), ((), ()))


def _params(*sem):
    return pltpu.CompilerParams(dimension_semantics=sem, vmem_limit_bytes=VMEM_LIMIT)


def _dot(a, b):
    return jnp.dot(a, b, preferred_element_type=F32)


def _sigmoid(x):
    return 1.0 / (1.0 + jnp.exp(-x))


def _split_bf16(x):
    hi = x.astype(BF16)
    lo = (x - hi.astype(F32)).astype(BF16)
    return hi, lo


def _rms_mod(x, g, shift, scale):
    xn = x * lax.rsqrt(jnp.mean(x * x, axis=-1, keepdims=True) + EPS) * g
    return xn * (1.0 + scale) + shift


def _ada_kernel(cond_ref, w_ref, b_ref, o_ref):
    c = cond_ref[...]
    a = c * _sigmoid(c)
    ah, al = _split_bf16(a)
    wh, wl = _split_bf16(w_ref[...])
    o_ref[...] = _dot(ah, wh) + _dot(al, wh) + _dot(ah, wl) + b_ref[...]


def _ada(cond, w_ada, b_ada):
    n = w_ada.shape[1]
    bn = 1024
    return pl.pallas_call(
        _ada_kernel,
        grid=(n // bn,),
        in_specs=[
            pl.BlockSpec(cond.shape, lambda j: (0, 0)),
            pl.BlockSpec((D_MODEL, bn), lambda j: (0, j)),
            pl.BlockSpec((1, bn), lambda j: (0, j)),
        ],
        out_specs=pl.BlockSpec((cond.shape[0], bn), lambda j: (0, j)),
        out_shape=jax.ShapeDtypeStruct((cond.shape[0], n), F32),
        compiler_params=_params("arbitrary"),
        name="ada",
    )(cond, w_ada, b_ada)


CONV_PAD = 16
FRONT_ROWS = 256
CONV_ROWS = 128
SUBLANES = 8
FRONT_CTX_PER_STEP = 4


def _front_kernel(x_ref, mods_ref, g1_ref, w_ref, dww_ref, dwb_ref, lng_ref, lnb_ref,
                  q_ref, k_ref, v_ref, c_ref, *gpad_refs, seq):
    nreq = x_ref.shape[0]
    m = mods_ref[0]
    shift, scale = m[0:1], m[1:2]
    zeros = jnp.zeros((CONV_PAD, D_CONV), F32)
    for gpad_ref in gpad_refs:
        gpad_ref[0:CONV_PAD, :] = zeros
        gpad_ref[CONV_PAD + seq:CONV_PAD + seq + CONV_PAD, :] = zeros

    def proj(r, i):
        rows = slice(i * FRONT_ROWS, (i + 1) * FRONT_ROWS)
        x = x_ref[r, rows, :]
        h = _rms_mod(x, g1_ref[...], shift, scale).astype(BF16)
        u = _dot(h, w_ref[...])
        glu = u[:, :D_CONV] * _sigmoid(u[:, D_CONV:2 * D_CONV])
        g0 = CONV_PAD + i * FRONT_ROWS
        gpad_refs[r][g0:g0 + FRONT_ROWS, :] = glu
        o = 2 * D_CONV
        q_ref[r, rows, :] = (u[:, o:o + D_ATTN] * (HEAD_DIM ** -0.5)).astype(q_ref.dtype)
        k_ref[r, rows, :] = u[:, o + D_ATTN:o + 2 * D_ATTN].astype(k_ref.dtype)
        v_ref[r, rows, :] = u[:, o + 2 * D_ATTN:o + 3 * D_ATTN].astype(v_ref.dtype)

    def conv(r, j):
        g0 = j * CONV_ROWS
        cols = []
        for cb in range(D_CONV // LANES):
            cs = slice(cb * LANES, (cb + 1) * LANES)
            win = gpad_refs[r][g0:g0 + CONV_ROWS + 2 * CONV_PAD, cs]
            acc = None
            for s in range(SUBLANES):
                nrow = CONV_ROWS + (SUBLANES if s else 0)
                z = None
                for a in range(2 * CONV_PAD // SUBLANES):
                    t = SUBLANES * a + s - (CONV_PAD - CONV_WIDTH // 2)
                    if 0 <= t < CONV_WIDTH:
                        term = win[SUBLANES * a:SUBLANES * a + nrow] * dww_ref[t:t + 1, cs]
                        z = term if z is None else z + term
                z = z[s:s + CONV_ROWS]
                acc = z if acc is None else acc + z
            cols.append(acc)
        y = jnp.concatenate(cols, axis=1) + dwb_ref[...]
        mu = jnp.mean(y, axis=-1, keepdims=True)
        yc = y - mu
        var = jnp.mean(yc * yc, axis=-1, keepdims=True)
        z = yc * lax.rsqrt(var + EPS) * lng_ref[...] + lnb_ref[...]
        c_ref[r, j * CONV_ROWS:(j + 1) * CONV_ROWS, :] = (z * _sigmoid(z)).astype(c_ref.dtype)

    chunks = [(r, i) for r in range(nreq) for i in range(seq // FRONT_ROWS)]
    blocks = [(r, j) for r in range(nreq) for j in range(seq // CONV_ROWS)]

    def ready(block, done):
        r, j = block
        last = min((j + 1) * CONV_ROWS + CONV_WIDTH // 2, seq) - 1
        return (r, last // FRONT_ROWS) in done

    done, pending = set(), list(blocks)
    proj(*chunks[0])
    for n, chunk in enumerate(chunks):
        done.add(chunk)
        if n + 1 < len(chunks):
            proj(*chunks[n + 1])
        now = [b for b in pending if ready(b, done)]
        pending = [b for b in pending if b not in now]
        for b in now:
            conv(*b)


def _front(x, mods, mod_row, g1, w_a, dww, dwb, lng, lnb, kv_dtype, per_step):
    nreq, seq, _ = x.shape
    const2 = lambda b: (0, 0)
    tok = lambda b: (b, 0, 0)
    out_tok = pl.BlockSpec((per_step, seq, D_ATTN), tok)
    return pl.pallas_call(
        functools.partial(_front_kernel, seq=seq),
        grid=(nreq // per_step,),
        in_specs=[
            pl.BlockSpec((per_step, seq, D_MODEL), tok),
            pl.BlockSpec((1, N_MOD, D_MODEL), lambda b: (mod_row(b * per_step), 0, 0)),
            pl.BlockSpec((1, D_MODEL), const2),
            pl.BlockSpec((D_MODEL, D_PROJ_A), const2),
            pl.BlockSpec((CONV_WIDTH, D_CONV), const2),
            pl.BlockSpec((1, D_CONV), const2),
            pl.BlockSpec((1, D_CONV), const2),
            pl.BlockSpec((1, D_CONV), const2),
        ],
        out_specs=[out_tok, out_tok, out_tok, out_tok],
        out_shape=[
            jax.ShapeDtypeStruct((nreq, seq, D_ATTN), BF16),
            jax.ShapeDtypeStruct((nreq, seq, D_ATTN), kv_dtype),
            jax.ShapeDtypeStruct((nreq, seq, D_ATTN), kv_dtype),
            jax.ShapeDtypeStruct((nreq, seq, D_CONV), BF16),
        ],
        scratch_shapes=[pltpu.VMEM((seq + 2 * CONV_PAD, D_CONV), F32)] * per_step,
        compiler_params=_params("arbitrary"),
        name="front",
    )(x, mods, g1, w_a, dww, dwb, lng, lnb)


def _pair_scores_q(q_pair):
    lane = lax.broadcasted_iota(jnp.int32, q_pair.shape, 1)
    zero = jnp.zeros_like(q_pair)
    return jnp.concatenate([jnp.where(lane < HEAD_DIM, q_pair, zero),
                            jnp.where(lane >= HEAD_DIM, q_pair, zero)], axis=0)


def _pair_merge(o2, t):
    lane = lax.broadcasted_iota(jnp.int32, (t, LANES), 1)
    return jnp.where(lane < HEAD_DIM, o2[:t], o2[t:])


CTX_ATTN_PER_STEP = 4


def _ctx_attn_kernel(q_ref, k_ref, v_ref, o_ref):
    nreq, seq, _ = q_ref.shape
    sls = [slice(p * LANES, (p + 1) * LANES) for p in range(N_HEADS // 2)]
    chains = [(r, p) for r in range(nreq) for p in range(N_HEADS // 2)]
    s = {(r, p): lax.dot_general(_pair_scores_q(q_ref[r, :, sls[p]]), k_ref[r, :, sls[p]].astype(BF16),
                                 _NT, preferred_element_type=F32) for r, p in chains}
    e = {c: jnp.exp(s[c] - jnp.max(s[c], axis=-1, keepdims=True)) for c in chains}
    den = {c: jnp.sum(e[c], axis=-1, keepdims=True) for c in chains}
    for r, p in chains:
        o2 = _dot(e[r, p].astype(BF16), v_ref[r, :, sls[p]].astype(BF16)) / den[r, p]
        o_ref[r, :, sls[p]] = _pair_merge(o2, seq).astype(o_ref.dtype)


def _ctx_attn(q, k, v):
    nreq, seq, _ = q.shape
    spec = pl.BlockSpec((CTX_ATTN_PER_STEP, seq, D_ATTN), lambda b: (b, 0, 0))
    return pl.pallas_call(
        _ctx_attn_kernel,
        grid=(nreq // CTX_ATTN_PER_STEP,),
        in_specs=[spec, spec, spec],
        out_specs=spec,
        out_shape=jax.ShapeDtypeStruct((nreq, seq, D_ATTN), BF16),
        compiler_params=_params("arbitrary"),
        name="ctx_attn",
    )(q, k, v)


N_DR = 2 * NA_ROWS - 1
N_DC = 2 * NA_COLS - 1


def _bias_kernel(rpb_ref, o_ref, t2_ref):
    h = pl.program_id(0)
    qi = lax.broadcasted_iota(jnp.int32, (GRID_W, LANES), 0)
    lane = lax.broadcasted_iota(jnp.int32, (GRID_W, LANES), 1)
    kc = lane & (GRID_W - 1)
    d = kc - qi
    cs = jnp.clip(qi - NA_COLS // 2, 0, GRID_W - NA_COLS)
    inside = (kc >= cs) & (kc < cs + NA_COLS)
    group = 5
    for dr0 in range(0, N_DR, group):
        ts = [jnp.full((GRID_W, LANES), MASK_NEG, F32)] * group
        for j in range(N_DC):
            hit = d == j - (NA_COLS - 1)
            ts = [jnp.where(hit, rpb_ref[h * (N_DR * N_DC) + (dr0 + g) * N_DC + j], ts[g])
                  for g in range(group)]
        for g in range(group):
            t2_ref[dr0 + g] = jnp.where(inside, ts[g], MASK_NEG)
    for o in range(NA_ROWS):
        for jj in range(NA_ROWS // 2):
            o_ref[0, o, :, jj * LANES:(jj + 1) * LANES] = jnp.where(
                lane < GRID_W, t2_ref[o + 2 * jj], t2_ref[o + 2 * jj + 1])


def _bias_table(rpb):
    return pl.pallas_call(
        _bias_kernel,
        grid=(N_HEADS,),
        in_specs=[pl.BlockSpec(memory_space=pltpu.SMEM)],
        out_specs=pl.BlockSpec((1, NA_ROWS, GRID_W, NA_ROWS * GRID_W), lambda h: (h, 0, 0, 0)),
        out_shape=jax.ShapeDtypeStruct((N_HEADS, NA_ROWS, GRID_W, NA_ROWS * GRID_W), F32),
        scratch_shapes=[pltpu.VMEM((N_DR, GRID_W, LANES), F32)],
        compiler_params=_params("arbitrary"),
        name="na_bias",
    )(rpb.reshape(-1))


NA_STEP_ROWS = 4


def _na_attn_kernel(q_ref, k_ref, v_ref, kct_ref, vct_ref, tab_ref, o_ref, kcb_ref, vcb_ref, *, rows):
    r = pl.program_id(1)

    @pl.when(r == 0)
    def _():
        kcb_ref[...] = kct_ref[0].astype(BF16)
        vcb_ref[...] = vct_ref[0].astype(BF16)

    nloc = NA_ROWS * GRID_W
    sls = [slice(p * LANES, (p + 1) * LANES) for p in range(N_HEADS // 2)]
    k0, off = [], []
    for i in range(NA_STEP_ROWS):
        qrow = r * NA_STEP_ROWS + i
        rs = jnp.clip(qrow - NA_ROWS // 2, 0, rows - NA_ROWS)
        k0.append(pl.multiple_of(rs * GRID_W, GRID_W))
        off.append(rs - qrow + (NA_ROWS - 1))
    chains = [(i, p) for i in range(NA_STEP_ROWS) for p in range(N_HEADS // 2)]
    qrows = [slice(i * GRID_W, (i + 1) * GRID_W) for i in range(NA_STEP_ROWS)]
    s_loc, s_ctx = {}, {}
    for i, p in chains:
        q2 = _pair_scores_q(q_ref[0, qrows[i], sls[p]])
        bias = jnp.concatenate([tab_ref[2 * p, off[i]], tab_ref[2 * p + 1, off[i]]], axis=0)
        kl = k_ref[0, pl.ds(k0[i], nloc), sls[p]]
        s_loc[i, p] = lax.dot_general(q2, kl, _NT, preferred_element_type=F32) + bias
        s_ctx[i, p] = _dot(q2, kcb_ref[sls[p], :])
    mx = {c: jnp.maximum(jnp.max(s_loc[c], axis=-1, keepdims=True),
                         jnp.max(s_ctx[c], axis=-1, keepdims=True)) for c in chains}
    e_loc = {c: jnp.exp(s_loc[c] - mx[c]) for c in chains}
    e_ctx = {c: jnp.exp(s_ctx[c] - mx[c]) for c in chains}
    den = {c: jnp.sum(e_loc[c], axis=-1, keepdims=True) + jnp.sum(e_ctx[c], axis=-1, keepdims=True)
           for c in chains}
    for i, p in chains:
        vl = v_ref[0, pl.ds(k0[i], nloc), sls[p]]
        o_ctx = lax.dot_general(e_ctx[i, p].astype(BF16), vcb_ref[sls[p], :], _NT,
                                preferred_element_type=F32)
        o2 = (_dot(e_loc[i, p].astype(BF16), vl) + o_ctx) / den[i, p]
        o_ref[0, qrows[i], sls[p]] = _pair_merge(o2, GRID_W).astype(o_ref.dtype)


def _na_attn(q, k, v, kc, vc, tab):
    nreq, seq, _ = q.shape
    rows = seq // GRID_W
    past = kc.shape[2]
    qspec = pl.BlockSpec((1, NA_STEP_ROWS * GRID_W, D_ATTN), lambda b, r: (b, r, 0))
    kvspec = pl.BlockSpec((1, seq, D_ATTN), lambda b, r: (b, 0, 0))
    cspec = pl.BlockSpec((1, D_ATTN, past), lambda b, r: (b, 0, 0))
    return pl.pallas_call(
        functools.partial(_na_attn_kernel, rows=rows),
        grid=(nreq, rows // NA_STEP_ROWS),
        in_specs=[qspec, kvspec, kvspec, cspec, cspec,
                  pl.BlockSpec(tab.shape, lambda b, r: (0, 0, 0, 0))],
        out_specs=qspec,
        out_shape=jax.ShapeDtypeStruct((nreq, seq, D_ATTN), BF16),
        scratch_shapes=[pltpu.VMEM((D_ATTN, past), BF16), pltpu.VMEM((D_ATTN, past), BF16)],
        compiler_params=_params("arbitrary", "arbitrary"),
        name="na_attn",
    )(q, k, v, kc, vc, tab)


MERGE_ROWS = 512
MERGE_SUB = 256


def _merge_kernel(x_ref, mods_ref, g1_ref, g2_ref, c_ref, a_ref, wg_ref, wpw_ref, wao_ref,
                  wout_ref, wr_ref, x1_ref, h2_ref, aff_ref):
    m = mods_ref[0]
    subs = range(MERGE_ROWS // MERGE_SUB)
    rows = [slice(i * MERGE_SUB, (i + 1) * MERGE_SUB) for i in subs]
    x = [x_ref[0, rows[i], :] for i in subs]
    h = [_rms_mod(x[i], g1_ref[...], m[0:1], m[1:2]).astype(BF16) for i in subs]
    gates = [_sigmoid(_dot(h[i], wg_ref[:, D_PROJ_A - D_IN // 2:])) for i in subs]
    conv_out = [_dot(c_ref[0, rows[i], :], wpw_ref[...]) for i in subs]
    attn_out = [_dot(a_ref[0, rows[i], :], wao_ref[...]) for i in subs]
    merged = [(gates[i][:, :D_MODEL] * conv_out[i] + gates[i][:, D_MODEL:] * attn_out[i]).astype(BF16)
              for i in subs]
    x1 = [x[i] + m[2:3] * _dot(merged[i], wout_ref[...]) for i in subs]
    h2 = [_rms_mod(x1[i], g2_ref[...], m[3:4], m[4:5]).astype(BF16) for i in subs]
    logits = [_dot(h2[i], wr_ref[...]) for i in subs]
    e = [jnp.exp(logits[i] - jnp.max(logits[i], axis=-1, keepdims=True)) for i in subs]
    for i in subs:
        x1_ref[0, rows[i], :] = x1[i]
        h2_ref[0, rows[i], :] = h2[i]
        aff_ref[0, rows[i], :] = e[i] / jnp.sum(e[i], axis=-1, keepdims=True)


def _merge(x, mods, mod_row, g1, g2, cact, heads, w_g, w_pw, w_ao, w_out, w_r):
    nreq, seq, _ = x.shape
    tm = MERGE_ROWS
    tok = lambda b, j: (b, j, 0)
    const2 = lambda b, j: (0, 0)
    return pl.pallas_call(
        _merge_kernel,
        grid=(nreq, seq // tm),
        in_specs=[
            pl.BlockSpec((1, tm, D_MODEL), tok),
            pl.BlockSpec((1, N_MOD, D_MODEL), lambda b, j: (mod_row(b), 0, 0)),
            pl.BlockSpec((1, D_MODEL), const2),
            pl.BlockSpec((1, D_MODEL), const2),
            pl.BlockSpec((1, tm, D_CONV), tok),
            pl.BlockSpec((1, tm, D_ATTN), tok),
            pl.BlockSpec((D_MODEL, D_IN // 2), lambda b, j: (0, 1)),
            pl.BlockSpec(w_pw.shape, const2),
            pl.BlockSpec(w_ao.shape, const2),
            pl.BlockSpec(w_out.shape, const2),
            pl.BlockSpec(w_r.shape, const2),
        ],
        out_specs=[
            pl.BlockSpec((1, tm, D_MODEL), tok),
            pl.BlockSpec((1, tm, D_MODEL), tok),
            pl.BlockSpec((1, tm, N_EXPERTS), tok),
        ],
        out_shape=[
            jax.ShapeDtypeStruct((nreq, seq, D_MODEL), F32),
            jax.ShapeDtypeStruct((nreq, seq, D_MODEL), BF16),
            jax.ShapeDtypeStruct((nreq, seq, N_EXPERTS), F32),
        ],
        compiler_params=_params("arbitrary", "arbitrary"),
        name="merge",
    )(x, mods, g1, g2, cact, heads, w_g, w_pw, w_ao, w_out, w_r)


def _route_kernel(aff_ref, slot_t_ref, slot_r_ref, aff_r_ref, *, cap):
    a = aff_ref[0]
    seq = a.shape[0]
    one, zero = jnp.ones_like(a), jnp.zeros_like(a)

    def search(i, thr):
        cand = thr | jnp.left_shift(jnp.int32(1), 30 - i)
        cnt = jnp.sum(jnp.where(a >= pltpu.bitcast(cand, F32), one, zero), axis=0, keepdims=True)
        return jnp.where(cnt >= cap, cand, thr)

    thr = lax.fori_loop(0, 31, search, jnp.zeros((1, LANES), jnp.int32))
    gt = a >= pltpu.bitcast(thr + 1, F32)
    eq = (a >= pltpu.bitcast(thr, F32)) & jnp.logical_not(gt)
    need = cap - jnp.sum(jnp.where(gt, one, zero), axis=0, keepdims=True)
    ti = lax.broadcasted_iota(jnp.int32, (seq, seq), 0)
    tj = lax.broadcasted_iota(jnp.int32, (seq, seq), 1)
    before = jnp.where(tj < ti, 1.0, 0.0).astype(BF16)
    eq_rank = _dot(before, jnp.where(eq, one, zero).astype(BF16))
    sel = gt | (eq & (eq_rank < need))
    pos = _dot(before, jnp.where(sel, one, zero).astype(BF16))
    slot = jnp.where(sel, pos, -1.0)
    slot_t_ref[0] = slot
    slot_r_ref[0] = slot.T
    aff_r_ref[0] = a.T


def _route(aff_t, cap):
    ngroup, seq, _ = aff_t.shape
    tspec = pl.BlockSpec((1, seq, LANES), lambda g: (g, 0, 0))
    rspec = pl.BlockSpec((1, LANES, seq), lambda g: (g, 0, 0))
    return pl.pallas_call(
        functools.partial(_route_kernel, cap=cap),
        grid=(ngroup,),
        in_specs=[tspec],
        out_specs=[tspec, rspec, rspec],
        out_shape=[
            jax.ShapeDtypeStruct((ngroup, seq, LANES), F32),
            jax.ShapeDtypeStruct((ngroup, LANES, seq), F32),
            jax.ShapeDtypeStruct((ngroup, LANES, seq), F32),
        ],
        compiler_params=_params("arbitrary"),
        name="route",
    )(aff_t)


GATHER_ROWS = 512


def _gather_kernel(slot_ref, aff_ref, h_ref, x_ref, g_ref, *, cap):
    per_step, seq, _ = h_ref.shape
    nslot = N_EXPERTS * cap
    sub = lax.broadcasted_iota(jnp.int32, (cap, seq), 0).astype(F32)
    one, zero = jnp.ones((cap, seq), F32), jnp.zeros((cap, seq), F32)
    ei = lax.broadcasted_iota(jnp.int32, (N_EXPERTS, nslot), 0)
    si = lax.broadcasted_iota(jnp.int32, (N_EXPERTS, nslot), 1)
    own = (si >= ei * cap) & (si < (ei + 1) * cap)
    per_chunk = GATHER_ROWS // cap
    for r in range(per_step):
        h = h_ref[r]
        a = aff_ref[0, r * N_EXPERTS:(r + 1) * N_EXPERTS, :]
        a_hi = a.astype(BF16)
        r1 = a - a_hi.astype(F32)
        a_mid = r1.astype(BF16)
        a_lo = (r1 - a_mid.astype(F32)).astype(BF16)
        a3 = jnp.concatenate([a_hi, a_mid, a_lo], axis=0)
        g3 = []
        for c in range(nslot // GATHER_ROWS):
            hot = []
            for e in range(c * per_chunk, (c + 1) * per_chunk):
                row = jnp.broadcast_to(slot_ref[0, r * N_EXPERTS + e:r * N_EXPERTS + e + 1, :], (cap, seq))
                hot.append(jnp.where(row == sub, one, zero).astype(BF16))
            hot = jnp.concatenate(hot, axis=0)
            x = _dot(hot, h).astype(x_ref.dtype)
            for i in range(per_chunk):
                x_ref[c * per_chunk + i, r * cap:(r + 1) * cap, :] = x[i * cap:(i + 1) * cap]
            g3.append(lax.dot_general(a3, hot, _NT, preferred_element_type=F32))
        g3 = jnp.concatenate(g3, axis=1)
        gfull = g3[0:N_EXPERTS] + g3[N_EXPERTS:2 * N_EXPERTS] + g3[2 * N_EXPERTS:3 * N_EXPERTS]
        g_ref[r] = jnp.sum(jnp.where(own, gfull, 0.0), axis=0, keepdims=True)


def _gather(slot_r, aff_r, h2, cap, per_step):
    nreq, seq, _ = h2.shape
    nslot = N_EXPERTS * cap
    steps_per_group = REQ_PER_GROUP // per_step
    grp = lambda s: (s // steps_per_group, s % steps_per_group, 0)
    x, g = pl.pallas_call(
        functools.partial(_gather_kernel, cap=cap),
        grid=(nreq // per_step,),
        in_specs=[
            pl.BlockSpec((1, per_step * N_EXPERTS, seq), grp),
            pl.BlockSpec((1, per_step * N_EXPERTS, seq), grp),
            pl.BlockSpec((per_step, seq, D_MODEL), lambda s: (s, 0, 0)),
        ],
        out_specs=[
            pl.BlockSpec((N_EXPERTS, per_step * cap, D_MODEL), lambda s: (0, s, 0)),
            pl.BlockSpec((per_step, 1, nslot), lambda s: (s, 0, 0)),
        ],
        out_shape=[
            jax.ShapeDtypeStruct((N_EXPERTS, nreq * cap, D_MODEL), BF16),
            jax.ShapeDtypeStruct((nreq, 1, nslot), F32),
        ],
        compiler_params=_params("arbitrary"),
        name="gather",
    )(slot_r, aff_r, h2)
    return x, g


EXPERT_FCHUNK = 1024
EXPERT_ROWS = 512
GATE_ROWS = 256


def _expert_kernel(xc_ref, xl_ref, g_ref, wg_ref, wu_ref, wd_ref, yc_ref, yl_ref,
                   acc_ref):
    e = pl.program_id(0)
    f = pl.program_id(1)
    nf = pl.num_programs(1)
    half = xc_ref.shape[1]

    @pl.when((e == 0) & (f == 0))
    def _():
        acc_ref[...] = jnp.zeros_like(acc_ref)

    ri = lax.broadcasted_iota(jnp.int32, (GATE_ROWS, GATE_ROWS), 0)
    ci = lax.broadcasted_iota(jnp.int32, (GATE_ROWS, GATE_ROWS), 1)
    for part, (x_ref, y_ref) in enumerate(((xc_ref, yc_ref), (xl_ref, yl_ref))):
        for i in range(half // EXPERT_ROWS):
            x = x_ref[0, i * EXPERT_ROWS:(i + 1) * EXPERT_ROWS, :]
            gq = _dot(x, wg_ref[0].astype(BF16))
            uq = _dot(x, wu_ref[0].astype(BF16))
            hid = (gq * _sigmoid(gq) * uq).astype(BF16)
            r0 = part * half + i * EXPERT_ROWS
            prev = jnp.where(f == 0, 0.0, acc_ref[r0:r0 + EXPERT_ROWS, :])
            acc = prev + _dot(hid, wd_ref[0].astype(BF16))
            acc_ref[r0:r0 + EXPERT_ROWS, :] = acc
            for j in range(EXPERT_ROWS // GATE_ROWS):
                t = r0 // GATE_ROWS + j
                grow = jnp.broadcast_to(g_ref[0, t:t + 1, :], (GATE_ROWS, GATE_ROWS))
                gcol = jnp.sum(jnp.where(ri == ci, grow, 0.0), axis=-1, keepdims=True)
                y = acc[j * GATE_ROWS:(j + 1) * GATE_ROWS] * gcol
                y_ref[0, i * EXPERT_ROWS + j * GATE_ROWS:i * EXPERT_ROWS + (j + 1) * GATE_ROWS, :] = (
                    y.astype(y_ref.dtype))


def _experts(xc, xl, g, w_gate, w_up, w_down):
    ne, half, _ = xc.shape
    fc = EXPERT_FCHUNK
    xspec = pl.BlockSpec((1, half, D_MODEL), lambda e, f: (e, 0, 0))
    return pl.pallas_call(
        _expert_kernel,
        grid=(ne, D_EXPERT // fc),
        in_specs=[
            xspec, xspec,
            pl.BlockSpec((1,) + g.shape[1:], lambda e, f: (e, 0, 0)),
            pl.BlockSpec((1, D_MODEL, fc), lambda e, f: (e, 0, f)),
            pl.BlockSpec((1, D_MODEL, fc), lambda e, f: (e, 0, f)),
            pl.BlockSpec((1, fc, D_MODEL), lambda e, f: (e, f, 0)),
        ],
        out_specs=[xspec, xspec],
        out_shape=[jax.ShapeDtypeStruct(xc.shape, BF16), jax.ShapeDtypeStruct(xl.shape, BF16)],
        scratch_shapes=[pltpu.VMEM((2 * half, D_MODEL), F32)],
        compiler_params=_params("arbitrary", "arbitrary"),
        name="experts",
    )(xc, xl, g, w_gate, w_up, w_down)


COMBINE_ROWS = 256
COMBINE_LAT_ROWS = 512
MOE_CTX_PER_STEP = 4


def _combine_kernel(x1_ref, mods_ref, slot_ref, y_ref, gf_ref, o_ref, *, cap):
    per_step, tm, _ = x1_ref.shape
    nslot = N_EXPERTS * cap
    li = lax.broadcasted_iota(jnp.int32, (LANES, nslot), 0)
    si = lax.broadcasted_iota(jnp.int32, (LANES, nslot), 1)
    want = (lax.broadcasted_iota(jnp.int32, (COMBINE_ROWS, nslot), 1) & (cap - 1)).astype(F32)
    chains = [(r, t) for r in range(per_step) for t in range(tm // COMBINE_ROWS)]
    rows = {t: slice(t * COMBINE_ROWS, (t + 1) * COMBINE_ROWS) for _, t in chains}
    expand = {}
    for r in range(per_step):
        lane0 = ((pl.program_id(0) * per_step + r) % REQ_PER_GROUP) * N_EXPERTS
        expand[r] = jnp.where(li == lane0 + si // cap, 1.0, 0.0).astype(BF16)
    slot_exp = {(r, t): _dot(slot_ref[0, rows[t], :].astype(BF16), expand[r]) for r, t in chains}
    scat = {c: jnp.where(slot_exp[c] == want, 1.0, 0.0).astype(BF16) for c in chains}
    moe = {(r, t): _dot(scat[r, t], y_ref[:, r * cap:(r + 1) * cap, :].reshape(nslot, D_MODEL))
           for r, t in chains}
    for r, t in chains:
        x2 = x1_ref[r, rows[t], :] + mods_ref[0][5:6] * moe[r, t]
        o_ref[r, rows[t], :] = (x2 * lax.rsqrt(jnp.mean(x2 * x2, axis=-1, keepdims=True) + EPS)
                                * gf_ref[...])


def _combine(x1, mods, mod_row, slot_t, y, gf, cap, per_step, tm):
    nreq, seq, _ = x1.shape
    assert per_step == 1 or seq == tm
    tok = lambda s, j: (s, j, 0)
    return pl.pallas_call(
        functools.partial(_combine_kernel, cap=cap),
        grid=(nreq // per_step, seq // tm),
        in_specs=[
            pl.BlockSpec((per_step, tm, D_MODEL), tok),
            pl.BlockSpec((1, N_MOD, D_MODEL), lambda s, j: (mod_row(s * per_step), 0, 0)),
            pl.BlockSpec((1, tm, LANES), lambda s, j: (s * per_step // REQ_PER_GROUP, j, 0)),
            pl.BlockSpec((N_EXPERTS, per_step * cap, D_MODEL), lambda s, j: (0, s, 0)),
            pl.BlockSpec((1, D_MODEL), lambda s, j: (0, 0)),
        ],
        out_specs=pl.BlockSpec((per_step, tm, D_MODEL), tok),
        out_shape=jax.ShapeDtypeStruct(x1.shape, F32),
        compiler_params=_params("arbitrary", "arbitrary"),
        name="combine",
    )(x1, mods, slot_t, y, gf)


def _group_lanes(aff):
    nreq, seq, ne = aff.shape
    a = aff.reshape(nreq // REQ_PER_GROUP, REQ_PER_GROUP, seq, ne)
    return jnp.transpose(a, (0, 2, 1, 3)).reshape(nreq // REQ_PER_GROUP, seq, REQ_PER_GROUP * ne)


def kernel(x_prompt, x_sample, cache_ctx_k, cache_ctx_v, c, c_ctx, w_ada, b_ada, norm1_g, w_in,
           conv_dw_w, conv_dw_b, conv_ln_g, conv_ln_b, w_conv_pw, na_rpb, w_attn_o, w_out,
           norm2_g, w_router, w_gate_e, w_up_e, w_down_e, final_g):
    assert w_ada.shape[0] == 1, "single trunk layer"
    nctx, ctx_seq, _ = x_prompt.shape
    nlat, lat_seq, _ = x_sample.shape
    ctx_cap = EC_CAPACITY * ctx_seq // N_EXPERTS
    lat_cap = EC_CAPACITY * lat_seq // N_EXPERTS
    assert nctx * ctx_cap == nlat * lat_cap

    cond = jnp.zeros((2 * nlat, D_MODEL), F32).at[:nlat].set(c).at[nlat].set(c_ctx)
    mods = _ada(cond, w_ada[0], b_ada).reshape(2 * nlat, N_MOD, D_MODEL)
    ctx_row = lambda b: nlat
    lat_row = lambda b: b

    w_a = w_g = w_in[0].astype(BF16)
    w_pw = w_conv_pw[0].astype(BF16)
    w_ao = w_attn_o[0].astype(BF16)
    w_o = w_out[0].astype(BF16)
    w_r = w_router[0].astype(BF16)
    conv_args = (conv_dw_w[0], conv_dw_b, conv_ln_g, conv_ln_b)

    q_c, k_c, v_c, cact_c = _front(x_prompt, mods, ctx_row, norm1_g, w_a, *conv_args, F32,
                                   FRONT_CTX_PER_STEP)
    q_l, k_l, v_l, cact_l = _front(x_sample, mods, lat_row, norm1_g, w_a, *conv_args, BF16, 1)

    heads_c = _ctx_attn(q_c, k_c, v_c)
    tab = _bias_table(na_rpb[0])
    past = cache_ctx_k.shape[2]
    feat_major = lambda t: jnp.transpose(t[:, 0], (0, 2, 3, 1)).reshape(nlat, D_ATTN, past)
    heads_l = _na_attn(q_l, k_l, v_l, feat_major(cache_ctx_k), feat_major(cache_ctx_v), tab)

    merge_w = (w_g, w_pw, w_ao, w_o, w_r)
    flat = lambda t: t.reshape(1, nctx * ctx_seq, t.shape[-1])
    x1_c, h2_c, aff_c = _merge(flat(x_prompt), mods, ctx_row, norm1_g, norm2_g, flat(cact_c),
                               flat(heads_c), *merge_w)
    x1_c, h2_c, aff_c = (t.reshape(nctx, ctx_seq, t.shape[-1]) for t in (x1_c, h2_c, aff_c))
    x1_l, h2_l, aff_l = _merge(x_sample, mods, lat_row, norm1_g, norm2_g, cact_l, heads_l, *merge_w)

    slot_t_c, slot_r_c, aff_r_c = _route(_group_lanes(aff_c), ctx_cap)
    slot_t_l, slot_r_l, aff_r_l = _route(_group_lanes(aff_l), lat_cap)

    xg_c, g_c = _gather(slot_r_c, aff_r_c, h2_c, ctx_cap, MOE_CTX_PER_STEP)
    xg_l, g_l = _gather(slot_r_l, aff_r_l, h2_l, lat_cap, 1)

    def gate_rows(g, nreq, cap):
        g = jnp.transpose(g.reshape(nreq, N_EXPERTS, cap), (1, 0, 2))
        return g.reshape(N_EXPERTS, nreq * cap // GATE_ROWS, GATE_ROWS)

    gates = jnp.concatenate([gate_rows(g_c, nctx, ctx_cap), gate_rows(g_l, nlat, lat_cap)], axis=1)
    y_c, y_l = _experts(xg_c, xg_l, gates, w_gate_e[0], w_up_e[0], w_down_e[0])

    gf = final_g.reshape(1, D_MODEL)
    y_prompt = _combine(x1_c, mods, ctx_row, slot_t_c, y_c, gf, ctx_cap, MOE_CTX_PER_STEP, ctx_seq)
    y_sample = _combine(x1_l, mods, lat_row, slot_t_l, y_l, gf, lat_cap, 1, COMBINE_LAT_ROWS)

    state_k = k_c.reshape(nctx, 1, ctx_seq, N_HEADS, HEAD_DIM)
    state_v = v_c.reshape(nctx, 1, ctx_seq, N_HEADS, HEAD_DIM)
    return (y_prompt, y_sample, state_k, state_v)
```

```python
import functools

import jax
import jax.numpy as jnp
from jax import lax
from jax.experimental import pallas as pl
from jax.experimental.pallas import tpu as pltpu

F32 = jnp.float32
BF16 = jnp.bfloat16

D_MODEL = 1024
D_CONV = 512
CONV_WIDTH = 31
N_HEADS = 8
HEAD_DIM = 64
D_ATTN = N_HEADS * HEAD_DIM
GRID_W = 64
NA_ROWS = 8
NA_COLS = 16
N_EXPERTS = 16
D_EXPERT = 2048
EC_CAPACITY = 2
N_MOD = 6
EPS = 1e-6
D_PROJ_A = 2 * D_CONV + 3 * D_ATTN
D_IN = D_PROJ_A + 2 * D_MODEL
LANES = 128
REQ_PER_GROUP = LANES // N_EXPERTS
MASK_NEG = -1e30
VMEM_LIMIT = 56 * 1024 * 1024

_NT = (((1,), (1,)), ((), ()))


def _params(*sem):
    return pltpu.CompilerParams(dimension_semantics=sem, vmem_limit_bytes=VMEM_LIMIT)


def _dot(a, b):
    return jnp.dot(a, b, preferred_element_type=F32)


def _sigmoid(x):
    return 1.0 / (1.0 + jnp.exp(-x))


def _split_bf16(x):
    hi = x.astype(BF16)
    lo = (x - hi.astype(F32)).astype(BF16)
    return hi, lo


def _rms_mod(x, g, shift, scale):
    xn = x * lax.rsqrt(jnp.mean(x * x, axis=-1, keepdims=True) + EPS) * g
    return xn * (1.0 + scale) + shift


def _ada_kernel(cond_ref, w_ref, b_ref, o_ref):
    c = cond_ref[...]
    a = c * _sigmoid(c)
    ah, al = _split_bf16(a)
    wh, wl = _split_bf16(w_ref[...])
    o_ref[...] = _dot(ah, wh) + _dot(al, wh) + _dot(ah, wl) + b_ref[...]


def _ada(cond, w_ada, b_ada):
    n = w_ada.shape[1]
    bn = 1024
    return pl.pallas_call(
        _ada_kernel,
        grid=(n // bn,),
        in_specs=[
            pl.BlockSpec(cond.shape, lambda j: (0, 0)),
            pl.BlockSpec((D_MODEL, bn), lambda j: (0, j)),
            pl.BlockSpec((1, bn), lambda j: (0, j)),
        ],
        out_specs=pl.BlockSpec((cond.shape[0], bn), lambda j: (0, j)),
        out_shape=jax.ShapeDtypeStruct((cond.shape[0], n), F32),
        compiler_params=_params("arbitrary"),
        name="ada",
    )(cond, w_ada, b_ada)


CONV_PAD = 16
FRONT_ROWS = 256
CONV_ROWS = 128
SUBLANES = 8
FRONT_CTX_PER_STEP = 4


def _front_kernel(x_ref, mods_ref, g1_ref, w_ref, dww_ref, dwb_ref, lng_ref, lnb_ref,
                  q_ref, k_ref, v_ref, c_ref, *, seq):
    nreq = x_ref.shape[0]
    m = mods_ref[0]
    shift, scale = m[0:1], m[1:2]
    nchunk = seq // FRONT_ROWS
    per_chunk = FRONT_ROWS // CONV_ROWS
    zeros = jnp.zeros((CONV_PAD, D_CONV), F32)
    glu, stores = {}, []

    def proj(r, i):
        rows = slice(i * FRONT_ROWS, (i + 1) * FRONT_ROWS)
        x = x_ref[r, rows, :]
        h = _rms_mod(x, g1_ref[...], shift, scale).astype(BF16)
        u = _dot(h, w_ref[...])
        glu[r, i] = u[:, :D_CONV] * _sigmoid(u[:, D_CONV:2 * D_CONV])
        o = 2 * D_CONV
        stores.append((q_ref, r, rows, (u[:, o:o + D_ATTN] * (HEAD_DIM ** -0.5)).astype(q_ref.dtype)))
        stores.append((k_ref, r, rows, u[:, o + D_ATTN:o + 2 * D_ATTN].astype(k_ref.dtype)))
        stores.append((v_ref, r, rows, u[:, o + 2 * D_ATTN:o + 3 * D_ATTN].astype(v_ref.dtype)))

    def conv(r, j):
        i, o = j // per_chunk, (j % per_chunk) * CONV_ROWS
        g = glu[r, i]
        if o:
            top = g[o - CONV_PAD:o]
        else:
            top = glu[r, i - 1][FRONT_ROWS - CONV_PAD:] if i else zeros
        if o + CONV_ROWS < FRONT_ROWS:
            bottom = g[o + CONV_ROWS:o + CONV_ROWS + CONV_PAD]
        else:
            bottom = glu[r, i + 1][:CONV_PAD] if i + 1 < nchunk else zeros
        window = jnp.concatenate([top, g[o:o + CONV_ROWS], bottom], axis=0)
        cols = []
        for cb in range(D_CONV // LANES):
            cs = slice(cb * LANES, (cb + 1) * LANES)
            win = window[:, cs]
            acc = None
            for s in range(SUBLANES):
                nrow = CONV_ROWS + (SUBLANES if s else 0)
                z = None
                for a in range(2 * CONV_PAD // SUBLANES):
                    t = SUBLANES * a + s - (CONV_PAD - CONV_WIDTH // 2)
                    if 0 <= t < CONV_WIDTH:
                        term = win[SUBLANES * a:SUBLANES * a + nrow] * dww_ref[t:t + 1, cs]
                        z = term if z is None else z + term
                z = z[s:s + CONV_ROWS]
                acc = z if acc is None else acc + z
            cols.append(acc)
        y = jnp.concatenate(cols, axis=1) + dwb_ref[...]
        mu = jnp.mean(y, axis=-1, keepdims=True)
        yc = y - mu
        var = jnp.mean(yc * yc, axis=-1, keepdims=True)
        z = yc * lax.rsqrt(var + EPS) * lng_ref[...] + lnb_ref[...]
        stores.append((c_ref, r, slice(j * CONV_ROWS, (j + 1) * CONV_ROWS),
                       (z * _sigmoid(z)).astype(c_ref.dtype)))

    for r in range(nreq):
        for i in range(nchunk):
            proj(r, i)
    for r in range(nreq):
        for j in range(seq // CONV_ROWS):
            conv(r, j)
    for ref, r, rows, val in stores:
        ref[r, rows, :] = val


def _front(x, mods, mod_row, g1, w_a, dww, dwb, lng, lnb, kv_dtype, per_step):
    nreq, seq, _ = x.shape
    const2 = lambda b: (0, 0)
    tok = lambda b: (b, 0, 0)
    out_tok = pl.BlockSpec((per_step, seq, D_ATTN), tok)
    return pl.pallas_call(
        functools.partial(_front_kernel, seq=seq),
        grid=(nreq // per_step,),
        in_specs=[
            pl.BlockSpec((per_step, seq, D_MODEL), tok),
            pl.BlockSpec((1, N_MOD, D_MODEL), lambda b: (mod_row(b * per_step), 0, 0)),
            pl.BlockSpec((1, D_MODEL), const2),
            pl.BlockSpec((D_MODEL, D_PROJ_A), const2),
            pl.BlockSpec((CONV_WIDTH, D_CONV), const2),
            pl.BlockSpec((1, D_CONV), const2),
            pl.BlockSpec((1, D_CONV), const2),
            pl.BlockSpec((1, D_CONV), const2),
        ],
        out_specs=[out_tok, out_tok, out_tok, out_tok],
        out_shape=[
            jax.ShapeDtypeStruct((nreq, seq, D_ATTN), BF16),
            jax.ShapeDtypeStruct((nreq, seq, D_ATTN), kv_dtype),
            jax.ShapeDtypeStruct((nreq, seq, D_ATTN), kv_dtype),
            jax.ShapeDtypeStruct((nreq, seq, D_CONV), BF16),
        ],
        compiler_params=_params("arbitrary"),
        name="front",
    )(x, mods, g1, w_a, dww, dwb, lng, lnb)


def _pair_scores_q(q_pair):
    lane = lax.broadcasted_iota(jnp.int32, q_pair.shape, 1)
    zero = jnp.zeros_like(q_pair)
    return jnp.concatenate([jnp.where(lane < HEAD_DIM, q_pair, zero),
                            jnp.where(lane >= HEAD_DIM, q_pair, zero)], axis=0)


def _pair_merge(o2, t):
    lane = lax.broadcasted_iota(jnp.int32, (t, LANES), 1)
    return jnp.where(lane < HEAD_DIM, o2[:t], o2[t:])


CTX_ATTN_PER_STEP = 4


def _ctx_attn_kernel(q_ref, k_ref, v_ref, o_ref):
    nreq, seq, _ = q_ref.shape
    sls = [slice(p * LANES, (p + 1) * LANES) for p in range(N_HEADS // 2)]
    chains = [(r, p) for r in range(nreq) for p in range(N_HEADS // 2)]
    s = {(r, p): lax.dot_general(_pair_scores_q(q_ref[r, :, sls[p]]), k_ref[r, :, sls[p]].astype(BF16),
                                 _NT, preferred_element_type=F32) for r, p in chains}
    e = {c: jnp.exp(s[c] - jnp.max(s[c], axis=-1, keepdims=True)) for c in chains}
    den = {c: jnp.sum(e[c], axis=-1, keepdims=True) for c in chains}
    for r, p in chains:
        o2 = _dot(e[r, p].astype(BF16), v_ref[r, :, sls[p]].astype(BF16)) / den[r, p]
        o_ref[r, :, sls[p]] = _pair_merge(o2, seq).astype(o_ref.dtype)


def _ctx_attn(q, k, v):
    nreq, seq, _ = q.shape
    spec = pl.BlockSpec((CTX_ATTN_PER_STEP, seq, D_ATTN), lambda b: (b, 0, 0))
    return pl.pallas_call(
        _ctx_attn_kernel,
        grid=(nreq // CTX_ATTN_PER_STEP,),
        in_specs=[spec, spec, spec],
        out_specs=spec,
        out_shape=jax.ShapeDtypeStruct((nreq, seq, D_ATTN), BF16),
        compiler_params=_params("arbitrary"),
        name="ctx_attn",
    )(q, k, v)


N_DR = 2 * NA_ROWS - 1
N_DC = 2 * NA_COLS - 1


def _bias_kernel(rpb_ref, o_ref, t2_ref):
    h = pl.program_id(0)
    qi = lax.broadcasted_iota(jnp.int32, (GRID_W, LANES), 0)
    lane = lax.broadcasted_iota(jnp.int32, (GRID_W, LANES), 1)
    kc = lane & (GRID_W - 1)
    d = kc - qi
    cs = jnp.clip(qi - NA_COLS // 2, 0, GRID_W - NA_COLS)
    inside = (kc >= cs) & (kc < cs + NA_COLS)
    group = 5
    for dr0 in range(0, N_DR, group):
        ts = [jnp.full((GRID_W, LANES), MASK_NEG, F32)] * group
        for j in range(N_DC):
            hit = d == j - (NA_COLS - 1)
            ts = [jnp.where(hit, rpb_ref[h * (N_DR * N_DC) + (dr0 + g) * N_DC + j], ts[g])
                  for g in range(group)]
        for g in range(group):
            t2_ref[dr0 + g] = jnp.where(inside, ts[g], MASK_NEG)
    for o in range(NA_ROWS):
        for jj in range(NA_ROWS // 2):
            o_ref[0, o, :, jj * LANES:(jj + 1) * LANES] = jnp.where(
                lane < GRID_W, t2_ref[o + 2 * jj], t2_ref[o + 2 * jj + 1])


def _bias_table(rpb):
    return pl.pallas_call(
        _bias_kernel,
        grid=(N_HEADS,),
        in_specs=[pl.BlockSpec(memory_space=pltpu.SMEM)],
        out_specs=pl.BlockSpec((1, NA_ROWS, GRID_W, NA_ROWS * GRID_W), lambda h: (h, 0, 0, 0)),
        out_shape=jax.ShapeDtypeStruct((N_HEADS, NA_ROWS, GRID_W, NA_ROWS * GRID_W), F32),
        scratch_shapes=[pltpu.VMEM((N_DR, GRID_W, LANES), F32)],
        compiler_params=_params("arbitrary"),
        name="na_bias",
    )(rpb.reshape(-1))


NA_STEP_ROWS = 4


def _na_attn_kernel(q_ref, k_ref, v_ref, kct_ref, vct_ref, tab_ref, o_ref, kcb_ref, vcb_ref, *, rows):
    r = pl.program_id(1)

    @pl.when(r == 0)
    def _():
        kcb_ref[...] = kct_ref[0].astype(BF16)
        vcb_ref[...] = vct_ref[0].astype(BF16)

    nloc = NA_ROWS * GRID_W
    sls = [slice(p * LANES, (p + 1) * LANES) for p in range(N_HEADS // 2)]
    k0, off = [], []
    for i in range(NA_STEP_ROWS):
        qrow = r * NA_STEP_ROWS + i
        rs = jnp.clip(qrow - NA_ROWS // 2, 0, rows - NA_ROWS)
        k0.append(pl.multiple_of(rs * GRID_W, GRID_W))
        off.append(rs - qrow + (NA_ROWS - 1))
    chains = [(i, p) for i in range(NA_STEP_ROWS) for p in range(N_HEADS // 2)]
    qrows = [slice(i * GRID_W, (i + 1) * GRID_W) for i in range(NA_STEP_ROWS)]
    s_loc, s_ctx = {}, {}
    for i, p in chains:
        q2 = _pair_scores_q(q_ref[0, qrows[i], sls[p]])
        bias = jnp.concatenate([tab_ref[2 * p, off[i]], tab_ref[2 * p + 1, off[i]]], axis=0)
        kl = k_ref[0, pl.ds(k0[i], nloc), sls[p]]
        s_loc[i, p] = lax.dot_general(q2, kl, _NT, preferred_element_type=F32) + bias
        s_ctx[i, p] = _dot(q2, kcb_ref[sls[p], :])
    mx = {c: jnp.maximum(jnp.max(s_loc[c], axis=-1, keepdims=True),
                         jnp.max(s_ctx[c], axis=-1, keepdims=True)) for c in chains}
    e_loc = {c: jnp.exp(s_loc[c] - mx[c]) for c in chains}
    e_ctx = {c: jnp.exp(s_ctx[c] - mx[c]) for c in chains}
    den = {c: jnp.sum(e_loc[c], axis=-1, keepdims=True) + jnp.sum(e_ctx[c], axis=-1, keepdims=True)
           for c in chains}
    for i, p in chains:
        vl = v_ref[0, pl.ds(k0[i], nloc), sls[p]]
        o_ctx = lax.dot_general(e_ctx[i, p].astype(BF16), vcb_ref[sls[p], :], _NT,
                                preferred_element_type=F32)
        o2 = (_dot(e_loc[i, p].astype(BF16), vl) + o_ctx) / den[i, p]
        o_ref[0, qrows[i], sls[p]] = _pair_merge(o2, GRID_W).astype(o_ref.dtype)


def _na_attn(q, k, v, kc, vc, tab):
    nreq, seq, _ = q.shape
    rows = seq // GRID_W
    past = kc.shape[2]
    qspec = pl.BlockSpec((1, NA_STEP_ROWS * GRID_W, D_ATTN), lambda b, r: (b, r, 0))
    kvspec = pl.BlockSpec((1, seq, D_ATTN), lambda b, r: (b, 0, 0))
    cspec = pl.BlockSpec((1, D_ATTN, past), lambda b, r: (b, 0, 0))
    return pl.pallas_call(
        functools.partial(_na_attn_kernel, rows=rows),
        grid=(nreq, rows // NA_STEP_ROWS),
        in_specs=[qspec, kvspec, kvspec, cspec, cspec,
                  pl.BlockSpec(tab.shape, lambda b, r: (0, 0, 0, 0))],
        out_specs=qspec,
        out_shape=jax.ShapeDtypeStruct((nreq, seq, D_ATTN), BF16),
        scratch_shapes=[pltpu.VMEM((D_ATTN, past), BF16), pltpu.VMEM((D_ATTN, past), BF16)],
        compiler_params=_params("arbitrary", "arbitrary"),
        name="na_attn",
    )(q, k, v, kc, vc, tab)


MERGE_ROWS = 512
MERGE_SUB = 256


def _merge_kernel(x_ref, mods_ref, g1_ref, g2_ref, c_ref, a_ref, wg_ref, wpw_ref, wao_ref,
                  wout_ref, wr_ref, x1_ref, h2_ref, aff_ref):
    m = mods_ref[0]
    subs = range(MERGE_ROWS // MERGE_SUB)
    rows = [slice(i * MERGE_SUB, (i + 1) * MERGE_SUB) for i in subs]
    x = [x_ref[0, rows[i], :] for i in subs]
    h = [_rms_mod(x[i], g1_ref[...], m[0:1], m[1:2]).astype(BF16) for i in subs]
    gates = [_sigmoid(_dot(h[i], wg_ref[:, D_PROJ_A - D_IN // 2:])) for i in subs]
    conv_out = [_dot(c_ref[0, rows[i], :], wpw_ref[...]) for i in subs]
    attn_out = [_dot(a_ref[0, rows[i], :], wao_ref[...]) for i in subs]
    merged = [(gates[i][:, :D_MODEL] * conv_out[i] + gates[i][:, D_MODEL:] * attn_out[i]).astype(BF16)
              for i in subs]
    x1 = [x[i] + m[2:3] * _dot(merged[i], wout_ref[...]) for i in subs]
    h2 = [_rms_mod(x1[i], g2_ref[...], m[3:4], m[4:5]).astype(BF16) for i in subs]
    logits = [_dot(h2[i], wr_ref[...]) for i in subs]
    e = [jnp.exp(logits[i] - jnp.max(logits[i], axis=-1, keepdims=True)) for i in subs]
    for i in subs:
        x1_ref[0, rows[i], :] = x1[i]
        h2_ref[0, rows[i], :] = h2[i]
        aff_ref[0, rows[i], :] = e[i] / jnp.sum(e[i], axis=-1, keepdims=True)


def _merge(x, mods, mod_row, g1, g2, cact, heads, w_g, w_pw, w_ao, w_out, w_r):
    nreq, seq, _ = x.shape
    tm = MERGE_ROWS
    tok = lambda b, j: (b, j, 0)
    const2 = lambda b, j: (0, 0)
    return pl.pallas_call(
        _merge_kernel,
        grid=(nreq, seq // tm),
        in_specs=[
            pl.BlockSpec((1, tm, D_MODEL), tok),
            pl.BlockSpec((1, N_MOD, D_MODEL), lambda b, j: (mod_row(b), 0, 0)),
            pl.BlockSpec((1, D_MODEL), const2),
            pl.BlockSpec((1, D_MODEL), const2),
            pl.BlockSpec((1, tm, D_CONV), tok),
            pl.BlockSpec((1, tm, D_ATTN), tok),
            pl.BlockSpec((D_MODEL, D_IN // 2), lambda b, j: (0, 1)),
            pl.BlockSpec(w_pw.shape, const2),
            pl.BlockSpec(w_ao.shape, const2),
            pl.BlockSpec(w_out.shape, const2),
            pl.BlockSpec(w_r.shape, const2),
        ],
        out_specs=[
            pl.BlockSpec((1, tm, D_MODEL), tok),
            pl.BlockSpec((1, tm, D_MODEL), tok),
            pl.BlockSpec((1, tm, N_EXPERTS), tok),
        ],
        out_shape=[
            jax.ShapeDtypeStruct((nreq, seq, D_MODEL), F32),
            jax.ShapeDtypeStruct((nreq, seq, D_MODEL), BF16),
            jax.ShapeDtypeStruct((nreq, seq, N_EXPERTS), F32),
        ],
        compiler_params=_params("arbitrary", "arbitrary"),
        name="merge",
    )(x, mods, g1, g2, cact, heads, w_g, w_pw, w_ao, w_out, w_r)


def _route_kernel(aff_ref, slot_t_ref, slot_r_ref, aff_r_ref, *, cap):
    a = aff_ref[0]
    seq = a.shape[0]
    one, zero = jnp.ones_like(a), jnp.zeros_like(a)

    def search(i, thr):
        cand = thr | jnp.left_shift(jnp.int32(1), 30 - i)
        cnt = jnp.sum(jnp.where(a >= pltpu.bitcast(cand, F32), one, zero), axis=0, keepdims=True)
        return jnp.where(cnt >= cap, cand, thr)

    thr = lax.fori_loop(0, 31, search, jnp.zeros((1, LANES), jnp.int32))
    gt = a >= pltpu.bitcast(thr + 1, F32)
    eq = (a >= pltpu.bitcast(thr, F32)) & jnp.logical_not(gt)
    need = cap - jnp.sum(jnp.where(gt, one, zero), axis=0, keepdims=True)
    ti = lax.broadcasted_iota(jnp.int32, (seq, seq), 0)
    tj = lax.broadcasted_iota(jnp.int32, (seq, seq), 1)
    before = jnp.where(tj < ti, 1.0, 0.0).astype(BF16)
    eq_rank = _dot(before, jnp.where(eq, one, zero).astype(BF16))
    sel = gt | (eq & (eq_rank < need))
    pos = _dot(before, jnp.where(sel, one, zero).astype(BF16))
    slot = jnp.where(sel, pos, -1.0)
    slot_t_ref[0] = slot
    slot_r_ref[0] = slot.T
    aff_r_ref[0] = a.T


def _route(aff_t, cap):
    ngroup, seq, _ = aff_t.shape
    tspec = pl.BlockSpec((1, seq, LANES), lambda g: (g, 0, 0))
    rspec = pl.BlockSpec((1, LANES, seq), lambda g: (g, 0, 0))
    return pl.pallas_call(
        functools.partial(_route_kernel, cap=cap),
        grid=(ngroup,),
        in_specs=[tspec],
        out_specs=[tspec, rspec, rspec],
        out_shape=[
            jax.ShapeDtypeStruct((ngroup, seq, LANES), F32),
            jax.ShapeDtypeStruct((ngroup, LANES, seq), F32),
            jax.ShapeDtypeStruct((ngroup, LANES, seq), F32),
        ],
        compiler_params=_params("arbitrary"),
        name="route",
    )(aff_t)


GATHER_ROWS = 512


def _gather_kernel(slot_ref, aff_ref, h_ref, x_ref, g_ref, *, cap):
    per_step, seq, _ = h_ref.shape
    nslot = N_EXPERTS * cap
    sub = lax.broadcasted_iota(jnp.int32, (cap, seq), 0).astype(F32)
    one, zero = jnp.ones((cap, seq), F32), jnp.zeros((cap, seq), F32)
    ei = lax.broadcasted_iota(jnp.int32, (N_EXPERTS, nslot), 0)
    si = lax.broadcasted_iota(jnp.int32, (N_EXPERTS, nslot), 1)
    own = (si >= ei * cap) & (si < (ei + 1) * cap)
    per_chunk = GATHER_ROWS // cap
    for r in range(per_step):
        h = h_ref[r]
        a = aff_ref[0, r * N_EXPERTS:(r + 1) * N_EXPERTS, :]
        a_hi = a.astype(BF16)
        r1 = a - a_hi.astype(F32)
        a_mid = r1.astype(BF16)
        a_lo = (r1 - a_mid.astype(F32)).astype(BF16)
        a3 = jnp.concatenate([a_hi, a_mid, a_lo], axis=0)
        g3 = []
        for c in range(nslot // GATHER_ROWS):
            hot = []
            for e in range(c * per_chunk, (c + 1) * per_chunk):
                row = jnp.broadcast_to(slot_ref[0, r * N_EXPERTS + e:r * N_EXPERTS + e + 1, :], (cap, seq))
                hot.append(jnp.where(row == sub, one, zero).astype(BF16))
            hot = jnp.concatenate(hot, axis=0)
            x = _dot(hot, h).astype(x_ref.dtype)
            for i in range(per_chunk):
                x_ref[c * per_chunk + i, r * cap:(r + 1) * cap, :] = x[i * cap:(i + 1) * cap]
            g3.append(lax.dot_general(a3, hot, _NT, preferred_element_type=F32))
        g3 = jnp.concatenate(g3, axis=1)
        gfull = g3[0:N_EXPERTS] + g3[N_EXPERTS:2 * N_EXPERTS] + g3[2 * N_EXPERTS:3 * N_EXPERTS]
        g_ref[r] = jnp.sum(jnp.where(own, gfull, 0.0), axis=0, keepdims=True)


def _gather(slot_r, aff_r, h2, cap, per_step):
    nreq, seq, _ = h2.shape
    nslot = N_EXPERTS * cap
    steps_per_group = REQ_PER_GROUP // per_step
    grp = lambda s: (s // steps_per_group, s % steps_per_group, 0)
    x, g = pl.pallas_call(
        functools.partial(_gather_kernel, cap=cap),
        grid=(nreq // per_step,),
        in_specs=[
            pl.BlockSpec((1, per_step * N_EXPERTS, seq), grp),
            pl.BlockSpec((1, per_step * N_EXPERTS, seq), grp),
            pl.BlockSpec((per_step, seq, D_MODEL), lambda s: (s, 0, 0)),
        ],
        out_specs=[
            pl.BlockSpec((N_EXPERTS, per_step * cap, D_MODEL), lambda s: (0, s, 0)),
            pl.BlockSpec((per_step, 1, nslot), lambda s: (s, 0, 0)),
        ],
        out_shape=[
            jax.ShapeDtypeStruct((N_EXPERTS, nreq * cap, D_MODEL), BF16),
            jax.ShapeDtypeStruct((nreq, 1, nslot), F32),
        ],
        compiler_params=_params("arbitrary"),
        name="gather",
    )(slot_r, aff_r, h2)
    return x, g


EXPERT_FCHUNK = 1024
EXPERT_ROWS = 512
GATE_ROWS = 256


def _expert_kernel(xc_ref, xl_ref, g_ref, wg_ref, wu_ref, wd_ref, yc_ref, yl_ref,
                   acc_ref):
    e = pl.program_id(0)
    f = pl.program_id(1)
    nf = pl.num_programs(1)
    half = xc_ref.shape[1]

    @pl.when((e == 0) & (f == 0))
    def _():
        acc_ref[...] = jnp.zeros_like(acc_ref)

    ri = lax.broadcasted_iota(jnp.int32, (GATE_ROWS, GATE_ROWS), 0)
    ci = lax.broadcasted_iota(jnp.int32, (GATE_ROWS, GATE_ROWS), 1)
    for part, (x_ref, y_ref) in enumerate(((xc_ref, yc_ref), (xl_ref, yl_ref))):
        for i in range(half // EXPERT_ROWS):
            x = x_ref[0, i * EXPERT_ROWS:(i + 1) * EXPERT_ROWS, :]
            gq = _dot(x, wg_ref[0].astype(BF16))
            uq = _dot(x, wu_ref[0].astype(BF16))
            hid = (gq * _sigmoid(gq) * uq).astype(BF16)
            r0 = part * half + i * EXPERT_ROWS
            prev = jnp.where(f == 0, 0.0, acc_ref[r0:r0 + EXPERT_ROWS, :])
            acc = prev + _dot(hid, wd_ref[0].astype(BF16))
            acc_ref[r0:r0 + EXPERT_ROWS, :] = acc
            for j in range(EXPERT_ROWS // GATE_ROWS):
                t = r0 // GATE_ROWS + j
                grow = jnp.broadcast_to(g_ref[0, t:t + 1, :], (GATE_ROWS, GATE_ROWS))
                gcol = jnp.sum(jnp.where(ri == ci, grow, 0.0), axis=-1, keepdims=True)
                y = acc[j * GATE_ROWS:(j + 1) * GATE_ROWS] * gcol
                y_ref[0, i * EXPERT_ROWS + j * GATE_ROWS:i * EXPERT_ROWS + (j + 1) * GATE_ROWS, :] = (
                    y.astype(y_ref.dtype))


def _experts(xc, xl, g, w_gate, w_up, w_down):
    ne, half, _ = xc.shape
    fc = EXPERT_FCHUNK
    xspec = pl.BlockSpec((1, half, D_MODEL), lambda e, f: (e, 0, 0))
    return pl.pallas_call(
        _expert_kernel,
        grid=(ne, D_EXPERT // fc),
        in_specs=[
            xspec, xspec,
            pl.BlockSpec((1,) + g.shape[1:], lambda e, f: (e, 0, 0)),
            pl.BlockSpec((1, D_MODEL, fc), lambda e, f: (e, 0, f)),
            pl.BlockSpec((1, D_MODEL, fc), lambda e, f: (e, 0, f)),
            pl.BlockSpec((1, fc, D_MODEL), lambda e, f: (e, f, 0)),
        ],
        out_specs=[xspec, xspec],
        out_shape=[jax.ShapeDtypeStruct(xc.shape, BF16), jax.ShapeDtypeStruct(xl.shape, BF16)],
        scratch_shapes=[pltpu.VMEM((2 * half, D_MODEL), F32)],
        compiler_params=_params("arbitrary", "arbitrary"),
        name="experts",
    )(xc, xl, g, w_gate, w_up, w_down)


COMBINE_ROWS = 256
COMBINE_LAT_ROWS = 512
MOE_CTX_PER_STEP = 4


def _combine_kernel(x1_ref, mods_ref, slot_ref, y_ref, gf_ref, o_ref, *, cap):
    per_step, tm, _ = x1_ref.shape
    nslot = N_EXPERTS * cap
    li = lax.broadcasted_iota(jnp.int32, (LANES, nslot), 0)
    si = lax.broadcasted_iota(jnp.int32, (LANES, nslot), 1)
    want = (lax.broadcasted_iota(jnp.int32, (COMBINE_ROWS, nslot), 1) & (cap - 1)).astype(F32)
    chains = [(r, t) for r in range(per_step) for t in range(tm // COMBINE_ROWS)]
    rows = {t: slice(t * COMBINE_ROWS, (t + 1) * COMBINE_ROWS) for _, t in chains}
    expand = {}
    for r in range(per_step):
        lane0 = ((pl.program_id(0) * per_step + r) % REQ_PER_GROUP) * N_EXPERTS
        expand[r] = jnp.where(li == lane0 + si // cap, 1.0, 0.0).astype(BF16)
    slot_exp = {(r, t): _dot(slot_ref[0, rows[t], :].astype(BF16), expand[r]) for r, t in chains}
    scat = {c: jnp.where(slot_exp[c] == want, 1.0, 0.0).astype(BF16) for c in chains}
    moe = {(r, t): _dot(scat[r, t], y_ref[:, r * cap:(r + 1) * cap, :].reshape(nslot, D_MODEL))
           for r, t in chains}
    for r, t in chains:
        x2 = x1_ref[r, rows[t], :] + mods_ref[0][5:6] * moe[r, t]
        o_ref[r, rows[t], :] = (x2 * lax.rsqrt(jnp.mean(x2 * x2, axis=-1, keepdims=True) + EPS)
                                * gf_ref[...])


def _combine(x1, mods, mod_row, slot_t, y, gf, cap, per_step, tm):
    nreq, seq, _ = x1.shape
    assert per_step == 1 or seq == tm
    tok = lambda s, j: (s, j, 0)
    return pl.pallas_call(
        functools.partial(_combine_kernel, cap=cap),
        grid=(nreq // per_step, seq // tm),
        in_specs=[
            pl.BlockSpec((per_step, tm, D_MODEL), tok),
            pl.BlockSpec((1, N_MOD, D_MODEL), lambda s, j: (mod_row(s * per_step), 0, 0)),
            pl.BlockSpec((1, tm, LANES), lambda s, j: (s * per_step // REQ_PER_GROUP, j, 0)),
            pl.BlockSpec((N_EXPERTS, per_step * cap, D_MODEL), lambda s, j: (0, s, 0)),
            pl.BlockSpec((1, D_MODEL), lambda s, j: (0, 0)),
        ],
        out_specs=pl.BlockSpec((per_step, tm, D_MODEL), tok),
        out_shape=jax.ShapeDtypeStruct(x1.shape, F32),
        compiler_params=_params("arbitrary", "arbitrary"),
        name="combine",
    )(x1, mods, slot_t, y, gf)


def _group_lanes(aff):
    nreq, seq, ne = aff.shape
    a = aff.reshape(nreq // REQ_PER_GROUP, REQ_PER_GROUP, seq, ne)
    return jnp.transpose(a, (0, 2, 1, 3)).reshape(nreq // REQ_PER_GROUP, seq, REQ_PER_GROUP * ne)


def kernel(x_prompt, x_sample, cache_ctx_k, cache_ctx_v, c, c_ctx, w_ada, b_ada, norm1_g, w_in,
           conv_dw_w, conv_dw_b, conv_ln_g, conv_ln_b, w_conv_pw, na_rpb, w_attn_o, w_out,
           norm2_g, w_router, w_gate_e, w_up_e, w_down_e, final_g):
    assert w_ada.shape[0] == 1, "single trunk layer"
    nctx, ctx_seq, _ = x_prompt.shape
    nlat, lat_seq, _ = x_sample.shape
    ctx_cap = EC_CAPACITY * ctx_seq // N_EXPERTS
    lat_cap = EC_CAPACITY * lat_seq // N_EXPERTS
    assert nctx * ctx_cap == nlat * lat_cap

    cond = jnp.zeros((2 * nlat, D_MODEL), F32).at[:nlat].set(c).at[nlat].set(c_ctx)
    mods = _ada(cond, w_ada[0], b_ada).reshape(2 * nlat, N_MOD, D_MODEL)
    ctx_row = lambda b: nlat
    lat_row = lambda b: b

    w_a = w_g = w_in[0].astype(BF16)
    w_pw = w_conv_pw[0].astype(BF16)
    w_ao = w_attn_o[0].astype(BF16)
    w_o = w_out[0].astype(BF16)
    w_r = w_router[0].astype(BF16)
    conv_args = (conv_dw_w[0], conv_dw_b, conv_ln_g, conv_ln_b)

    q_c, k_c, v_c, cact_c = _front(x_prompt, mods, ctx_row, norm1_g, w_a, *conv_args, F32,
                                   FRONT_CTX_PER_STEP)
    q_l, k_l, v_l, cact_l = _front(x_sample, mods, lat_row, norm1_g, w_a, *conv_args, BF16, 1)

    heads_c = _ctx_attn(q_c, k_c, v_c)
    tab = _bias_table(na_rpb[0])
    past = cache_ctx_k.shape[2]
    feat_major = lambda t: jnp.transpose(t[:, 0], (0, 2, 3, 1)).reshape(nlat, D_ATTN, past)
    heads_l = _na_attn(q_l, k_l, v_l, feat_major(cache_ctx_k), feat_major(cache_ctx_v), tab)

    merge_w = (w_g, w_pw, w_ao, w_o, w_r)
    flat = lambda t: t.reshape(1, nctx * ctx_seq, t.shape[-1])
    x1_c, h2_c, aff_c = _merge(flat(x_prompt), mods, ctx_row, norm1_g, norm2_g, flat(cact_c),
                               flat(heads_c), *merge_w)
    x1_c, h2_c, aff_c = (t.reshape(nctx, ctx_seq, t.shape[-1]) for t in (x1_c, h2_c, aff_c))
    x1_l, h2_l, aff_l = _merge(x_sample, mods, lat_row, norm1_g, norm2_g, cact_l, heads_l, *merge_w)

    slot_t_c, slot_r_c, aff_r_c = _route(_group_lanes(aff_c), ctx_cap)
    slot_t_l, slot_r_l, aff_r_l = _route(_group_lanes(aff_l), lat_cap)

    xg_c, g_c = _gather(slot_r_c, aff_r_c, h2_c, ctx_cap, MOE_CTX_PER_STEP)
    xg_l, g_l = _gather(slot_r_l, aff_r_l, h2_l, lat_cap, 1)

    def gate_rows(g, nreq, cap):
        g = jnp.transpose(g.reshape(nreq, N_EXPERTS, cap), (1, 0, 2))
        return g.reshape(N_EXPERTS, nreq * cap // GATE_ROWS, GATE_ROWS)

    gates = jnp.concatenate([gate_rows(g_c, nctx, ctx_cap), gate_rows(g_l, nlat, lat_cap)], axis=1)
    y_c, y_l = _experts(xg_c, xg_l, gates, w_gate_e[0], w_up_e[0], w_down_e[0])

    gf = final_g.reshape(1, D_MODEL)
    y_prompt = _combine(x1_c, mods, ctx_row, slot_t_c, y_c, gf, ctx_cap, MOE_CTX_PER_STEP, ctx_seq)
    y_sample = _combine(x1_l, mods, lat_row, slot_t_l, y_l, gf, lat_cap, 1, COMBINE_LAT_ROWS)

    state_k = k_c.reshape(nctx, 1, ctx_seq, N_HEADS, HEAD_DIM)
    state_v = v_c.reshape(nctx, 1, ctx_seq, N_HEADS, HEAD_DIM)
    return (y_prompt, y_sample, state_k, state_v)
```

```python
import functools

import jax
import jax.numpy as jnp
from jax import lax
from jax.experimental import pallas as pl
from jax.experimental.pallas import tpu as pltpu

F32 = jnp.float32
BF16 = jnp.bfloat16

D_MODEL = 1024
D_CONV = 512
CONV_WIDTH = 31
N_HEADS = 8
HEAD_DIM = 64
D_ATTN = N_HEADS * HEAD_DIM
GRID_W = 64
NA_ROWS = 8
NA_COLS = 16
N_EXPERTS = 16
D_EXPERT = 2048
EC_CAPACITY = 2
N_MOD = 6
EPS = 1e-6
D_PROJ_A = 2 * D_CONV + 3 * D_ATTN
D_IN = D_PROJ_A + 2 * D_MODEL
LANES = 128
REQ_PER_GROUP = LANES // N_EXPERTS
MASK_NEG = -1e30
VMEM_LIMIT = 56 * 1024 * 1024

_NT = (((1,), (1,)), ((), ()))


def _params(*sem):
    return pltpu.CompilerParams(dimension_semantics=sem, vmem_limit_bytes=VMEM_LIMIT)


def _dot(a, b):
    return jnp.dot(a, b, preferred_element_type=F32)


def _sigmoid(x):
    return 1.0 / (1.0 + jnp.exp(-x))


def _split_bf16(x):
    hi = x.astype(BF16)
    lo = (x - hi.astype(F32)).astype(BF16)
    return hi, lo


def _rms_mod(x, g, shift, scale):
    xn = x * lax.rsqrt(jnp.mean(x * x, axis=-1, keepdims=True) + EPS) * g
    return xn * (1.0 + scale) + shift


ADA_COLS = 512


def _ada_kernel(cond_ref, w_ref, b_ref, o_ref):
    c = cond_ref[...]
    a = c * _sigmoid(c)
    ah, al = _split_bf16(a)
    wh, wl = _split_bf16(w_ref[...])
    o_ref[...] = _dot(ah, wh) + _dot(al, wh) + _dot(ah, wl) + b_ref[...]


def _ada(cond, w_ada, b_ada):
    n = w_ada.shape[1]
    bn = ADA_COLS
    return pl.pallas_call(
        _ada_kernel,
        grid=(n // bn,),
        in_specs=[
            pl.BlockSpec(cond.shape, lambda j: (0, 0)),
            pl.BlockSpec((D_MODEL, bn), lambda j: (0, j)),
            pl.BlockSpec((1, bn), lambda j: (0, j)),
        ],
        out_specs=pl.BlockSpec((cond.shape[0], bn), lambda j: (0, j)),
        out_shape=jax.ShapeDtypeStruct((cond.shape[0], n), F32),
        compiler_params=_params("arbitrary"),
        name="ada",
    )(cond, w_ada, b_ada)


CONV_PAD = 16
FRONT_ROWS = 256
CONV_ROWS = 128
SUBLANES = 8
FRONT_CTX_PER_STEP = 4


def _front_kernel(x_ref, mods_ref, g1_ref, w_ref, dww_ref, dwb_ref, lng_ref, lnb_ref,
                  q_ref, k_ref, v_ref, c_ref, *, seq):
    nreq = x_ref.shape[0]
    m = mods_ref[0]
    shift, scale = m[0:1], m[1:2]
    nchunk = seq // FRONT_ROWS
    per_chunk = FRONT_ROWS // CONV_ROWS
    zeros = jnp.zeros((CONV_PAD, D_CONV), F32)
    glu, stores = {}, []

    def proj(r, i):
        rows = slice(i * FRONT_ROWS, (i + 1) * FRONT_ROWS)
        x = x_ref[r, rows, :]
        h = _rms_mod(x, g1_ref[...], shift, scale).astype(BF16)
        u = _dot(h, w_ref[...])
        glu[r, i] = u[:, :D_CONV] * _sigmoid(u[:, D_CONV:2 * D_CONV])
        o = 2 * D_CONV
        stores.append((q_ref, r, rows, (u[:, o:o + D_ATTN] * (HEAD_DIM ** -0.5)).astype(q_ref.dtype)))
        stores.append((k_ref, r, rows, u[:, o + D_ATTN:o + 2 * D_ATTN].astype(k_ref.dtype)))
        stores.append((v_ref, r, rows, u[:, o + 2 * D_ATTN:o + 3 * D_ATTN].astype(v_ref.dtype)))

    def conv(r, j):
        i, o = j // per_chunk, (j % per_chunk) * CONV_ROWS
        g = glu[r, i]
        if o:
            top = g[o - CONV_PAD:o]
        else:
            top = glu[r, i - 1][FRONT_ROWS - CONV_PAD:] if i else zeros
        if o + CONV_ROWS < FRONT_ROWS:
            bottom = g[o + CONV_ROWS:o + CONV_ROWS + CONV_PAD]
        else:
            bottom = glu[r, i + 1][:CONV_PAD] if i + 1 < nchunk else zeros
        window = jnp.concatenate([top, g[o:o + CONV_ROWS], bottom], axis=0)
        cols = []
        for cb in range(D_CONV // LANES):
            cs = slice(cb * LANES, (cb + 1) * LANES)
            win = window[:, cs]
            acc = None
            for s in range(SUBLANES):
                nrow = CONV_ROWS + (SUBLANES if s else 0)
                z = None
                for a in range(2 * CONV_PAD // SUBLANES):
                    t = SUBLANES * a + s - (CONV_PAD - CONV_WIDTH // 2)
                    if 0 <= t < CONV_WIDTH:
                        term = win[SUBLANES * a:SUBLANES * a + nrow] * dww_ref[t:t + 1, cs]
                        z = term if z is None else z + term
                z = z[s:s + CONV_ROWS]
                acc = z if acc is None else acc + z
            cols.append(acc)
        y = jnp.concatenate(cols, axis=1) + dwb_ref[...]
        mu = jnp.mean(y, axis=-1, keepdims=True)
        yc = y - mu
        var = jnp.mean(yc * yc, axis=-1, keepdims=True)
        z = yc * lax.rsqrt(var + EPS) * lng_ref[...] + lnb_ref[...]
        stores.append((c_ref, r, slice(j * CONV_ROWS, (j + 1) * CONV_ROWS),
                       (z * _sigmoid(z)).astype(c_ref.dtype)))

    for r in range(nreq):
        for i in range(nchunk):
            proj(r, i)
    for r in range(nreq):
        for j in range(seq // CONV_ROWS):
            conv(r, j)
    for ref, r, rows, val in stores:
        ref[r, rows, :] = val


def _front(x, mods, mod_row, g1, w_a, dww, dwb, lng, lnb, kv_dtype, per_step):
    nreq, seq, _ = x.shape
    const2 = lambda b: (0, 0)
    tok = lambda b: (b, 0, 0)
    out_tok = pl.BlockSpec((per_step, seq, D_ATTN), tok)
    return pl.pallas_call(
        functools.partial(_front_kernel, seq=seq),
        grid=(nreq // per_step,),
        in_specs=[
            pl.BlockSpec((per_step, seq, D_MODEL), tok),
            pl.BlockSpec((1, N_MOD, D_MODEL), lambda b: (mod_row(b * per_step), 0, 0)),
            pl.BlockSpec((1, D_MODEL), const2),
            pl.BlockSpec((D_MODEL, D_PROJ_A), const2),
            pl.BlockSpec((CONV_WIDTH, D_CONV), const2),
            pl.BlockSpec((1, D_CONV), const2),
            pl.BlockSpec((1, D_CONV), const2),
            pl.BlockSpec((1, D_CONV), const2),
        ],
        out_specs=[out_tok, out_tok, out_tok, out_tok],
        out_shape=[
            jax.ShapeDtypeStruct((nreq, seq, D_ATTN), BF16),
            jax.ShapeDtypeStruct((nreq, seq, D_ATTN), kv_dtype),
            jax.ShapeDtypeStruct((nreq, seq, D_ATTN), kv_dtype),
            jax.ShapeDtypeStruct((nreq, seq, D_CONV), BF16),
        ],
        compiler_params=_params("arbitrary"),
        name="front",
    )(x, mods, g1, w_a, dww, dwb, lng, lnb)


def _pair_scores_q(q_pair):
    lane = lax.broadcasted_iota(jnp.int32, q_pair.shape, 1)
    zero = jnp.zeros_like(q_pair)
    return jnp.concatenate([jnp.where(lane < HEAD_DIM, q_pair, zero),
                            jnp.where(lane >= HEAD_DIM, q_pair, zero)], axis=0)


def _pair_merge(o2, t):
    lane = lax.broadcasted_iota(jnp.int32, (t, LANES), 1)
    return jnp.where(lane < HEAD_DIM, o2[:t], o2[t:])


CTX_ATTN_PER_STEP = 4


def _ctx_attn_kernel(q_ref, k_ref, v_ref, o_ref):
    nreq, seq, _ = q_ref.shape
    sls = [slice(p * LANES, (p + 1) * LANES) for p in range(N_HEADS // 2)]
    chains = [(r, p) for r in range(nreq) for p in range(N_HEADS // 2)]
    s = {(r, p): lax.dot_general(_pair_scores_q(q_ref[r, :, sls[p]]), k_ref[r, :, sls[p]].astype(BF16),
                                 _NT, preferred_element_type=F32) for r, p in chains}
    e = {c: jnp.exp(s[c] - jnp.max(s[c], axis=-1, keepdims=True)) for c in chains}
    den = {c: jnp.sum(e[c], axis=-1, keepdims=True) for c in chains}
    for r, p in chains:
        o2 = _dot(e[r, p].astype(BF16), v_ref[r, :, sls[p]].astype(BF16)) / den[r, p]
        o_ref[r, :, sls[p]] = _pair_merge(o2, seq).astype(o_ref.dtype)


def _ctx_attn(q, k, v):
    nreq, seq, _ = q.shape
    spec = pl.BlockSpec((CTX_ATTN_PER_STEP, seq, D_ATTN), lambda b: (b, 0, 0))
    return pl.pallas_call(
        _ctx_attn_kernel,
        grid=(nreq // CTX_ATTN_PER_STEP,),
        in_specs=[spec, spec, spec],
        out_specs=spec,
        out_shape=jax.ShapeDtypeStruct((nreq, seq, D_ATTN), BF16),
        compiler_params=_params("arbitrary"),
        name="ctx_attn",
    )(q, k, v)


N_DR = 2 * NA_ROWS - 1
N_DC = 2 * NA_COLS - 1


def _bias_kernel(rpb_ref, o_ref, t2_ref):
    h = pl.program_id(0)
    qi = lax.broadcasted_iota(jnp.int32, (GRID_W, LANES), 0)
    lane = lax.broadcasted_iota(jnp.int32, (GRID_W, LANES), 1)
    kc = lane & (GRID_W - 1)
    d = kc - qi
    cs = jnp.clip(qi - NA_COLS // 2, 0, GRID_W - NA_COLS)
    inside = (kc >= cs) & (kc < cs + NA_COLS)
    group = 5
    for dr0 in range(0, N_DR, group):
        ts = [jnp.full((GRID_W, LANES), MASK_NEG, F32)] * group
        for j in range(N_DC):
            hit = d == j - (NA_COLS - 1)
            ts = [jnp.where(hit, rpb_ref[h * (N_DR * N_DC) + (dr0 + g) * N_DC + j], ts[g])
                  for g in range(group)]
        for g in range(group):
            t2_ref[dr0 + g] = jnp.where(inside, ts[g], MASK_NEG)
    for o in range(NA_ROWS):
        for jj in range(NA_ROWS // 2):
            o_ref[0, o, :, jj * LANES:(jj + 1) * LANES] = jnp.where(
                lane < GRID_W, t2_ref[o + 2 * jj], t2_ref[o + 2 * jj + 1])


def _bias_table(rpb):
    return pl.pallas_call(
        _bias_kernel,
        grid=(N_HEADS,),
        in_specs=[pl.BlockSpec(memory_space=pltpu.SMEM)],
        out_specs=pl.BlockSpec((1, NA_ROWS, GRID_W, NA_ROWS * GRID_W), lambda h: (h, 0, 0, 0)),
        out_shape=jax.ShapeDtypeStruct((N_HEADS, NA_ROWS, GRID_W, NA_ROWS * GRID_W), F32),
        scratch_shapes=[pltpu.VMEM((N_DR, GRID_W, LANES), F32)],
        compiler_params=_params("arbitrary"),
        name="na_bias",
    )(rpb.reshape(-1))


NA_STEP_ROWS = 4


def _na_attn_kernel(q_ref, k_ref, v_ref, kct_ref, vct_ref, tab_ref, o_ref, kcb_ref, vcb_ref, *, rows):
    r = pl.program_id(1)

    @pl.when(r == 0)
    def _():
        kcb_ref[...] = kct_ref[0].astype(BF16)
        vcb_ref[...] = vct_ref[0].astype(BF16)

    nloc = NA_ROWS * GRID_W
    sls = [slice(p * LANES, (p + 1) * LANES) for p in range(N_HEADS // 2)]
    k0, off = [], []
    for i in range(NA_STEP_ROWS):
        qrow = r * NA_STEP_ROWS + i
        rs = jnp.clip(qrow - NA_ROWS // 2, 0, rows - NA_ROWS)
        k0.append(pl.multiple_of(rs * GRID_W, GRID_W))
        off.append(rs - qrow + (NA_ROWS - 1))
    chains = [(i, p) for i in range(NA_STEP_ROWS) for p in range(N_HEADS // 2)]
    qrows = [slice(i * GRID_W, (i + 1) * GRID_W) for i in range(NA_STEP_ROWS)]
    s_loc, s_ctx = {}, {}
    for i, p in chains:
        q2 = _pair_scores_q(q_ref[0, qrows[i], sls[p]])
        bias = jnp.concatenate([tab_ref[2 * p, off[i]], tab_ref[2 * p + 1, off[i]]], axis=0)
        kl = k_ref[0, pl.ds(k0[i], nloc), sls[p]]
        s_loc[i, p] = lax.dot_general(q2, kl, _NT, preferred_element_type=F32) + bias
        s_ctx[i, p] = _dot(q2, kcb_ref[sls[p], :])
    mx = {c: jnp.maximum(jnp.max(s_loc[c], axis=-1, keepdims=True),
                         jnp.max(s_ctx[c], axis=-1, keepdims=True)) for c in chains}
    e_loc = {c: jnp.exp(s_loc[c] - mx[c]) for c in chains}
    e_ctx = {c: jnp.exp(s_ctx[c] - mx[c]) for c in chains}
    den = {c: jnp.sum(e_loc[c], axis=-1, keepdims=True) + jnp.sum(e_ctx[c], axis=-1, keepdims=True)
           for c in chains}
    for i, p in chains:
        vl = v_ref[0, pl.ds(k0[i], nloc), sls[p]]
        o_ctx = lax.dot_general(e_ctx[i, p].astype(BF16), vcb_ref[sls[p], :], _NT,
                                preferred_element_type=F32)
        o2 = (_dot(e_loc[i, p].astype(BF16), vl) + o_ctx) / den[i, p]
        o_ref[0, qrows[i], sls[p]] = _pair_merge(o2, GRID_W).astype(o_ref.dtype)


def _na_attn(q, k, v, kc, vc, tab):
    nreq, seq, _ = q.shape
    rows = seq // GRID_W
    past = kc.shape[2]
    qspec = pl.BlockSpec((1, NA_STEP_ROWS * GRID_W, D_ATTN), lambda b, r: (b, r, 0))
    kvspec = pl.BlockSpec((1, seq, D_ATTN), lambda b, r: (b, 0, 0))
    cspec = pl.BlockSpec((1, D_ATTN, past), lambda b, r: (b, 0, 0))
    return pl.pallas_call(
        functools.partial(_na_attn_kernel, rows=rows),
        grid=(nreq, rows // NA_STEP_ROWS),
        in_specs=[qspec, kvspec, kvspec, cspec, cspec,
                  pl.BlockSpec(tab.shape, lambda b, r: (0, 0, 0, 0))],
        out_specs=qspec,
        out_shape=jax.ShapeDtypeStruct((nreq, seq, D_ATTN), BF16),
        scratch_shapes=[pltpu.VMEM((D_ATTN, past), BF16), pltpu.VMEM((D_ATTN, past), BF16)],
        compiler_params=_params("arbitrary", "arbitrary"),
        name="na_attn",
    )(q, k, v, kc, vc, tab)


MERGE_ROWS = 512
MERGE_SUB = 256


def _merge_kernel(x_ref, mods_ref, g1_ref, g2_ref, c_ref, a_ref, wg_ref, wpw_ref, wao_ref,
                  wout_ref, wr_ref, x1_ref, h2_ref, aff_ref):
    m = mods_ref[0]
    subs = range(MERGE_ROWS // MERGE_SUB)
    rows = [slice(i * MERGE_SUB, (i + 1) * MERGE_SUB) for i in subs]
    x = [x_ref[0, rows[i], :] for i in subs]
    h = [_rms_mod(x[i], g1_ref[...], m[0:1], m[1:2]).astype(BF16) for i in subs]
    gates = [_sigmoid(_dot(h[i], wg_ref[:, D_PROJ_A - D_IN // 2:])) for i in subs]
    conv_out = [_dot(c_ref[0, rows[i], :], wpw_ref[...]) for i in subs]
    attn_out = [_dot(a_ref[0, rows[i], :], wao_ref[...]) for i in subs]
    merged = [(gates[i][:, :D_MODEL] * conv_out[i] + gates[i][:, D_MODEL:] * attn_out[i]).astype(BF16)
              for i in subs]
    x1 = [x[i] + m[2:3] * _dot(merged[i], wout_ref[...]) for i in subs]
    h2 = [_rms_mod(x1[i], g2_ref[...], m[3:4], m[4:5]).astype(BF16) for i in subs]
    logits = [_dot(h2[i], wr_ref[...]) for i in subs]
    e = [jnp.exp(logits[i] - jnp.max(logits[i], axis=-1, keepdims=True)) for i in subs]
    for i in subs:
        x1_ref[0, rows[i], :] = x1[i]
        h2_ref[0, rows[i], :] = h2[i]
        aff_ref[0, rows[i], :] = e[i] / jnp.sum(e[i], axis=-1, keepdims=True)


def _merge(x, mods, mod_row, g1, g2, cact, heads, w_g, w_pw, w_ao, w_out, w_r):
    nreq, seq, _ = x.shape
    tm = MERGE_ROWS
    tok = lambda b, j: (b, j, 0)
    const2 = lambda b, j: (0, 0)
    return pl.pallas_call(
        _merge_kernel,
        grid=(nreq, seq // tm),
        in_specs=[
            pl.BlockSpec((1, tm, D_MODEL), tok),
            pl.BlockSpec((1, N_MOD, D_MODEL), lambda b, j: (mod_row(b), 0, 0)),
            pl.BlockSpec((1, D_MODEL), const2),
            pl.BlockSpec((1, D_MODEL), const2),
            pl.BlockSpec((1, tm, D_CONV), tok),
            pl.BlockSpec((1, tm, D_ATTN), tok),
            pl.BlockSpec((D_MODEL, D_IN // 2), lambda b, j: (0, 1)),
            pl.BlockSpec(w_pw.shape, const2),
            pl.BlockSpec(w_ao.shape, const2),
            pl.BlockSpec(w_out.shape, const2),
            pl.BlockSpec(w_r.shape, const2),
        ],
        out_specs=[
            pl.BlockSpec((1, tm, D_MODEL), tok),
            pl.BlockSpec((1, tm, D_MODEL), tok),
            pl.BlockSpec((1, tm, N_EXPERTS), tok),
        ],
        out_shape=[
            jax.ShapeDtypeStruct((nreq, seq, D_MODEL), F32),
            jax.ShapeDtypeStruct((nreq, seq, D_MODEL), BF16),
            jax.ShapeDtypeStruct((nreq, seq, N_EXPERTS), F32),
        ],
        compiler_params=_params("arbitrary", "arbitrary"),
        name="merge",
    )(x, mods, g1, g2, cact, heads, w_g, w_pw, w_ao, w_out, w_r)


def _route_kernel(aff_ref, slot_t_ref, slot_r_ref, aff_r_ref, *, cap):
    ngroup = aff_ref.shape[0]
    a = jnp.concatenate([aff_ref[g] for g in range(ngroup)], axis=1)
    seq = a.shape[0]
    one, zero = jnp.ones_like(a), jnp.zeros_like(a)

    def search(i, thr):
        cand = thr | jnp.left_shift(jnp.int32(1), 30 - i)
        cnt = jnp.sum(jnp.where(a >= pltpu.bitcast(cand, F32), one, zero), axis=0, keepdims=True)
        return jnp.where(cnt >= cap, cand, thr)

    thr = lax.fori_loop(0, 31, search, jnp.zeros((1, ngroup * LANES), jnp.int32))
    gt = a >= pltpu.bitcast(thr + 1, F32)
    eq = (a >= pltpu.bitcast(thr, F32)) & jnp.logical_not(gt)
    need = cap - jnp.sum(jnp.where(gt, one, zero), axis=0, keepdims=True)
    ti = lax.broadcasted_iota(jnp.int32, (seq, seq), 0)
    tj = lax.broadcasted_iota(jnp.int32, (seq, seq), 1)
    before = jnp.where(tj < ti, 1.0, 0.0).astype(BF16)
    eq_rank = _dot(before, jnp.where(eq, one, zero).astype(BF16))
    sel = gt | (eq & (eq_rank < need))
    pos = _dot(before, jnp.where(sel, one, zero).astype(BF16))
    slot = jnp.where(sel, pos, -1.0)
    for g in range(ngroup):
        lanes = slice(g * LANES, (g + 1) * LANES)
        slot_t_ref[g] = slot[:, lanes]
        slot_r_ref[g] = slot[:, lanes].T
        aff_r_ref[g] = aff_ref[g].T


def _route(aff_t, cap):
    ngroup, seq, _ = aff_t.shape
    tspec = pl.BlockSpec((ngroup, seq, LANES), lambda g: (0, 0, 0))
    rspec = pl.BlockSpec((ngroup, LANES, seq), lambda g: (0, 0, 0))
    return pl.pallas_call(
        functools.partial(_route_kernel, cap=cap),
        grid=(1,),
        in_specs=[tspec],
        out_specs=[tspec, rspec, rspec],
        out_shape=[
            jax.ShapeDtypeStruct((ngroup, seq, LANES), F32),
            jax.ShapeDtypeStruct((ngroup, LANES, seq), F32),
            jax.ShapeDtypeStruct((ngroup, LANES, seq), F32),
        ],
        compiler_params=_params("arbitrary"),
        name="route",
    )(aff_t)


GATHER_ROWS = 512


def _gather_kernel(slot_ref, aff_ref, h_ref, x_ref, g_ref, *, cap):
    per_step, seq, _ = h_ref.shape
    nslot = N_EXPERTS * cap
    sub = lax.broadcasted_iota(jnp.int32, (cap, seq), 0).astype(F32)
    one, zero = jnp.ones((cap, seq), F32), jnp.zeros((cap, seq), F32)
    ei = lax.broadcasted_iota(jnp.int32, (N_EXPERTS, nslot), 0)
    si = lax.broadcasted_iota(jnp.int32, (N_EXPERTS, nslot), 1)
    own = (si >= ei * cap) & (si < (ei + 1) * cap)
    per_chunk = GATHER_ROWS // cap
    for r in range(per_step):
        h = h_ref[r]
        a = aff_ref[0, r * N_EXPERTS:(r + 1) * N_EXPERTS, :]
        a_hi = a.astype(BF16)
        r1 = a - a_hi.astype(F32)
        a_mid = r1.astype(BF16)
        a_lo = (r1 - a_mid.astype(F32)).astype(BF16)
        a3 = jnp.concatenate([a_hi, a_mid, a_lo], axis=0)
        g3 = []
        for c in range(nslot // GATHER_ROWS):
            hot = []
            for e in range(c * per_chunk, (c + 1) * per_chunk):
                row = jnp.broadcast_to(slot_ref[0, r * N_EXPERTS + e:r * N_EXPERTS + e + 1, :], (cap, seq))
                hot.append(jnp.where(row == sub, one, zero).astype(BF16))
            hot = jnp.concatenate(hot, axis=0)
            x = _dot(hot, h).astype(x_ref.dtype)
            for i in range(per_chunk):
                x_ref[c * per_chunk + i, r * cap:(r + 1) * cap, :] = x[i * cap:(i + 1) * cap]
            g3.append(lax.dot_general(a3, hot, _NT, preferred_element_type=F32))
        g3 = jnp.concatenate(g3, axis=1)
        gfull = g3[0:N_EXPERTS] + g3[N_EXPERTS:2 * N_EXPERTS] + g3[2 * N_EXPERTS:3 * N_EXPERTS]
        g_ref[r] = jnp.sum(jnp.where(own, gfull, 0.0), axis=0, keepdims=True)


def _gather(slot_r, aff_r, h2, cap, per_step):
    nreq, seq, _ = h2.shape
    nslot = N_EXPERTS * cap
    steps_per_group = REQ_PER_GROUP // per_step
    grp = lambda s: (s // steps_per_group, s % steps_per_group, 0)
    x, g = pl.pallas_call(
        functools.partial(_gather_kernel, cap=cap),
        grid=(nreq // per_step,),
        in_specs=[
            pl.BlockSpec((1, per_step * N_EXPERTS, seq), grp),
            pl.BlockSpec((1, per_step * N_EXPERTS, seq), grp),
            pl.BlockSpec((per_step, seq, D_MODEL), lambda s: (s, 0, 0)),
        ],
        out_specs=[
            pl.BlockSpec((N_EXPERTS, per_step * cap, D_MODEL), lambda s: (0, s, 0)),
            pl.BlockSpec((per_step, 1, nslot), lambda s: (s, 0, 0)),
        ],
        out_shape=[
            jax.ShapeDtypeStruct((N_EXPERTS, nreq * cap, D_MODEL), BF16),
            jax.ShapeDtypeStruct((nreq, 1, nslot), F32),
        ],
        compiler_params=_params("arbitrary"),
        name="gather",
    )(slot_r, aff_r, h2)
    return x, g


EXPERT_FCHUNK = 1024
EXPERT_ROWS = 512
GATE_ROWS = 256


def _expert_kernel(xc_ref, xl_ref, g_ref, wg_ref, wu_ref, wd_ref, yc_ref, yl_ref,
                   acc_ref):
    e = pl.program_id(0)
    f = pl.program_id(1)
    nf = pl.num_programs(1)
    half = xc_ref.shape[1]

    @pl.when((e == 0) & (f == 0))
    def _():
        acc_ref[...] = jnp.zeros_like(acc_ref)

    ri = lax.broadcasted_iota(jnp.int32, (GATE_ROWS, GATE_ROWS), 0)
    ci = lax.broadcasted_iota(jnp.int32, (GATE_ROWS, GATE_ROWS), 1)
    for part, (x_ref, y_ref) in enumerate(((xc_ref, yc_ref), (xl_ref, yl_ref))):
        for i in range(half // EXPERT_ROWS):
            x = x_ref[0, i * EXPERT_ROWS:(i + 1) * EXPERT_ROWS, :]
            gq = _dot(x, wg_ref[0].astype(BF16))
            uq = _dot(x, wu_ref[0].astype(BF16))
            hid = (gq * _sigmoid(gq) * uq).astype(BF16)
            r0 = part * half + i * EXPERT_ROWS
            prev = jnp.where(f == 0, 0.0, acc_ref[r0:r0 + EXPERT_ROWS, :])
            acc = prev + _dot(hid, wd_ref[0].astype(BF16))
            acc_ref[r0:r0 + EXPERT_ROWS, :] = acc
            for j in range(EXPERT_ROWS // GATE_ROWS):
                t = r0 // GATE_ROWS + j
                grow = jnp.broadcast_to(g_ref[0, t:t + 1, :], (GATE_ROWS, GATE_ROWS))
                gcol = jnp.sum(jnp.where(ri == ci, grow, 0.0), axis=-1, keepdims=True)
                y = acc[j * GATE_ROWS:(j + 1) * GATE_ROWS] * gcol
                y_ref[0, i * EXPERT_ROWS + j * GATE_ROWS:i * EXPERT_ROWS + (j + 1) * GATE_ROWS, :] = (
                    y.astype(y_ref.dtype))


def _experts(xc, xl, g, w_gate, w_up, w_down):
    ne, half, _ = xc.shape
    fc = EXPERT_FCHUNK
    xspec = pl.BlockSpec((1, half, D_MODEL), lambda e, f: (e, 0, 0))
    return pl.pallas_call(
        _expert_kernel,
        grid=(ne, D_EXPERT // fc),
        in_specs=[
            xspec, xspec,
            pl.BlockSpec((1,) + g.shape[1:], lambda e, f: (e, 0, 0)),
            pl.BlockSpec((1, D_MODEL, fc), lambda e, f: (e, 0, f)),
            pl.BlockSpec((1, D_MODEL, fc), lambda e, f: (e, 0, f)),
            pl.BlockSpec((1, fc, D_MODEL), lambda e, f: (e, f, 0)),
        ],
        out_specs=[xspec, xspec],
        out_shape=[jax.ShapeDtypeStruct(xc.shape, BF16), jax.ShapeDtypeStruct(xl.shape, BF16)],
        scratch_shapes=[pltpu.VMEM((2 * half, D_MODEL), F32)],
        compiler_params=_params("arbitrary", "arbitrary"),
        name="experts",
    )(xc, xl, g, w_gate, w_up, w_down)


COMBINE_ROWS = 256
COMBINE_LAT_ROWS = 512
MOE_CTX_PER_STEP = 4


def _combine_kernel(x1_ref, mods_ref, slot_ref, y_ref, gf_ref, o_ref, *, cap):
    per_step, tm, _ = x1_ref.shape
    nslot = N_EXPERTS * cap
    li = lax.broadcasted_iota(jnp.int32, (LANES, nslot), 0)
    si = lax.broadcasted_iota(jnp.int32, (LANES, nslot), 1)
    want = (lax.broadcasted_iota(jnp.int32, (COMBINE_ROWS, nslot), 1) & (cap - 1)).astype(F32)
    chains = [(r, t) for r in range(per_step) for t in range(tm // COMBINE_ROWS)]
    rows = {t: slice(t * COMBINE_ROWS, (t + 1) * COMBINE_ROWS) for _, t in chains}
    expand = {}
    for r in range(per_step):
        lane0 = ((pl.program_id(0) * per_step + r) % REQ_PER_GROUP) * N_EXPERTS
        expand[r] = jnp.where(li == lane0 + si // cap, 1.0, 0.0).astype(BF16)
    slot_exp = {(r, t): _dot(slot_ref[0, rows[t], :].astype(BF16), expand[r]) for r, t in chains}
    scat = {c: jnp.where(slot_exp[c] == want, 1.0, 0.0).astype(BF16) for c in chains}
    moe = {(r, t): _dot(scat[r, t], y_ref[:, r * cap:(r + 1) * cap, :].reshape(nslot, D_MODEL))
           for r, t in chains}
    for r, t in chains:
        x2 = x1_ref[r, rows[t], :] + mods_ref[0][5:6] * moe[r, t]
        o_ref[r, rows[t], :] = (x2 * lax.rsqrt(jnp.mean(x2 * x2, axis=-1, keepdims=True) + EPS)
                                * gf_ref[...])


def _combine(x1, mods, mod_row, slot_t, y, gf, cap, per_step, tm):
    nreq, seq, _ = x1.shape
    assert per_step == 1 or seq == tm
    tok = lambda s, j: (s, j, 0)
    return pl.pallas_call(
        functools.partial(_combine_kernel, cap=cap),
        grid=(nreq // per_step, seq // tm),
        in_specs=[
            pl.BlockSpec((per_step, tm, D_MODEL), tok),
            pl.BlockSpec((1, N_MOD, D_MODEL), lambda s, j: (mod_row(s * per_step), 0, 0)),
            pl.BlockSpec((1, tm, LANES), lambda s, j: (s * per_step // REQ_PER_GROUP, j, 0)),
            pl.BlockSpec((N_EXPERTS, per_step * cap, D_MODEL), lambda s, j: (0, s, 0)),
            pl.BlockSpec((1, D_MODEL), lambda s, j: (0, 0)),
        ],
        out_specs=pl.BlockSpec((per_step, tm, D_MODEL), tok),
        out_shape=jax.ShapeDtypeStruct(x1.shape, F32),
        compiler_params=_params("arbitrary", "arbitrary"),
        name="combine",
    )(x1, mods, slot_t, y, gf)


def _group_lanes(aff):
    nreq, seq, ne = aff.shape
    a = aff.reshape(nreq // REQ_PER_GROUP, REQ_PER_GROUP, seq, ne)
    return jnp.transpose(a, (0, 2, 1, 3)).reshape(nreq // REQ_PER_GROUP, seq, REQ_PER_GROUP * ne)


def kernel(x_prompt, x_sample, cache_ctx_k, cache_ctx_v, c, c_ctx, w_ada, b_ada, norm1_g, w_in,
           conv_dw_w, conv_dw_b, conv_ln_g, conv_ln_b, w_conv_pw, na_rpb, w_attn_o, w_out,
           norm2_g, w_router, w_gate_e, w_up_e, w_down_e, final_g):
    assert w_ada.shape[0] == 1, "single trunk layer"
    nctx, ctx_seq, _ = x_prompt.shape
    nlat, lat_seq, _ = x_sample.shape
    ctx_cap = EC_CAPACITY * ctx_seq // N_EXPERTS
    lat_cap = EC_CAPACITY * lat_seq // N_EXPERTS
    assert nctx * ctx_cap == nlat * lat_cap

    cond = jnp.zeros((2 * nlat, D_MODEL), F32).at[:nlat].set(c).at[nlat].set(c_ctx)
    mods = _ada(cond, w_ada[0], b_ada).reshape(2 * nlat, N_MOD, D_MODEL)
    ctx_row = lambda b: nlat
    lat_row = lambda b: b

    w_a = w_g = w_in[0].astype(BF16)
    w_pw = w_conv_pw[0].astype(BF16)
    w_ao = w_attn_o[0].astype(BF16)
    w_o = w_out[0].astype(BF16)
    w_r = w_router[0].astype(BF16)
    conv_args = (conv_dw_w[0], conv_dw_b, conv_ln_g, conv_ln_b)

    q_c, k_c, v_c, cact_c = _front(x_prompt, mods, ctx_row, norm1_g, w_a, *conv_args, F32,
                                   FRONT_CTX_PER_STEP)
    q_l, k_l, v_l, cact_l = _front(x_sample, mods, lat_row, norm1_g, w_a, *conv_args, BF16, 1)

    heads_c = _ctx_attn(q_c, k_c, v_c)
    tab = _bias_table(na_rpb[0])
    past = cache_ctx_k.shape[2]
    feat_major = lambda t: jnp.transpose(t[:, 0], (0, 2, 3, 1)).reshape(nlat, D_ATTN, past)
    heads_l = _na_attn(q_l, k_l, v_l, feat_major(cache_ctx_k), feat_major(cache_ctx_v), tab)

    merge_w = (w_g, w_pw, w_ao, w_o, w_r)
    flat = lambda t: t.reshape(1, nctx * ctx_seq, t.shape[-1])
    x1_c, h2_c, aff_c = _merge(flat(x_prompt), mods, ctx_row, norm1_g, norm2_g, flat(cact_c),
                               flat(heads_c), *merge_w)
    x1_c, h2_c, aff_c = (t.reshape(nctx, ctx_seq, t.shape[-1]) for t in (x1_c, h2_c, aff_c))
    x1_l, h2_l, aff_l = _merge(x_sample, mods, lat_row, norm1_g, norm2_g, cact_l, heads_l, *merge_w)

    slot_t_c, slot_r_c, aff_r_c = _route(_group_lanes(aff_c), ctx_cap)
    slot_t_l, slot_r_l, aff_r_l = _route(_group_lanes(aff_l), lat_cap)

    xg_c, g_c = _gather(slot_r_c, aff_r_c, h2_c, ctx_cap, MOE_CTX_PER_STEP)
    xg_l, g_l = _gather(slot_r_l, aff_r_l, h2_l, lat_cap, 1)

    def gate_rows(g, nreq, cap):
        g = jnp.transpose(g.reshape(nreq, N_EXPERTS, cap), (1, 0, 2))
        return g.reshape(N_EXPERTS, nreq * cap // GATE_ROWS, GATE_ROWS)

    gates = jnp.concatenate([gate_rows(g_c, nctx, ctx_cap), gate_rows(g_l, nlat, lat_cap)], axis=1)
    y_c, y_l = _experts(xg_c, xg_l, gates, w_gate_e[0], w_up_e[0], w_down_e[0])

    gf = final_g.reshape(1, D_MODEL)
    y_prompt = _combine(x1_c, mods, ctx_row, slot_t_c, y_c, gf, ctx_cap, MOE_CTX_PER_STEP, ctx_seq)
    y_sample = _combine(x1_l, mods, lat_row, slot_t_l, y_l, gf, lat_cap, 1, COMBINE_LAT_ROWS)

    state_k = k_c.reshape(nctx, 1, ctx_seq, N_HEADS, HEAD_DIM)
    state_v = v_c.reshape(nctx, 1, ctx_seq, N_HEADS, HEAD_DIM)
    return (y_prompt, y_sample, state_k, state_v)
```

```python
import functools

import jax
import jax.numpy as jnp
from jax import lax
from jax.experimental import pallas as pl
from jax.experimental.pallas import tpu as pltpu

F32 = jnp.float32
BF16 = jnp.bfloat16

D_MODEL = 1024
D_CONV = 512
CONV_WIDTH = 31
N_HEADS = 8
HEAD_DIM = 64
D_ATTN = N_HEADS * HEAD_DIM
GRID_W = 64
NA_ROWS = 8
NA_COLS = 16
N_EXPERTS = 16
D_EXPERT = 2048
EC_CAPACITY = 2
N_MOD = 6
EPS = 1e-6
D_PROJ_A = 2 * D_CONV + 3 * D_ATTN
D_IN = D_PROJ_A + 2 * D_MODEL
LANES = 128
REQ_PER_GROUP = LANES // N_EXPERTS
MASK_NEG = -1e30
VMEM_LIMIT = 56 * 1024 * 1024

_NT = (((1,), (1,)), ((), ()))


def _params(*sem):
    return pltpu.CompilerParams(dimension_semantics=sem, vmem_limit_bytes=VMEM_LIMIT)


def _dot(a, b):
    return jnp.dot(a, b, preferred_element_type=F32)


def _sigmoid(x):
    return 1.0 / (1.0 + jnp.exp(-x))


def _split_bf16(x):
    hi = x.astype(BF16)
    lo = (x - hi.astype(F32)).astype(BF16)
    return hi, lo


def _rms_mod(x, g, shift, scale):
    xn = x * lax.rsqrt(jnp.mean(x * x, axis=-1, keepdims=True) + EPS) * g
    return xn * (1.0 + scale) + shift


ADA_COLS = 1024


def _ada_kernel(cond_ref, w_ref, b_ref, o_ref):
    c = cond_ref[...]
    a = c * _sigmoid(c)
    ah, al = _split_bf16(a)
    wh, wl = _split_bf16(w_ref[...])
    o_ref[...] = _dot(ah, wh) + _dot(al, wh) + _dot(ah, wl) + b_ref[...]


def _ada(cond, w_ada, b_ada):
    n = w_ada.shape[1]
    bn = ADA_COLS
    return pl.pallas_call(
        _ada_kernel,
        grid=(n // bn,),
        in_specs=[
            pl.BlockSpec(cond.shape, lambda j: (0, 0)),
            pl.BlockSpec((D_MODEL, bn), lambda j: (0, j)),
            pl.BlockSpec((1, bn), lambda j: (0, j)),
        ],
        out_specs=pl.BlockSpec((cond.shape[0], bn), lambda j: (0, j)),
        out_shape=jax.ShapeDtypeStruct((cond.shape[0], n), F32),
        compiler_params=_params("arbitrary"),
        name="ada",
    )(cond, w_ada, b_ada)


CONV_PAD = 16
FRONT_ROWS = 256
CONV_ROWS = 128
SUBLANES = 8
FRONT_CTX_PER_STEP = 4


def _front_kernel(x_ref, mods_ref, g1_ref, w_ref, dww_ref, dwb_ref, lng_ref, lnb_ref,
                  q_ref, k_ref, v_ref, c_ref, *, seq):
    nreq = x_ref.shape[0]
    m = mods_ref[0]
    shift, scale = m[0:1], m[1:2]
    nchunk = seq // FRONT_ROWS
    per_chunk = FRONT_ROWS // CONV_ROWS
    zeros = jnp.zeros((CONV_PAD, D_CONV), F32)
    glu, stores = {}, []

    def proj(r, i):
        rows = slice(i * FRONT_ROWS, (i + 1) * FRONT_ROWS)
        x = x_ref[r, rows, :]
        h = _rms_mod(x, g1_ref[...], shift, scale).astype(BF16)
        u = _dot(h, w_ref[...])
        glu[r, i] = u[:, :D_CONV] * _sigmoid(u[:, D_CONV:2 * D_CONV])
        o = 2 * D_CONV
        stores.append((q_ref, r, rows, (u[:, o:o + D_ATTN] * (HEAD_DIM ** -0.5)).astype(q_ref.dtype)))
        stores.append((k_ref, r, rows, u[:, o + D_ATTN:o + 2 * D_ATTN].astype(k_ref.dtype)))
        stores.append((v_ref, r, rows, u[:, o + 2 * D_ATTN:o + 3 * D_ATTN].astype(v_ref.dtype)))

    def conv(r, j):
        i, o = j // per_chunk, (j % per_chunk) * CONV_ROWS
        g = glu[r, i]
        if o:
            top = g[o - CONV_PAD:o]
        else:
            top = glu[r, i - 1][FRONT_ROWS - CONV_PAD:] if i else zeros
        if o + CONV_ROWS < FRONT_ROWS:
            bottom = g[o + CONV_ROWS:o + CONV_ROWS + CONV_PAD]
        else:
            bottom = glu[r, i + 1][:CONV_PAD] if i + 1 < nchunk else zeros
        window = jnp.concatenate([top, g[o:o + CONV_ROWS], bottom], axis=0)
        cols = []
        for cb in range(D_CONV // LANES):
            cs = slice(cb * LANES, (cb + 1) * LANES)
            win = window[:, cs]
            acc = None
            for s in range(SUBLANES):
                nrow = CONV_ROWS + (SUBLANES if s else 0)
                z = None
                for a in range(2 * CONV_PAD // SUBLANES):
                    t = SUBLANES * a + s - (CONV_PAD - CONV_WIDTH // 2)
                    if 0 <= t < CONV_WIDTH:
                        term = win[SUBLANES * a:SUBLANES * a + nrow] * dww_ref[t:t + 1, cs]
                        z = term if z is None else z + term
                z = z[s:s + CONV_ROWS]
                acc = z if acc is None else acc + z
            cols.append(acc)
        y = jnp.concatenate(cols, axis=1) + dwb_ref[...]
        mu = jnp.mean(y, axis=-1, keepdims=True)
        yc = y - mu
        var = jnp.mean(yc * yc, axis=-1, keepdims=True)
        z = yc * lax.rsqrt(var + EPS) * lng_ref[...] + lnb_ref[...]
        stores.append((c_ref, r, slice(j * CONV_ROWS, (j + 1) * CONV_ROWS),
                       (z * _sigmoid(z)).astype(c_ref.dtype)))

    for r in range(nreq):
        for i in range(nchunk):
            proj(r, i)
    for r in range(nreq):
        for j in range(seq // CONV_ROWS):
            conv(r, j)
    for ref, r, rows, val in stores:
        ref[r, rows, :] = val


def _front(x, mods, mod_row, g1, w_a, dww, dwb, lng, lnb, kv_dtype, per_step):
    nreq, seq, _ = x.shape
    const2 = lambda b: (0, 0)
    tok = lambda b: (b, 0, 0)
    out_tok = pl.BlockSpec((per_step, seq, D_ATTN), tok)
    return pl.pallas_call(
        functools.partial(_front_kernel, seq=seq),
        grid=(nreq // per_step,),
        in_specs=[
            pl.BlockSpec((per_step, seq, D_MODEL), tok),
            pl.BlockSpec((1, N_MOD, D_MODEL), lambda b: (mod_row(b * per_step), 0, 0)),
            pl.BlockSpec((1, D_MODEL), const2),
            pl.BlockSpec((D_MODEL, D_PROJ_A), const2),
            pl.BlockSpec((CONV_WIDTH, D_CONV), const2),
            pl.BlockSpec((1, D_CONV), const2),
            pl.BlockSpec((1, D_CONV), const2),
            pl.BlockSpec((1, D_CONV), const2),
        ],
        out_specs=[out_tok, out_tok, out_tok, out_tok],
        out_shape=[
            jax.ShapeDtypeStruct((nreq, seq, D_ATTN), BF16),
            jax.ShapeDtypeStruct((nreq, seq, D_ATTN), kv_dtype),
            jax.ShapeDtypeStruct((nreq, seq, D_ATTN), kv_dtype),
            jax.ShapeDtypeStruct((nreq, seq, D_CONV), BF16),
        ],
        compiler_params=_params("arbitrary"),
        name="front",
    )(x, mods, g1, w_a, dww, dwb, lng, lnb)


def _pair_scores_q(q_pair):
    lane = lax.broadcasted_iota(jnp.int32, q_pair.shape, 1)
    zero = jnp.zeros_like(q_pair)
    return jnp.concatenate([jnp.where(lane < HEAD_DIM, q_pair, zero),
                            jnp.where(lane >= HEAD_DIM, q_pair, zero)], axis=0)


def _pair_merge(o2, t):
    lane = lax.broadcasted_iota(jnp.int32, (t, LANES), 1)
    return jnp.where(lane < HEAD_DIM, o2[:t], o2[t:])


CTX_ATTN_PER_STEP = 4


def _ctx_attn_kernel(q_ref, k_ref, v_ref, o_ref):
    nreq, seq, _ = q_ref.shape
    sls = [slice(p * LANES, (p + 1) * LANES) for p in range(N_HEADS // 2)]
    chains = [(r, p) for r in range(nreq) for p in range(N_HEADS // 2)]
    s = {(r, p): lax.dot_general(_pair_scores_q(q_ref[r, :, sls[p]]), k_ref[r, :, sls[p]].astype(BF16),
                                 _NT, preferred_element_type=F32) for r, p in chains}
    e = {c: jnp.exp(s[c] - jnp.max(s[c], axis=-1, keepdims=True)) for c in chains}
    den = {c: jnp.sum(e[c], axis=-1, keepdims=True) for c in chains}
    for r, p in chains:
        o2 = _dot(e[r, p].astype(BF16), v_ref[r, :, sls[p]].astype(BF16)) / den[r, p]
        o_ref[r, :, sls[p]] = _pair_merge(o2, seq).astype(o_ref.dtype)


def _ctx_attn(q, k, v):
    nreq, seq, _ = q.shape
    spec = pl.BlockSpec((CTX_ATTN_PER_STEP, seq, D_ATTN), lambda b: (b, 0, 0))
    return pl.pallas_call(
        _ctx_attn_kernel,
        grid=(nreq // CTX_ATTN_PER_STEP,),
        in_specs=[spec, spec, spec],
        out_specs=spec,
        out_shape=jax.ShapeDtypeStruct((nreq, seq, D_ATTN), BF16),
        compiler_params=_params("arbitrary"),
        name="ctx_attn",
    )(q, k, v)


N_DR = 2 * NA_ROWS - 1
N_DC = 2 * NA_COLS - 1


def _bias_kernel(rpb_ref, o_ref, t2_ref):
    h = pl.program_id(0)
    qi = lax.broadcasted_iota(jnp.int32, (GRID_W, LANES), 0)
    lane = lax.broadcasted_iota(jnp.int32, (GRID_W, LANES), 1)
    kc = lane & (GRID_W - 1)
    d = kc - qi
    cs = jnp.clip(qi - NA_COLS // 2, 0, GRID_W - NA_COLS)
    inside = (kc >= cs) & (kc < cs + NA_COLS)
    group = 5
    for dr0 in range(0, N_DR, group):
        ts = [jnp.full((GRID_W, LANES), MASK_NEG, F32)] * group
        for j in range(N_DC):
            hit = d == j - (NA_COLS - 1)
            ts = [jnp.where(hit, rpb_ref[h * (N_DR * N_DC) + (dr0 + g) * N_DC + j], ts[g])
                  for g in range(group)]
        for g in range(group):
            t2_ref[dr0 + g] = jnp.where(inside, ts[g], MASK_NEG)
    for o in range(NA_ROWS):
        for jj in range(NA_ROWS // 2):
            o_ref[0, o, :, jj * LANES:(jj + 1) * LANES] = jnp.where(
                lane < GRID_W, t2_ref[o + 2 * jj], t2_ref[o + 2 * jj + 1])


def _bias_table(rpb):
    return pl.pallas_call(
        _bias_kernel,
        grid=(N_HEADS,),
        in_specs=[pl.BlockSpec(memory_space=pltpu.SMEM)],
        out_specs=pl.BlockSpec((1, NA_ROWS, GRID_W, NA_ROWS * GRID_W), lambda h: (h, 0, 0, 0)),
        out_shape=jax.ShapeDtypeStruct((N_HEADS, NA_ROWS, GRID_W, NA_ROWS * GRID_W), F32),
        scratch_shapes=[pltpu.VMEM((N_DR, GRID_W, LANES), F32)],
        compiler_params=_params("arbitrary"),
        name="na_bias",
    )(rpb.reshape(-1))


NA_STEP_ROWS = 4


def _na_attn_kernel(q_ref, k_ref, v_ref, kct_ref, vct_ref, tab_ref, o_ref, kcb_ref, vcb_ref, *, rows):
    r = pl.program_id(1)

    @pl.when(r == 0)
    def _():
        kcb_ref[...] = kct_ref[0].astype(BF16)
        vcb_ref[...] = vct_ref[0].astype(BF16)

    nloc = NA_ROWS * GRID_W
    sls = [slice(p * LANES, (p + 1) * LANES) for p in range(N_HEADS // 2)]
    k0, off = [], []
    for i in range(NA_STEP_ROWS):
        qrow = r * NA_STEP_ROWS + i
        rs = jnp.clip(qrow - NA_ROWS // 2, 0, rows - NA_ROWS)
        k0.append(pl.multiple_of(rs * GRID_W, GRID_W))
        off.append(rs - qrow + (NA_ROWS - 1))
    chains = [(i, p) for i in range(NA_STEP_ROWS) for p in range(N_HEADS // 2)]
    qrows = [slice(i * GRID_W, (i + 1) * GRID_W) for i in range(NA_STEP_ROWS)]
    s_loc, s_ctx = {}, {}
    for i, p in chains:
        q2 = _pair_scores_q(q_ref[0, qrows[i], sls[p]])
        bias = jnp.concatenate([tab_ref[2 * p, off[i]], tab_ref[2 * p + 1, off[i]]], axis=0)
        kl = k_ref[0, pl.ds(k0[i], nloc), sls[p]]
        s_loc[i, p] = lax.dot_general(q2, kl, _NT, preferred_element_type=F32) + bias
        s_ctx[i, p] = _dot(q2, kcb_ref[sls[p], :])
    mx = {c: jnp.maximum(jnp.max(s_loc[c], axis=-1, keepdims=True),
                         jnp.max(s_ctx[c], axis=-1, keepdims=True)) for c in chains}
    e_loc = {c: jnp.exp(s_loc[c] - mx[c]) for c in chains}
    e_ctx = {c: jnp.exp(s_ctx[c] - mx[c]) for c in chains}
    den = {c: jnp.sum(e_loc[c], axis=-1, keepdims=True) + jnp.sum(e_ctx[c], axis=-1, keepdims=True)
           for c in chains}
    for i, p in chains:
        vl = v_ref[0, pl.ds(k0[i], nloc), sls[p]]
        o_ctx = lax.dot_general(e_ctx[i, p].astype(BF16), vcb_ref[sls[p], :], _NT,
                                preferred_element_type=F32)
        o2 = (_dot(e_loc[i, p].astype(BF16), vl) + o_ctx) / den[i, p]
        o_ref[0, qrows[i], sls[p]] = _pair_merge(o2, GRID_W).astype(o_ref.dtype)


def _na_attn(q, k, v, kc, vc, tab):
    nreq, seq, _ = q.shape
    rows = seq // GRID_W
    past = kc.shape[2]
    qspec = pl.BlockSpec((1, NA_STEP_ROWS * GRID_W, D_ATTN), lambda b, r: (b, r, 0))
    kvspec = pl.BlockSpec((1, seq, D_ATTN), lambda b, r: (b, 0, 0))
    cspec = pl.BlockSpec((1, D_ATTN, past), lambda b, r: (b, 0, 0))
    return pl.pallas_call(
        functools.partial(_na_attn_kernel, rows=rows),
        grid=(nreq, rows // NA_STEP_ROWS),
        in_specs=[qspec, kvspec, kvspec, cspec, cspec,
                  pl.BlockSpec(tab.shape, lambda b, r: (0, 0, 0, 0))],
        out_specs=qspec,
        out_shape=jax.ShapeDtypeStruct((nreq, seq, D_ATTN), BF16),
        scratch_shapes=[pltpu.VMEM((D_ATTN, past), BF16), pltpu.VMEM((D_ATTN, past), BF16)],
        compiler_params=_params("arbitrary", "arbitrary"),
        name="na_attn",
    )(q, k, v, kc, vc, tab)


MERGE_ROWS = 512
MERGE_SUB = 256


def _merge_kernel(x_ref, mods_ref, g1_ref, g2_ref, c_ref, a_ref, wg_ref, wpw_ref, wao_ref,
                  wout_ref, wr_ref, x1_ref, h2_ref, aff_ref):
    m = mods_ref[0]
    subs = range(MERGE_ROWS // MERGE_SUB)
    rows = [slice(i * MERGE_SUB, (i + 1) * MERGE_SUB) for i in subs]
    x = [x_ref[0, rows[i], :] for i in subs]
    h = [_rms_mod(x[i], g1_ref[...], m[0:1], m[1:2]).astype(BF16) for i in subs]
    gates = [_sigmoid(_dot(h[i], wg_ref[:, D_PROJ_A - D_IN // 2:])) for i in subs]
    conv_out = [_dot(c_ref[0, rows[i], :], wpw_ref[...]) for i in subs]
    attn_out = [_dot(a_ref[0, rows[i], :], wao_ref[...]) for i in subs]
    merged = [(gates[i][:, :D_MODEL] * conv_out[i] + gates[i][:, D_MODEL:] * attn_out[i]).astype(BF16)
              for i in subs]
    x1 = [x[i] + m[2:3] * _dot(merged[i], wout_ref[...]) for i in subs]
    h2 = [_rms_mod(x1[i], g2_ref[...], m[3:4], m[4:5]).astype(BF16) for i in subs]
    logits = [_dot(h2[i], wr_ref[...]) for i in subs]
    e = [jnp.exp(logits[i] - jnp.max(logits[i], axis=-1, keepdims=True)) for i in subs]
    for i in subs:
        x1_ref[0, rows[i], :] = x1[i]
        h2_ref[0, rows[i], :] = h2[i]
        aff_ref[0, rows[i], :] = e[i] / jnp.sum(e[i], axis=-1, keepdims=True)


def _merge(x, mods, mod_row, g1, g2, cact, heads, w_g, w_pw, w_ao, w_out, w_r):
    nreq, seq, _ = x.shape
    tm = MERGE_ROWS
    tok = lambda b, j: (b, j, 0)
    const2 = lambda b, j: (0, 0)
    return pl.pallas_call(
        _merge_kernel,
        grid=(nreq, seq // tm),
        in_specs=[
            pl.BlockSpec((1, tm, D_MODEL), tok),
            pl.BlockSpec((1, N_MOD, D_MODEL), lambda b, j: (mod_row(b), 0, 0)),
            pl.BlockSpec((1, D_MODEL), const2),
            pl.BlockSpec((1, D_MODEL), const2),
            pl.BlockSpec((1, tm, D_CONV), tok),
            pl.BlockSpec((1, tm, D_ATTN), tok),
            pl.BlockSpec((D_MODEL, D_IN // 2), lambda b, j: (0, 1)),
            pl.BlockSpec(w_pw.shape, const2),
            pl.BlockSpec(w_ao.shape, const2),
            pl.BlockSpec(w_out.shape, const2),
            pl.BlockSpec(w_r.shape, const2),
        ],
        out_specs=[
            pl.BlockSpec((1, tm, D_MODEL), tok),
            pl.BlockSpec((1, tm, D_MODEL), tok),
            pl.BlockSpec((1, tm, N_EXPERTS), tok),
        ],
        out_shape=[
            jax.ShapeDtypeStruct((nreq, seq, D_MODEL), F32),
            jax.ShapeDtypeStruct((nreq, seq, D_MODEL), BF16),
            jax.ShapeDtypeStruct((nreq, seq, N_EXPERTS), F32),
        ],
        compiler_params=_params("arbitrary", "arbitrary"),
        name="merge",
    )(x, mods, g1, g2, cact, heads, w_g, w_pw, w_ao, w_out, w_r)


def _route_kernel(aff_ref, slot_t_ref, slot_r_ref, aff_r_ref, *, cap):
    nreq, seq, _ = aff_ref.shape
    ngroup = nreq // REQ_PER_GROUP
    a = jnp.concatenate([aff_ref[b] for b in range(nreq)], axis=1)
    one, zero = jnp.ones_like(a), jnp.zeros_like(a)

    def search(i, thr):
        cand = thr | jnp.left_shift(jnp.int32(1), 30 - i)
        cnt = jnp.sum(jnp.where(a >= pltpu.bitcast(cand, F32), one, zero), axis=0, keepdims=True)
        return jnp.where(cnt >= cap, cand, thr)

    thr = lax.fori_loop(0, 31, search, jnp.zeros((1, ngroup * LANES), jnp.int32))
    gt = a >= pltpu.bitcast(thr + 1, F32)
    eq = (a >= pltpu.bitcast(thr, F32)) & jnp.logical_not(gt)
    need = cap - jnp.sum(jnp.where(gt, one, zero), axis=0, keepdims=True)
    ti = lax.broadcasted_iota(jnp.int32, (seq, seq), 0)
    tj = lax.broadcasted_iota(jnp.int32, (seq, seq), 1)
    before = jnp.where(tj < ti, 1.0, 0.0).astype(BF16)
    eq_rank = _dot(before, jnp.where(eq, one, zero).astype(BF16))
    sel = gt | (eq & (eq_rank < need))
    pos = _dot(before, jnp.where(sel, one, zero).astype(BF16))
    slot = jnp.where(sel, pos, -1.0)
    for g in range(ngroup):
        lanes = slice(g * LANES, (g + 1) * LANES)
        slot_t_ref[g] = slot[:, lanes]
        slot_r_ref[g] = slot[:, lanes].T
        aff_r_ref[g] = a[:, lanes].T


def _route(aff, cap):
    nreq, seq, _ = aff.shape
    ngroup = nreq // REQ_PER_GROUP
    tspec = pl.BlockSpec((ngroup, seq, LANES), lambda g: (0, 0, 0))
    rspec = pl.BlockSpec((ngroup, LANES, seq), lambda g: (0, 0, 0))
    return pl.pallas_call(
        functools.partial(_route_kernel, cap=cap),
        grid=(1,),
        in_specs=[pl.BlockSpec(aff.shape, lambda g: (0, 0, 0))],
        out_specs=[tspec, rspec, rspec],
        out_shape=[
            jax.ShapeDtypeStruct((ngroup, seq, LANES), F32),
            jax.ShapeDtypeStruct((ngroup, LANES, seq), F32),
            jax.ShapeDtypeStruct((ngroup, LANES, seq), F32),
        ],
        compiler_params=_params("arbitrary"),
        name="route",
    )(aff)


GATHER_ROWS = 512


def _gather_kernel(slot_ref, aff_ref, h_ref, x_ref, g_ref, *, cap):
    per_step, seq, _ = h_ref.shape
    nslot = N_EXPERTS * cap
    sub = lax.broadcasted_iota(jnp.int32, (cap, seq), 0).astype(F32)
    one, zero = jnp.ones((cap, seq), F32), jnp.zeros((cap, seq), F32)
    ei = lax.broadcasted_iota(jnp.int32, (N_EXPERTS, nslot), 0)
    si = lax.broadcasted_iota(jnp.int32, (N_EXPERTS, nslot), 1)
    own = (si >= ei * cap) & (si < (ei + 1) * cap)
    per_chunk = GATHER_ROWS // cap
    for r in range(per_step):
        h = h_ref[r]
        a = aff_ref[0, r * N_EXPERTS:(r + 1) * N_EXPERTS, :]
        a_hi = a.astype(BF16)
        r1 = a - a_hi.astype(F32)
        a_mid = r1.astype(BF16)
        a_lo = (r1 - a_mid.astype(F32)).astype(BF16)
        a3 = jnp.concatenate([a_hi, a_mid, a_lo], axis=0)
        g3 = []
        for c in range(nslot // GATHER_ROWS):
            hot = []
            for e in range(c * per_chunk, (c + 1) * per_chunk):
                row = jnp.broadcast_to(slot_ref[0, r * N_EXPERTS + e:r * N_EXPERTS + e + 1, :], (cap, seq))
                hot.append(jnp.where(row == sub, one, zero).astype(BF16))
            hot = jnp.concatenate(hot, axis=0)
            x = _dot(hot, h).astype(x_ref.dtype)
            for i in range(per_chunk):
                x_ref[c * per_chunk + i, r * cap:(r + 1) * cap, :] = x[i * cap:(i + 1) * cap]
            g3.append(lax.dot_general(a3, hot, _NT, preferred_element_type=F32))
        g3 = jnp.concatenate(g3, axis=1)
        gfull = g3[0:N_EXPERTS] + g3[N_EXPERTS:2 * N_EXPERTS] + g3[2 * N_EXPERTS:3 * N_EXPERTS]
        g_ref[r] = jnp.sum(jnp.where(own, gfull, 0.0), axis=0, keepdims=True)


def _gather(slot_r, aff_r, h2, cap, per_step):
    nreq, seq, _ = h2.shape
    nslot = N_EXPERTS * cap
    steps_per_group = REQ_PER_GROUP // per_step
    grp = lambda s: (s // steps_per_group, s % steps_per_group, 0)
    x, g = pl.pallas_call(
        functools.partial(_gather_kernel, cap=cap),
        grid=(nreq // per_step,),
        in_specs=[
            pl.BlockSpec((1, per_step * N_EXPERTS, seq), grp),
            pl.BlockSpec((1, per_step * N_EXPERTS, seq), grp),
            pl.BlockSpec((per_step, seq, D_MODEL), lambda s: (s, 0, 0)),
        ],
        out_specs=[
            pl.BlockSpec((N_EXPERTS, per_step * cap, D_MODEL), lambda s: (0, s, 0)),
            pl.BlockSpec((per_step, 1, nslot), lambda s: (s, 0, 0)),
        ],
        out_shape=[
            jax.ShapeDtypeStruct((N_EXPERTS, nreq * cap, D_MODEL), BF16),
            jax.ShapeDtypeStruct((nreq, 1, nslot), F32),
        ],
        compiler_params=_params("arbitrary"),
        name="gather",
    )(slot_r, aff_r, h2)
    return x, g


EXPERT_FCHUNK = 1024
EXPERT_ROWS = 512
GATE_ROWS = 256


def _expert_kernel(xc_ref, xl_ref, g_ref, wg_ref, wu_ref, wd_ref, yc_ref, yl_ref,
                   acc_ref):
    e = pl.program_id(0)
    f = pl.program_id(1)
    nf = pl.num_programs(1)
    half = xc_ref.shape[1]

    @pl.when((e == 0) & (f == 0))
    def _():
        acc_ref[...] = jnp.zeros_like(acc_ref)

    ri = lax.broadcasted_iota(jnp.int32, (GATE_ROWS, GATE_ROWS), 0)
    ci = lax.broadcasted_iota(jnp.int32, (GATE_ROWS, GATE_ROWS), 1)
    for part, (x_ref, y_ref) in enumerate(((xc_ref, yc_ref), (xl_ref, yl_ref))):
        for i in range(half // EXPERT_ROWS):
            x = x_ref[0, i * EXPERT_ROWS:(i + 1) * EXPERT_ROWS, :]
            gq = _dot(x, wg_ref[0].astype(BF16))
            uq = _dot(x, wu_ref[0].astype(BF16))
            hid = (gq * _sigmoid(gq) * uq).astype(BF16)
            r0 = part * half + i * EXPERT_ROWS
            prev = jnp.where(f == 0, 0.0, acc_ref[r0:r0 + EXPERT_ROWS, :])
            acc = prev + _dot(hid, wd_ref[0].astype(BF16))
            acc_ref[r0:r0 + EXPERT_ROWS, :] = acc
            for j in range(EXPERT_ROWS // GATE_ROWS):
                t = r0 // GATE_ROWS + j
                grow = jnp.broadcast_to(g_ref[0, t:t + 1, :], (GATE_ROWS, GATE_ROWS))
                gcol = jnp.sum(jnp.where(ri == ci, grow, 0.0), axis=-1, keepdims=True)
                y = acc[j * GATE_ROWS:(j + 1) * GATE_ROWS] * gcol
                y_ref[0, i * EXPERT_ROWS + j * GATE_ROWS:i * EXPERT_ROWS + (j + 1) * GATE_ROWS, :] = (
                    y.astype(y_ref.dtype))


def _experts(xc, xl, g, w_gate, w_up, w_down):
    ne, half, _ = xc.shape
    fc = EXPERT_FCHUNK
    xspec = pl.BlockSpec((1, half, D_MODEL), lambda e, f: (e, 0, 0))
    return pl.pallas_call(
        _expert_kernel,
        grid=(ne, D_EXPERT // fc),
        in_specs=[
            xspec, xspec,
            pl.BlockSpec((1,) + g.shape[1:], lambda e, f: (e, 0, 0)),
            pl.BlockSpec((1, D_MODEL, fc), lambda e, f: (e, 0, f)),
            pl.BlockSpec((1, D_MODEL, fc), lambda e, f: (e, 0, f)),
            pl.BlockSpec((1, fc, D_MODEL), lambda e, f: (e, f, 0)),
        ],
        out_specs=[xspec, xspec],
        out_shape=[jax.ShapeDtypeStruct(xc.shape, BF16), jax.ShapeDtypeStruct(xl.shape, BF16)],
        scratch_shapes=[pltpu.VMEM((2 * half, D_MODEL), F32)],
        compiler_params=_params("arbitrary", "arbitrary"),
        name="experts",
    )(xc, xl, g, w_gate, w_up, w_down)


COMBINE_ROWS = 256
COMBINE_LAT_ROWS = 512
MOE_CTX_PER_STEP = 4


def _combine_kernel(x1_ref, mods_ref, slot_ref, y_ref, gf_ref, o_ref, *, cap):
    per_step, tm, _ = x1_ref.shape
    nslot = N_EXPERTS * cap
    li = lax.broadcasted_iota(jnp.int32, (LANES, nslot), 0)
    si = lax.broadcasted_iota(jnp.int32, (LANES, nslot), 1)
    want = (lax.broadcasted_iota(jnp.int32, (COMBINE_ROWS, nslot), 1) & (cap - 1)).astype(F32)
    chains = [(r, t) for r in range(per_step) for t in range(tm // COMBINE_ROWS)]
    rows = {t: slice(t * COMBINE_ROWS, (t + 1) * COMBINE_ROWS) for _, t in chains}
    expand = {}
    for r in range(per_step):
        lane0 = ((pl.program_id(0) * per_step + r) % REQ_PER_GROUP) * N_EXPERTS
        expand[r] = jnp.where(li == lane0 + si // cap, 1.0, 0.0).astype(BF16)
    slot_exp = {(r, t): _dot(slot_ref[0, rows[t], :].astype(BF16), expand[r]) for r, t in chains}
    scat = {c: jnp.where(slot_exp[c] == want, 1.0, 0.0).astype(BF16) for c in chains}
    moe = {(r, t): _dot(scat[r, t], y_ref[:, r * cap:(r + 1) * cap, :].reshape(nslot, D_MODEL))
           for r, t in chains}
    for r, t in chains:
        x2 = x1_ref[r, rows[t], :] + mods_ref[0][5:6] * moe[r, t]
        o_ref[r, rows[t], :] = (x2 * lax.rsqrt(jnp.mean(x2 * x2, axis=-1, keepdims=True) + EPS)
                                * gf_ref[...])


def _combine(x1, mods, mod_row, slot_t, y, gf, cap, per_step, tm):
    nreq, seq, _ = x1.shape
    assert per_step == 1 or seq == tm
    tok = lambda s, j: (s, j, 0)
    return pl.pallas_call(
        functools.partial(_combine_kernel, cap=cap),
        grid=(nreq // per_step, seq // tm),
        in_specs=[
            pl.BlockSpec((per_step, tm, D_MODEL), tok),
            pl.BlockSpec((1, N_MOD, D_MODEL), lambda s, j: (mod_row(s * per_step), 0, 0)),
            pl.BlockSpec((1, tm, LANES), lambda s, j: (s * per_step // REQ_PER_GROUP, j, 0)),
            pl.BlockSpec((N_EXPERTS, per_step * cap, D_MODEL), lambda s, j: (0, s, 0)),
            pl.BlockSpec((1, D_MODEL), lambda s, j: (0, 0)),
        ],
        out_specs=pl.BlockSpec((per_step, tm, D_MODEL), tok),
        out_shape=jax.ShapeDtypeStruct(x1.shape, F32),
        compiler_params=_params("arbitrary", "arbitrary"),
        name="combine",
    )(x1, mods, slot_t, y, gf)


def kernel(x_prompt, x_sample, cache_ctx_k, cache_ctx_v, c, c_ctx, w_ada, b_ada, norm1_g, w_in,
           conv_dw_w, conv_dw_b, conv_ln_g, conv_ln_b, w_conv_pw, na_rpb, w_attn_o, w_out,
           norm2_g, w_router, w_gate_e, w_up_e, w_down_e, final_g):
    assert w_ada.shape[0] == 1, "single trunk layer"
    nctx, ctx_seq, _ = x_prompt.shape
    nlat, lat_seq, _ = x_sample.shape
    ctx_cap = EC_CAPACITY * ctx_seq // N_EXPERTS
    lat_cap = EC_CAPACITY * lat_seq // N_EXPERTS
    assert nctx * ctx_cap == nlat * lat_cap

    cond = jnp.zeros((2 * nlat, D_MODEL), F32).at[:nlat].set(c).at[nlat].set(c_ctx)
    mods = _ada(cond, w_ada[0], b_ada).reshape(2 * nlat, N_MOD, D_MODEL)
    ctx_row = lambda b: nlat
    lat_row = lambda b: b

    w_a = w_g = w_in[0].astype(BF16)
    w_pw = w_conv_pw[0].astype(BF16)
    w_ao = w_attn_o[0].astype(BF16)
    w_o = w_out[0].astype(BF16)
    w_r = w_router[0].astype(BF16)
    conv_args = (conv_dw_w[0], conv_dw_b, conv_ln_g, conv_ln_b)

    q_c, k_c, v_c, cact_c = _front(x_prompt, mods, ctx_row, norm1_g, w_a, *conv_args, F32,
                                   FRONT_CTX_PER_STEP)
    q_l, k_l, v_l, cact_l = _front(x_sample, mods, lat_row, norm1_g, w_a, *conv_args, BF16, 1)

    heads_c = _ctx_attn(q_c, k_c, v_c)
    tab = _bias_table(na_rpb[0])
    past = cache_ctx_k.shape[2]
    feat_major = lambda t: jnp.transpose(t[:, 0], (0, 2, 3, 1)).reshape(nlat, D_ATTN, past)
    heads_l = _na_attn(q_l, k_l, v_l, feat_major(cache_ctx_k), feat_major(cache_ctx_v), tab)

    merge_w = (w_g, w_pw, w_ao, w_o, w_r)
    flat = lambda t: t.reshape(1, nctx * ctx_seq, t.shape[-1])
    x1_c, h2_c, aff_c = _merge(flat(x_prompt), mods, ctx_row, norm1_g, norm2_g, flat(cact_c),
                               flat(heads_c), *merge_w)
    x1_c, h2_c, aff_c = (t.reshape(nctx, ctx_seq, t.shape[-1]) for t in (x1_c, h2_c, aff_c))
    x1_l, h2_l, aff_l = _merge(x_sample, mods, lat_row, norm1_g, norm2_g, cact_l, heads_l, *merge_w)

    slot_t_c, slot_r_c, aff_r_c = _route(aff_c, ctx_cap)
    slot_t_l, slot_r_l, aff_r_l = _route(aff_l, lat_cap)

    xg_c, g_c = _gather(slot_r_c, aff_r_c, h2_c, ctx_cap, MOE_CTX_PER_STEP)
    xg_l, g_l = _gather(slot_r_l, aff_r_l, h2_l, lat_cap, 1)

    def gate_rows(g, nreq, cap):
        g = jnp.transpose(g.reshape(nreq, N_EXPERTS, cap), (1, 0, 2))
        return g.reshape(N_EXPERTS, nreq * cap // GATE_ROWS, GATE_ROWS)

    gates = jnp.concatenate([gate_rows(g_c, nctx, ctx_cap), gate_rows(g_l, nlat, lat_cap)], axis=1)
    y_c, y_l = _experts(xg_c, xg_l, gates, w_gate_e[0], w_up_e[0], w_down_e[0])

    gf = final_g.reshape(1, D_MODEL)
    y_prompt = _combine(x1_c, mods, ctx_row, slot_t_c, y_c, gf, ctx_cap, MOE_CTX_PER_STEP, ctx_seq)
    y_sample = _combine(x1_l, mods, lat_row, slot_t_l, y_l, gf, lat_cap, 1, COMBINE_LAT_ROWS)

    state_k = k_c.reshape(nctx, 1, ctx_seq, N_HEADS, HEAD_DIM)
    state_v = v_c.reshape(nctx, 1, ctx_seq, N_HEADS, HEAD_DIM)
    return (y_prompt, y_sample, state_k, state_v)
```

```python
import functools

import jax
import jax.numpy as jnp
from jax import lax
from jax.experimental import pallas as pl
from jax.experimental.pallas import tpu as pltpu

F32 = jnp.float32
BF16 = jnp.bfloat16

D_MODEL = 1024
D_CONV = 512
CONV_WIDTH = 31
N_HEADS = 8
HEAD_DIM = 64
D_ATTN = N_HEADS * HEAD_DIM
GRID_W = 64
NA_ROWS = 8
NA_COLS = 16
N_EXPERTS = 16
D_EXPERT = 2048
EC_CAPACITY = 2
N_MOD = 6
EPS = 1e-6
D_PROJ_A = 2 * D_CONV + 3 * D_ATTN
D_IN = D_PROJ_A + 2 * D_MODEL
LANES = 128
REQ_PER_GROUP = LANES // N_EXPERTS
MASK_NEG = -1e30
VMEM_LIMIT = 56 * 1024 * 1024

_NT = (((1,), (1,)), ((), ()))


def _params(*sem):
    return pltpu.CompilerParams(dimension_semantics=sem, vmem_limit_bytes=VMEM_LIMIT)


def _dot(a, b):
    return jnp.dot(a, b, preferred_element_type=F32)


def _sigmoid(x):
    return 1.0 / (1.0 + jnp.exp(-x))


def _split_bf16(x):
    hi = x.astype(BF16)
    lo = (x - hi.astype(F32)).astype(BF16)
    return hi, lo


def _rms_mod(x, g, shift, scale):
    xn = x * lax.rsqrt(jnp.mean(x * x, axis=-1, keepdims=True) + EPS) * g
    return xn * (1.0 + scale) + shift


ADA_COLS = 1024


def _ada_kernel(cond_ref, w_ref, b_ref, o_ref):
    c = cond_ref[...]
    a = c * _sigmoid(c)
    ah, al = _split_bf16(a)
    wh, wl = _split_bf16(w_ref[...])
    o_ref[...] = _dot(ah, wh) + _dot(al, wh) + _dot(ah, wl) + b_ref[...]


def _ada(cond, w_ada, b_ada):
    n = w_ada.shape[1]
    bn = ADA_COLS
    return pl.pallas_call(
        _ada_kernel,
        grid=(n // bn,),
        in_specs=[
            pl.BlockSpec(cond.shape, lambda j: (0, 0)),
            pl.BlockSpec((D_MODEL, bn), lambda j: (0, j)),
            pl.BlockSpec((1, bn), lambda j: (0, j)),
        ],
        out_specs=pl.BlockSpec((cond.shape[0], bn), lambda j: (0, j)),
        out_shape=jax.ShapeDtypeStruct((cond.shape[0], n), F32),
        compiler_params=_params("arbitrary"),
        name="ada",
    )(cond, w_ada, b_ada)


CONV_PAD = 16
FRONT_ROWS = 256
CONV_ROWS = 128
SUBLANES = 8
FRONT_CTX_PER_STEP = 4


def _front_kernel(x_ref, mods_ref, g1_ref, w_ref, dww_ref, dwb_ref, lng_ref, lnb_ref,
                  q_ref, k_ref, v_ref, c_ref, *, seq):
    nreq = x_ref.shape[0]
    m = mods_ref[0]
    shift, scale = m[0:1], m[1:2]
    nchunk = seq // FRONT_ROWS
    per_chunk = FRONT_ROWS // CONV_ROWS
    zeros = jnp.zeros((CONV_PAD, D_CONV), F32)
    glu, stores = {}, []

    def proj(r, i):
        rows = slice(i * FRONT_ROWS, (i + 1) * FRONT_ROWS)
        x = x_ref[r, rows, :]
        h = _rms_mod(x, g1_ref[...], shift, scale).astype(BF16)
        u = _dot(h, w_ref[...])
        glu[r, i] = u[:, :D_CONV] * _sigmoid(u[:, D_CONV:2 * D_CONV])
        o = 2 * D_CONV
        stores.append((q_ref, r, rows, (u[:, o:o + D_ATTN] * (HEAD_DIM ** -0.5)).astype(q_ref.dtype)))
        stores.append((k_ref, r, rows, u[:, o + D_ATTN:o + 2 * D_ATTN].astype(k_ref.dtype)))
        stores.append((v_ref, r, rows, u[:, o + 2 * D_ATTN:o + 3 * D_ATTN].astype(v_ref.dtype)))

    def conv(r, j):
        i, o = j // per_chunk, (j % per_chunk) * CONV_ROWS
        g = glu[r, i]
        if o:
            top = g[o - CONV_PAD:o]
        else:
            top = glu[r, i - 1][FRONT_ROWS - CONV_PAD:] if i else zeros
        if o + CONV_ROWS < FRONT_ROWS:
            bottom = g[o + CONV_ROWS:o + CONV_ROWS + CONV_PAD]
        else:
            bottom = glu[r, i + 1][:CONV_PAD] if i + 1 < nchunk else zeros
        window = jnp.concatenate([top, g[o:o + CONV_ROWS], bottom], axis=0)
        cols = []
        for cb in range(D_CONV // LANES):
            cs = slice(cb * LANES, (cb + 1) * LANES)
            win = window[:, cs]
            acc = None
            for s in range(SUBLANES):
                nrow = CONV_ROWS + (SUBLANES if s else 0)
                z = None
                for a in range(2 * CONV_PAD // SUBLANES):
                    t = SUBLANES * a + s - (CONV_PAD - CONV_WIDTH // 2)
                    if 0 <= t < CONV_WIDTH:
                        term = win[SUBLANES * a:SUBLANES * a + nrow] * dww_ref[t:t + 1, cs]
                        z = term if z is None else z + term
                z = z[s:s + CONV_ROWS]
                acc = z if acc is None else acc + z
            cols.append(acc)
        y = jnp.concatenate(cols, axis=1) + dwb_ref[...]
        mu = jnp.mean(y, axis=-1, keepdims=True)
        yc = y - mu
        var = jnp.mean(yc * yc, axis=-1, keepdims=True)
        z = yc * lax.rsqrt(var + EPS) * lng_ref[...] + lnb_ref[...]
        stores.append((c_ref, r, slice(j * CONV_ROWS, (j + 1) * CONV_ROWS),
                       (z * _sigmoid(z)).astype(c_ref.dtype)))

    for r in range(nreq):
        for i in range(nchunk):
            proj(r, i)
    for r in range(nreq):
        for j in range(seq // CONV_ROWS):
            conv(r, j)
    for ref, r, rows, val in stores:
        ref[r, rows, :] = val


def _front(x, mods, mod_row, g1, w_a, dww, dwb, lng, lnb, kv_dtype, per_step):
    nreq, seq, _ = x.shape
    const2 = lambda b: (0, 0)
    tok = lambda b: (b, 0, 0)
    out_tok = pl.BlockSpec((per_step, seq, D_ATTN), tok)
    return pl.pallas_call(
        functools.partial(_front_kernel, seq=seq),
        grid=(nreq // per_step,),
        in_specs=[
            pl.BlockSpec((per_step, seq, D_MODEL), tok),
            pl.BlockSpec((1, N_MOD, D_MODEL), lambda b: (mod_row(b * per_step), 0, 0)),
            pl.BlockSpec((1, D_MODEL), const2),
            pl.BlockSpec((D_MODEL, D_PROJ_A), const2),
            pl.BlockSpec((CONV_WIDTH, D_CONV), const2),
            pl.BlockSpec((1, D_CONV), const2),
            pl.BlockSpec((1, D_CONV), const2),
            pl.BlockSpec((1, D_CONV), const2),
        ],
        out_specs=[out_tok, out_tok, out_tok, out_tok],
        out_shape=[
            jax.ShapeDtypeStruct((nreq, seq, D_ATTN), BF16),
            jax.ShapeDtypeStruct((nreq, seq, D_ATTN), kv_dtype),
            jax.ShapeDtypeStruct((nreq, seq, D_ATTN), kv_dtype),
            jax.ShapeDtypeStruct((nreq, seq, D_CONV), BF16),
        ],
        compiler_params=_params("arbitrary"),
        name="front",
    )(x, mods, g1, w_a, dww, dwb, lng, lnb)


def _pair_scores_q(q_pair):
    lane = lax.broadcasted_iota(jnp.int32, q_pair.shape, 1)
    zero = jnp.zeros_like(q_pair)
    return jnp.concatenate([jnp.where(lane < HEAD_DIM, q_pair, zero),
                            jnp.where(lane >= HEAD_DIM, q_pair, zero)], axis=0)


def _pair_merge(o2, t):
    lane = lax.broadcasted_iota(jnp.int32, (t, LANES), 1)
    return jnp.where(lane < HEAD_DIM, o2[:t], o2[t:])


CTX_ATTN_PER_STEP = 4


def _ctx_attn_kernel(q_ref, k_ref, v_ref, o_ref):
    nreq, seq, _ = q_ref.shape
    sls = [slice(p * LANES, (p + 1) * LANES) for p in range(N_HEADS // 2)]
    chains = [(r, p) for r in range(nreq) for p in range(N_HEADS // 2)]
    s = {(r, p): lax.dot_general(_pair_scores_q(q_ref[r, :, sls[p]]), k_ref[r, :, sls[p]].astype(BF16),
                                 _NT, preferred_element_type=F32) for r, p in chains}
    e = {c: jnp.exp(s[c] - jnp.max(s[c], axis=-1, keepdims=True)) for c in chains}
    den = {c: jnp.sum(e[c], axis=-1, keepdims=True) for c in chains}
    for r, p in chains:
        o2 = _dot(e[r, p].astype(BF16), v_ref[r, :, sls[p]].astype(BF16)) / den[r, p]
        o_ref[r, :, sls[p]] = _pair_merge(o2, seq).astype(o_ref.dtype)


def _ctx_attn(q, k, v):
    nreq, seq, _ = q.shape
    spec = pl.BlockSpec((CTX_ATTN_PER_STEP, seq, D_ATTN), lambda b: (b, 0, 0))
    return pl.pallas_call(
        _ctx_attn_kernel,
        grid=(nreq // CTX_ATTN_PER_STEP,),
        in_specs=[spec, spec, spec],
        out_specs=spec,
        out_shape=jax.ShapeDtypeStruct((nreq, seq, D_ATTN), BF16),
        compiler_params=_params("arbitrary"),
        name="ctx_attn",
    )(q, k, v)


N_DR = 2 * NA_ROWS - 1
N_DC = 2 * NA_COLS - 1


def _bias_kernel(rpb_ref, o_ref, t2_ref):
    h = pl.program_id(0)
    qi = lax.broadcasted_iota(jnp.int32, (GRID_W, LANES), 0)
    lane = lax.broadcasted_iota(jnp.int32, (GRID_W, LANES), 1)
    kc = lane & (GRID_W - 1)
    d = kc - qi
    cs = jnp.clip(qi - NA_COLS // 2, 0, GRID_W - NA_COLS)
    inside = (kc >= cs) & (kc < cs + NA_COLS)
    group = 5
    for dr0 in range(0, N_DR, group):
        ts = [jnp.full((GRID_W, LANES), MASK_NEG, F32)] * group
        for j in range(N_DC):
            hit = d == j - (NA_COLS - 1)
            ts = [jnp.where(hit, rpb_ref[h * (N_DR * N_DC) + (dr0 + g) * N_DC + j], ts[g])
                  for g in range(group)]
        for g in range(group):
            t2_ref[dr0 + g] = jnp.where(inside, ts[g], MASK_NEG)
    for o in range(NA_ROWS):
        for jj in range(NA_ROWS // 2):
            o_ref[0, o, :, jj * LANES:(jj + 1) * LANES] = jnp.where(
                lane < GRID_W, t2_ref[o + 2 * jj], t2_ref[o + 2 * jj + 1])


def _bias_table(rpb):
    return pl.pallas_call(
        _bias_kernel,
        grid=(N_HEADS,),
        in_specs=[pl.BlockSpec(memory_space=pltpu.SMEM)],
        out_specs=pl.BlockSpec((1, NA_ROWS, GRID_W, NA_ROWS * GRID_W), lambda h: (h, 0, 0, 0)),
        out_shape=jax.ShapeDtypeStruct((N_HEADS, NA_ROWS, GRID_W, NA_ROWS * GRID_W), F32),
        scratch_shapes=[pltpu.VMEM((N_DR, GRID_W, LANES), F32)],
        compiler_params=_params("arbitrary"),
        name="na_bias",
    )(rpb.reshape(-1))


NA_STEP_ROWS = 4


def _na_attn_kernel(q_ref, k_ref, v_ref, kct_ref, vct_ref, tab_ref, o_ref, kcb_ref, vcb_ref, *, rows):
    r = pl.program_id(1)

    @pl.when(r == 0)
    def _():
        kcb_ref[...] = kct_ref[0].astype(BF16)
        vcb_ref[...] = vct_ref[0].astype(BF16)

    nloc = NA_ROWS * GRID_W
    sls = [slice(p * LANES, (p + 1) * LANES) for p in range(N_HEADS // 2)]
    k0, off = [], []
    for i in range(NA_STEP_ROWS):
        qrow = r * NA_STEP_ROWS + i
        rs = jnp.clip(qrow - NA_ROWS // 2, 0, rows - NA_ROWS)
        k0.append(pl.multiple_of(rs * GRID_W, GRID_W))
        off.append(rs - qrow + (NA_ROWS - 1))
    chains = [(i, p) for i in range(NA_STEP_ROWS) for p in range(N_HEADS // 2)]
    qrows = [slice(i * GRID_W, (i + 1) * GRID_W) for i in range(NA_STEP_ROWS)]
    s_loc, s_ctx = {}, {}
    for i, p in chains:
        q2 = _pair_scores_q(q_ref[0, qrows[i], sls[p]])
        bias = jnp.concatenate([tab_ref[2 * p, off[i]], tab_ref[2 * p + 1, off[i]]], axis=0)
        kl = k_ref[0, pl.ds(k0[i], nloc), sls[p]]
        s_loc[i, p] = lax.dot_general(q2, kl, _NT, preferred_element_type=F32) + bias
        s_ctx[i, p] = _dot(q2, kcb_ref[sls[p], :])
    mx = {c: jnp.maximum(jnp.max(s_loc[c], axis=-1, keepdims=True),
                         jnp.max(s_ctx[c], axis=-1, keepdims=True)) for c in chains}
    e_loc = {c: jnp.exp(s_loc[c] - mx[c]) for c in chains}
    e_ctx = {c: jnp.exp(s_ctx[c] - mx[c]) for c in chains}
    den = {c: jnp.sum(e_loc[c], axis=-1, keepdims=True) + jnp.sum(e_ctx[c], axis=-1, keepdims=True)
           for c in chains}
    for i, p in chains:
        vl = v_ref[0, pl.ds(k0[i], nloc), sls[p]]
        o_ctx = lax.dot_general(e_ctx[i, p].astype(BF16), vcb_ref[sls[p], :], _NT,
                                preferred_element_type=F32)
        o2 = (_dot(e_loc[i, p].astype(BF16), vl) + o_ctx) / den[i, p]
        o_ref[0, qrows[i], sls[p]] = _pair_merge(o2, GRID_W).astype(o_ref.dtype)


def _na_attn(q, k, v, kc, vc, tab):
    nreq, seq, _ = q.shape
    rows = seq // GRID_W
    past = kc.shape[2]
    qspec = pl.BlockSpec((1, NA_STEP_ROWS * GRID_W, D_ATTN), lambda b, r: (b, r, 0))
    kvspec = pl.BlockSpec((1, seq, D_ATTN), lambda b, r: (b, 0, 0))
    cspec = pl.BlockSpec((1, D_ATTN, past), lambda b, r: (b, 0, 0))
    return pl.pallas_call(
        functools.partial(_na_attn_kernel, rows=rows),
        grid=(nreq, rows // NA_STEP_ROWS),
        in_specs=[qspec, kvspec, kvspec, cspec, cspec,
                  pl.BlockSpec(tab.shape, lambda b, r: (0, 0, 0, 0))],
        out_specs=qspec,
        out_shape=jax.ShapeDtypeStruct((nreq, seq, D_ATTN), BF16),
        scratch_shapes=[pltpu.VMEM((D_ATTN, past), BF16), pltpu.VMEM((D_ATTN, past), BF16)],
        compiler_params=_params("arbitrary", "arbitrary"),
        name="na_attn",
    )(q, k, v, kc, vc, tab)


MERGE_ROWS = 512
MERGE_SUB = 256


def _merge_kernel(x_ref, mods_ref, g1_ref, g2_ref, c_ref, a_ref, wg_ref, wpw_ref, wao_ref,
                  wout_ref, wr_ref, x1_ref, h2_ref, aff_ref):
    m = mods_ref[0]
    subs = range(MERGE_ROWS // MERGE_SUB)
    rows = [slice(i * MERGE_SUB, (i + 1) * MERGE_SUB) for i in subs]
    x = [x_ref[0, rows[i], :] for i in subs]
    h = [_rms_mod(x[i], g1_ref[...], m[0:1], m[1:2]).astype(BF16) for i in subs]
    gates = [_sigmoid(_dot(h[i], wg_ref[:, D_PROJ_A - D_IN // 2:])) for i in subs]
    conv_out = [_dot(c_ref[0, rows[i], :], wpw_ref[...]) for i in subs]
    attn_out = [_dot(a_ref[0, rows[i], :], wao_ref[...]) for i in subs]
    merged = [(gates[i][:, :D_MODEL] * conv_out[i] + gates[i][:, D_MODEL:] * attn_out[i]).astype(BF16)
              for i in subs]
    x1 = [x[i] + m[2:3] * _dot(merged[i], wout_ref[...]) for i in subs]
    h2 = [_rms_mod(x1[i], g2_ref[...], m[3:4], m[4:5]).astype(BF16) for i in subs]
    logits = [_dot(h2[i], wr_ref[...]) for i in subs]
    e = [jnp.exp(logits[i] - jnp.max(logits[i], axis=-1, keepdims=True)) for i in subs]
    for i in subs:
        x1_ref[0, rows[i], :] = x1[i]
        h2_ref[0, rows[i], :] = h2[i]
        aff_ref[0, rows[i], :] = e[i] / jnp.sum(e[i], axis=-1, keepdims=True)


def _merge(x, mods, mod_row, g1, g2, cact, heads, w_g, w_pw, w_ao, w_out, w_r):
    nreq, seq, _ = x.shape
    tm = MERGE_ROWS
    tok = lambda b, j: (b, j, 0)
    const2 = lambda b, j: (0, 0)
    return pl.pallas_call(
        _merge_kernel,
        grid=(nreq, seq // tm),
        in_specs=[
            pl.BlockSpec((1, tm, D_MODEL), tok),
            pl.BlockSpec((1, N_MOD, D_MODEL), lambda b, j: (mod_row(b), 0, 0)),
            pl.BlockSpec((1, D_MODEL), const2),
            pl.BlockSpec((1, D_MODEL), const2),
            pl.BlockSpec((1, tm, D_CONV), tok),
            pl.BlockSpec((1, tm, D_ATTN), tok),
            pl.BlockSpec((D_MODEL, D_IN // 2), lambda b, j: (0, 1)),
            pl.BlockSpec(w_pw.shape, const2),
            pl.BlockSpec(w_ao.shape, const2),
            pl.BlockSpec(w_out.shape, const2),
            pl.BlockSpec(w_r.shape, const2),
        ],
        out_specs=[
            pl.BlockSpec((1, tm, D_MODEL), tok),
            pl.BlockSpec((1, tm, D_MODEL), tok),
            pl.BlockSpec((1, tm, N_EXPERTS), tok),
        ],
        out_shape=[
            jax.ShapeDtypeStruct((nreq, seq, D_MODEL), F32),
            jax.ShapeDtypeStruct((nreq, seq, D_MODEL), BF16),
            jax.ShapeDtypeStruct((nreq, seq, N_EXPERTS), F32),
        ],
        compiler_params=_params("arbitrary", "arbitrary"),
        name="merge",
    )(x, mods, g1, g2, cact, heads, w_g, w_pw, w_ao, w_out, w_r)


def _route_kernel(aff_ref, slot_t_ref, slot_r_ref, aff_r_ref, *, cap):
    nreq, seq, _ = aff_ref.shape
    ngroup = nreq // REQ_PER_GROUP
    a = jnp.concatenate([aff_ref[b] for b in range(nreq)], axis=1)
    one, zero = jnp.ones_like(a), jnp.zeros_like(a)

    def search(i, thr):
        cand = thr | jnp.left_shift(jnp.int32(1), 30 - i)
        cnt = jnp.sum(jnp.where(a >= pltpu.bitcast(cand, F32), one, zero), axis=0, keepdims=True)
        return jnp.where(cnt >= cap, cand, thr)

    thr = lax.fori_loop(0, 31, search, jnp.zeros((1, ngroup * LANES), jnp.int32))
    gt = a >= pltpu.bitcast(thr + 1, F32)
    eq = (a >= pltpu.bitcast(thr, F32)) & jnp.logical_not(gt)
    need = cap - jnp.sum(jnp.where(gt, one, zero), axis=0, keepdims=True)
    ti = lax.broadcasted_iota(jnp.int32, (seq, seq), 0)
    tj = lax.broadcasted_iota(jnp.int32, (seq, seq), 1)
    before = jnp.where(tj < ti, 1.0, 0.0).astype(BF16)
    eq_rank = _dot(before, jnp.where(eq, one, zero).astype(BF16))
    sel = gt | (eq & (eq_rank < need))
    pos = _dot(before, jnp.where(sel, one, zero).astype(BF16))
    slot = jnp.where(sel, pos, -1.0)
    for g in range(ngroup):
        lanes = slice(g * LANES, (g + 1) * LANES)
        slot_t_ref[g] = slot[:, lanes]
        slot_r_ref[g] = slot[:, lanes].T
        aff_r_ref[g] = a[:, lanes].T


def _route(aff, cap):
    nreq, seq, _ = aff.shape
    ngroup = nreq // REQ_PER_GROUP
    tspec = pl.BlockSpec((ngroup, seq, LANES), lambda g: (0, 0, 0))
    rspec = pl.BlockSpec((ngroup, LANES, seq), lambda g: (0, 0, 0))
    return pl.pallas_call(
        functools.partial(_route_kernel, cap=cap),
        grid=(1,),
        in_specs=[pl.BlockSpec(aff.shape, lambda g: (0, 0, 0))],
        out_specs=[tspec, rspec, rspec],
        out_shape=[
            jax.ShapeDtypeStruct((ngroup, seq, LANES), F32),
            jax.ShapeDtypeStruct((ngroup, LANES, seq), F32),
            jax.ShapeDtypeStruct((ngroup, LANES, seq), F32),
        ],
        compiler_params=_params("arbitrary"),
        name="route",
    )(aff)


GATHER_ROWS = 512


def _gather_kernel(slot_ref, aff_ref, h_ref, x_ref, g_ref, *, cap):
    per_step, seq, _ = h_ref.shape
    nslot = N_EXPERTS * cap
    sub = lax.broadcasted_iota(jnp.int32, (cap, seq), 0).astype(F32)
    one, zero = jnp.ones((cap, seq), F32), jnp.zeros((cap, seq), F32)
    ei = lax.broadcasted_iota(jnp.int32, (N_EXPERTS, nslot), 0)
    si = lax.broadcasted_iota(jnp.int32, (N_EXPERTS, nslot), 1)
    own = (si >= ei * cap) & (si < (ei + 1) * cap)
    per_chunk = GATHER_ROWS // cap
    for r in range(per_step):
        h = h_ref[r]
        a = aff_ref[0, r * N_EXPERTS:(r + 1) * N_EXPERTS, :]
        a_hi = a.astype(BF16)
        r1 = a - a_hi.astype(F32)
        a_mid = r1.astype(BF16)
        a_lo = (r1 - a_mid.astype(F32)).astype(BF16)
        a3 = jnp.concatenate([a_hi, a_mid, a_lo], axis=0)
        g3 = []
        for c in range(nslot // GATHER_ROWS):
            hot = []
            for e in range(c * per_chunk, (c + 1) * per_chunk):
                row = jnp.broadcast_to(slot_ref[0, r * N_EXPERTS + e:r * N_EXPERTS + e + 1, :], (cap, seq))
                hot.append(jnp.where(row == sub, one, zero).astype(BF16))
            hot = jnp.concatenate(hot, axis=0)
            x = _dot(hot, h).astype(x_ref.dtype)
            for i in range(per_chunk):
                x_ref[c * per_chunk + i, r * cap:(r + 1) * cap, :] = x[i * cap:(i + 1) * cap]
            g3.append(lax.dot_general(a3, hot, _NT, preferred_element_type=F32))
        g3 = jnp.concatenate(g3, axis=1)
        gfull = g3[0:N_EXPERTS] + g3[N_EXPERTS:2 * N_EXPERTS] + g3[2 * N_EXPERTS:3 * N_EXPERTS]
        g_ref[r] = jnp.sum(jnp.where(own, gfull, 0.0), axis=0, keepdims=True)


def _gather(slot_r, aff_r, h2, cap, per_step):
    nreq, seq, _ = h2.shape
    nslot = N_EXPERTS * cap
    steps_per_group = REQ_PER_GROUP // per_step
    grp = lambda s: (s // steps_per_group, s % steps_per_group, 0)
    x, g = pl.pallas_call(
        functools.partial(_gather_kernel, cap=cap),
        grid=(nreq // per_step,),
        in_specs=[
            pl.BlockSpec((1, per_step * N_EXPERTS, seq), grp),
            pl.BlockSpec((1, per_step * N_EXPERTS, seq), grp),
            pl.BlockSpec((per_step, seq, D_MODEL), lambda s: (s, 0, 0)),
        ],
        out_specs=[
            pl.BlockSpec((N_EXPERTS, per_step * cap, D_MODEL), lambda s: (0, s, 0)),
            pl.BlockSpec((per_step, 1, nslot), lambda s: (s, 0, 0)),
        ],
        out_shape=[
            jax.ShapeDtypeStruct((N_EXPERTS, nreq * cap, D_MODEL), BF16),
            jax.ShapeDtypeStruct((nreq, 1, nslot), F32),
        ],
        compiler_params=_params("arbitrary"),
        name="gather",
    )(slot_r, aff_r, h2)
    return x, g


EXPERT_FCHUNK = 1024
EXPERT_ROWS = 512
GATE_ROWS = 256


def _expert_kernel(xc_ref, xl_ref, g_ref, wg_ref, wu_ref, wd_ref, yc_ref, yl_ref,
                   acc_ref):
    e = pl.program_id(0)
    f = pl.program_id(1)
    nf = pl.num_programs(1)
    half = xc_ref.shape[1]

    @pl.when((e == 0) & (f == 0))
    def _():
        acc_ref[...] = jnp.zeros_like(acc_ref)

    ri = lax.broadcasted_iota(jnp.int32, (GATE_ROWS, GATE_ROWS), 0)
    ci = lax.broadcasted_iota(jnp.int32, (GATE_ROWS, GATE_ROWS), 1)
    for part, (x_ref, y_ref) in enumerate(((xc_ref, yc_ref), (xl_ref, yl_ref))):
        for i in range(half // EXPERT_ROWS):
            x = x_ref[0, i * EXPERT_ROWS:(i + 1) * EXPERT_ROWS, :]
            gq = _dot(x, wg_ref[0].astype(BF16))
            uq = _dot(x, wu_ref[0].astype(BF16))
            hid = (gq * _sigmoid(gq) * uq).astype(BF16)
            r0 = part * half + i * EXPERT_ROWS
            prev = jnp.where(f == 0, 0.0, acc_ref[r0:r0 + EXPERT_ROWS, :])
            acc = prev + _dot(hid, wd_ref[0].astype(BF16))
            acc_ref[r0:r0 + EXPERT_ROWS, :] = acc
            for j in range(EXPERT_ROWS // GATE_ROWS):
                t = r0 // GATE_ROWS + j
                grow = jnp.broadcast_to(g_ref[0, t:t + 1, :], (GATE_ROWS, GATE_ROWS))
                gcol = jnp.sum(jnp.where(ri == ci, grow, 0.0), axis=-1, keepdims=True)
                y = acc[j * GATE_ROWS:(j + 1) * GATE_ROWS] * gcol
                y_ref[0, i * EXPERT_ROWS + j * GATE_ROWS:i * EXPERT_ROWS + (j + 1) * GATE_ROWS, :] = (
                    y.astype(y_ref.dtype))


def _experts(xc, xl, g, w_gate, w_up, w_down):
    ne, half, _ = xc.shape
    fc = EXPERT_FCHUNK
    xspec = pl.BlockSpec((1, half, D_MODEL), lambda e, f: (e, 0, 0))
    return pl.pallas_call(
        _expert_kernel,
        grid=(ne, D_EXPERT // fc),
        in_specs=[
            xspec, xspec,
            pl.BlockSpec((1,) + g.shape[1:], lambda e, f: (e, 0, 0)),
            pl.BlockSpec((1, D_MODEL, fc), lambda e, f: (e, 0, f)),
            pl.BlockSpec((1, D_MODEL, fc), lambda e, f: (e, 0, f)),
            pl.BlockSpec((1, fc, D_MODEL), lambda e, f: (e, f, 0)),
        ],
        out_specs=[xspec, xspec],
        out_shape=[jax.ShapeDtypeStruct(xc.shape, BF16), jax.ShapeDtypeStruct(xl.shape, BF16)],
        scratch_shapes=[pltpu.VMEM((2 * half, D_MODEL), F32)],
        compiler_params=_params("arbitrary", "arbitrary"),
        name="experts",
    )(xc, xl, g, w_gate, w_up, w_down)


COMBINE_ROWS = 256
COMBINE_LAT_ROWS = 1024
MOE_CTX_PER_STEP = 8


def _combine_kernel(x1_ref, mods_ref, slot_ref, y_ref, gf_ref, o_ref, *, cap):
    per_step, tm, _ = x1_ref.shape
    nslot = N_EXPERTS * cap
    li = lax.broadcasted_iota(jnp.int32, (LANES, nslot), 0)
    si = lax.broadcasted_iota(jnp.int32, (LANES, nslot), 1)
    want = (lax.broadcasted_iota(jnp.int32, (COMBINE_ROWS, nslot), 1) & (cap - 1)).astype(F32)
    chains = [(r, t) for r in range(per_step) for t in range(tm // COMBINE_ROWS)]
    rows = {t: slice(t * COMBINE_ROWS, (t + 1) * COMBINE_ROWS) for _, t in chains}
    expand = {}
    for r in range(per_step):
        lane0 = ((pl.program_id(0) * per_step + r) % REQ_PER_GROUP) * N_EXPERTS
        expand[r] = jnp.where(li == lane0 + si // cap, 1.0, 0.0).astype(BF16)
    slot_exp = {(r, t): _dot(slot_ref[0, rows[t], :].astype(BF16), expand[r]) for r, t in chains}
    scat = {c: jnp.where(slot_exp[c] == want, 1.0, 0.0).astype(BF16) for c in chains}
    moe = {(r, t): _dot(scat[r, t], y_ref[:, r * cap:(r + 1) * cap, :].reshape(nslot, D_MODEL))
           for r, t in chains}
    for r, t in chains:
        x2 = x1_ref[r, rows[t], :] + mods_ref[0][5:6] * moe[r, t]
        o_ref[r, rows[t], :] = (x2 * lax.rsqrt(jnp.mean(x2 * x2, axis=-1, keepdims=True) + EPS)
                                * gf_ref[...])


def _combine(x1, mods, mod_row, slot_t, y, gf, cap, per_step, tm):
    nreq, seq, _ = x1.shape
    assert per_step == 1 or seq == tm
    tok = lambda s, j: (s, j, 0)
    return pl.pallas_call(
        functools.partial(_combine_kernel, cap=cap),
        grid=(nreq // per_step, seq // tm),
        in_specs=[
            pl.BlockSpec((per_step, tm, D_MODEL), tok),
            pl.BlockSpec((1, N_MOD, D_MODEL), lambda s, j: (mod_row(s * per_step), 0, 0)),
            pl.BlockSpec((1, tm, LANES), lambda s, j: (s * per_step // REQ_PER_GROUP, j, 0)),
            pl.BlockSpec((N_EXPERTS, per_step * cap, D_MODEL), lambda s, j: (0, s, 0)),
            pl.BlockSpec((1, D_MODEL), lambda s, j: (0, 0)),
        ],
        out_specs=pl.BlockSpec((per_step, tm, D_MODEL), tok),
        out_shape=jax.ShapeDtypeStruct(x1.shape, F32),
        compiler_params=_params("arbitrary", "arbitrary"),
        name="combine",
    )(x1, mods, slot_t, y, gf)


def kernel(x_prompt, x_sample, cache_ctx_k, cache_ctx_v, c, c_ctx, w_ada, b_ada, norm1_g, w_in,
           conv_dw_w, conv_dw_b, conv_ln_g, conv_ln_b, w_conv_pw, na_rpb, w_attn_o, w_out,
           norm2_g, w_router, w_gate_e, w_up_e, w_down_e, final_g):
    assert w_ada.shape[0] == 1, "single trunk layer"
    nctx, ctx_seq, _ = x_prompt.shape
    nlat, lat_seq, _ = x_sample.shape
    ctx_cap = EC_CAPACITY * ctx_seq // N_EXPERTS
    lat_cap = EC_CAPACITY * lat_seq // N_EXPERTS
    assert nctx * ctx_cap == nlat * lat_cap

    cond = jnp.zeros((2 * nlat, D_MODEL), F32).at[:nlat].set(c).at[nlat].set(c_ctx)
    mods = _ada(cond, w_ada[0], b_ada).reshape(2 * nlat, N_MOD, D_MODEL)
    ctx_row = lambda b: nlat
    lat_row = lambda b: b

    w_a = w_g = w_in[0].astype(BF16)
    w_pw = w_conv_pw[0].astype(BF16)
    w_ao = w_attn_o[0].astype(BF16)
    w_o = w_out[0].astype(BF16)
    w_r = w_router[0].astype(BF16)
    conv_args = (conv_dw_w[0], conv_dw_b, conv_ln_g, conv_ln_b)

    q_c, k_c, v_c, cact_c = _front(x_prompt, mods, ctx_row, norm1_g, w_a, *conv_args, F32,
                                   FRONT_CTX_PER_STEP)
    q_l, k_l, v_l, cact_l = _front(x_sample, mods, lat_row, norm1_g, w_a, *conv_args, BF16, 1)

    heads_c = _ctx_attn(q_c, k_c, v_c)
    tab = _bias_table(na_rpb[0])
    past = cache_ctx_k.shape[2]
    feat_major = lambda t: jnp.transpose(t[:, 0], (0, 2, 3, 1)).reshape(nlat, D_ATTN, past)
    heads_l = _na_attn(q_l, k_l, v_l, feat_major(cache_ctx_k), feat_major(cache_ctx_v), tab)

    merge_w = (w_g, w_pw, w_ao, w_o, w_r)
    flat = lambda t: t.reshape(1, nctx * ctx_seq, t.shape[-1])
    x1_c, h2_c, aff_c = _merge(flat(x_prompt), mods, ctx_row, norm1_g, norm2_g, flat(cact_c),
                               flat(heads_c), *merge_w)
    x1_c, h2_c, aff_c = (t.reshape(nctx, ctx_seq, t.shape[-1]) for t in (x1_c, h2_c, aff_c))
    x1_l, h2_l, aff_l = _merge(x_sample, mods, lat_row, norm1_g, norm2_g, cact_l, heads_l, *merge_w)

    slot_t_c, slot_r_c, aff_r_c = _route(aff_c, ctx_cap)
    slot_t_l, slot_r_l, aff_r_l = _route(aff_l, lat_cap)

    xg_c, g_c = _gather(slot_r_c, aff_r_c, h2_c, ctx_cap, MOE_CTX_PER_STEP)
    xg_l, g_l = _gather(slot_r_l, aff_r_l, h2_l, lat_cap, 1)

    def gate_rows(g, nreq, cap):
        g = jnp.transpose(g.reshape(nreq, N_EXPERTS, cap), (1, 0, 2))
        return g.reshape(N_EXPERTS, nreq * cap // GATE_ROWS, GATE_ROWS)

    gates = jnp.concatenate([gate_rows(g_c, nctx, ctx_cap), gate_rows(g_l, nlat, lat_cap)], axis=1)
    y_c, y_l = _experts(xg_c, xg_l, gates, w_gate_e[0], w_up_e[0], w_down_e[0])

    gf = final_g.reshape(1, D_MODEL)
    y_prompt = _combine(x1_c, mods, ctx_row, slot_t_c, y_c, gf, ctx_cap, MOE_CTX_PER_STEP, ctx_seq)
    y_sample = _combine(x1_l, mods, lat_row, slot_t_l, y_l, gf, lat_cap, 1, COMBINE_LAT_ROWS)

    state_k = k_c.reshape(nctx, 1, ctx_seq, N_HEADS, HEAD_DIM)
    state_v = v_c.reshape(nctx, 1, ctx_seq, N_HEADS, HEAD_DIM)
    return (y_prompt, y_sample, state_k, state_v)
```

```python
import functools

import jax
import jax.numpy as jnp
from jax import lax
from jax.experimental import pallas as pl
from jax.experimental.pallas import tpu as pltpu

F32 = jnp.float32
BF16 = jnp.bfloat16

D_MODEL = 1024
D_CONV = 512
CONV_WIDTH = 31
N_HEADS = 8
HEAD_DIM = 64
D_ATTN = N_HEADS * HEAD_DIM
GRID_W = 64
NA_ROWS = 8
NA_COLS = 16
N_EXPERTS = 16
D_EXPERT = 2048
EC_CAPACITY = 2
N_MOD = 6
EPS = 1e-6
D_PROJ_A = 2 * D_CONV + 3 * D_ATTN
D_IN = D_PROJ_A + 2 * D_MODEL
LANES = 128
REQ_PER_GROUP = LANES // N_EXPERTS
MASK_NEG = -1e30
VMEM_LIMIT = 56 * 1024 * 1024

_NT = (((1,), (1,)), ((), ()))


def _params(*sem):
    return pltpu.CompilerParams(dimension_semantics=sem, vmem_limit_bytes=VMEM_LIMIT)


def _dot(a, b):
    return jnp.dot(a, b, preferred_element_type=F32)


def _sigmoid(x):
    return 1.0 / (1.0 + jnp.exp(-x))


def _split_bf16(x):
    hi = x.astype(BF16)
    lo = (x - hi.astype(F32)).astype(BF16)
    return hi, lo


def _rms_mod(x, g, shift, scale):
    xn = x * lax.rsqrt(jnp.mean(x * x, axis=-1, keepdims=True) + EPS) * g
    return xn * (1.0 + scale) + shift


ADA_COLS = 1024


def _ada_kernel(cond_ref, w_ref, b_ref, o_ref):
    c = cond_ref[...]
    a = c * _sigmoid(c)
    ah, al = _split_bf16(a)
    wh, wl = _split_bf16(w_ref[...])
    o_ref[...] = _dot(ah, wh) + _dot(al, wh) + _dot(ah, wl) + b_ref[...]


def _ada(cond, w_ada, b_ada):
    n = w_ada.shape[1]
    bn = ADA_COLS
    return pl.pallas_call(
        _ada_kernel,
        grid=(n // bn,),
        in_specs=[
            pl.BlockSpec(cond.shape, lambda j: (0, 0)),
            pl.BlockSpec((D_MODEL, bn), lambda j: (0, j)),
            pl.BlockSpec((1, bn), lambda j: (0, j)),
        ],
        out_specs=pl.BlockSpec((cond.shape[0], bn), lambda j: (0, j)),
        out_shape=jax.ShapeDtypeStruct((cond.shape[0], n), F32),
        compiler_params=_params("arbitrary"),
        name="ada",
    )(cond, w_ada, b_ada)


CONV_PAD = 16
FRONT_ROWS = 256
CONV_ROWS = 128
SUBLANES = 8
FRONT_CTX_PER_STEP = 4
FRONT_LAT_PER_STEP = 2


def _front_kernel(x_ref, mods_ref, g1_ref, w_ref, dww_ref, dwb_ref, lng_ref, lnb_ref,
                  q_ref, k_ref, v_ref, c_ref, *, seq):
    nreq = x_ref.shape[0]
    nchunk = seq // FRONT_ROWS
    per_chunk = FRONT_ROWS // CONV_ROWS
    zeros = jnp.zeros((CONV_PAD, D_CONV), F32)
    glu, stores = {}, []

    def proj(r, i):
        rows = slice(i * FRONT_ROWS, (i + 1) * FRONT_ROWS)
        x = x_ref[r, rows, :]
        m = mods_ref[r % mods_ref.shape[0]]
        h = _rms_mod(x, g1_ref[...], m[0:1], m[1:2]).astype(BF16)
        u = _dot(h, w_ref[...])
        glu[r, i] = u[:, :D_CONV] * _sigmoid(u[:, D_CONV:2 * D_CONV])
        o = 2 * D_CONV
        stores.append((q_ref, r, rows, (u[:, o:o + D_ATTN] * (HEAD_DIM ** -0.5)).astype(q_ref.dtype)))
        stores.append((k_ref, r, rows, u[:, o + D_ATTN:o + 2 * D_ATTN].astype(k_ref.dtype)))
        stores.append((v_ref, r, rows, u[:, o + 2 * D_ATTN:o + 3 * D_ATTN].astype(v_ref.dtype)))

    def conv(r, j):
        i, o = j // per_chunk, (j % per_chunk) * CONV_ROWS
        g = glu[r, i]
        if o:
            top = g[o - CONV_PAD:o]
        else:
            top = glu[r, i - 1][FRONT_ROWS - CONV_PAD:] if i else zeros
        if o + CONV_ROWS < FRONT_ROWS:
            bottom = g[o + CONV_ROWS:o + CONV_ROWS + CONV_PAD]
        else:
            bottom = glu[r, i + 1][:CONV_PAD] if i + 1 < nchunk else zeros
        window = jnp.concatenate([top, g[o:o + CONV_ROWS], bottom], axis=0)
        cols = []
        for cb in range(D_CONV // LANES):
            cs = slice(cb * LANES, (cb + 1) * LANES)
            win = window[:, cs]
            acc = None
            for s in range(SUBLANES):
                nrow = CONV_ROWS + (SUBLANES if s else 0)
                z = None
                for a in range(2 * CONV_PAD // SUBLANES):
                    t = SUBLANES * a + s - (CONV_PAD - CONV_WIDTH // 2)
                    if 0 <= t < CONV_WIDTH:
                        term = win[SUBLANES * a:SUBLANES * a + nrow] * dww_ref[t:t + 1, cs]
                        z = term if z is None else z + term
                z = z[s:s + CONV_ROWS]
                acc = z if acc is None else acc + z
            cols.append(acc)
        y = jnp.concatenate(cols, axis=1) + dwb_ref[...]
        mu = jnp.mean(y, axis=-1, keepdims=True)
        yc = y - mu
        var = jnp.mean(yc * yc, axis=-1, keepdims=True)
        z = yc * lax.rsqrt(var + EPS) * lng_ref[...] + lnb_ref[...]
        stores.append((c_ref, r, slice(j * CONV_ROWS, (j + 1) * CONV_ROWS),
                       (z * _sigmoid(z)).astype(c_ref.dtype)))

    for r in range(nreq):
        for i in range(nchunk):
            proj(r, i)
    for r in range(nreq):
        for j in range(seq // CONV_ROWS):
            conv(r, j)
    for ref, r, rows, val in stores:
        ref[r, rows, :] = val


def _front(x, mods, shared_row, g1, w_a, dww, dwb, lng, lnb, kv_dtype, per_step):
    nreq, seq, _ = x.shape
    const2 = lambda b: (0, 0)
    tok = lambda b: (b, 0, 0)
    out_tok = pl.BlockSpec((per_step, seq, D_ATTN), tok)
    if shared_row is None:
        mod_spec = pl.BlockSpec((per_step, N_MOD, D_MODEL), tok)
    else:
        mod_spec = pl.BlockSpec((1, N_MOD, D_MODEL), lambda b: (shared_row, 0, 0))
    return pl.pallas_call(
        functools.partial(_front_kernel, seq=seq),
        grid=(nreq // per_step,),
        in_specs=[
            pl.BlockSpec((per_step, seq, D_MODEL), tok),
            mod_spec,
            pl.BlockSpec((1, D_MODEL), const2),
            pl.BlockSpec((D_MODEL, D_PROJ_A), const2),
            pl.BlockSpec((CONV_WIDTH, D_CONV), const2),
            pl.BlockSpec((1, D_CONV), const2),
            pl.BlockSpec((1, D_CONV), const2),
            pl.BlockSpec((1, D_CONV), const2),
        ],
        out_specs=[out_tok, out_tok, out_tok, out_tok],
        out_shape=[
            jax.ShapeDtypeStruct((nreq, seq, D_ATTN), BF16),
            jax.ShapeDtypeStruct((nreq, seq, D_ATTN), kv_dtype),
            jax.ShapeDtypeStruct((nreq, seq, D_ATTN), kv_dtype),
            jax.ShapeDtypeStruct((nreq, seq, D_CONV), BF16),
        ],
        compiler_params=_params("arbitrary"),
        name="front",
    )(x, mods, g1, w_a, dww, dwb, lng, lnb)


def _pair_scores_q(q_pair):
    lane = lax.broadcasted_iota(jnp.int32, q_pair.shape, 1)
    zero = jnp.zeros_like(q_pair)
    return jnp.concatenate([jnp.where(lane < HEAD_DIM, q_pair, zero),
                            jnp.where(lane >= HEAD_DIM, q_pair, zero)], axis=0)


def _pair_merge(o2, t):
    lane = lax.broadcasted_iota(jnp.int32, (t, LANES), 1)
    return jnp.where(lane < HEAD_DIM, o2[:t], o2[t:])


CTX_ATTN_PER_STEP = 4


def _ctx_attn_kernel(q_ref, k_ref, v_ref, o_ref):
    nreq, seq, _ = q_ref.shape
    sls = [slice(p * LANES, (p + 1) * LANES) for p in range(N_HEADS // 2)]
    chains = [(r, p) for r in range(nreq) for p in range(N_HEADS // 2)]
    s = {(r, p): lax.dot_general(_pair_scores_q(q_ref[r, :, sls[p]]), k_ref[r, :, sls[p]].astype(BF16),
                                 _NT, preferred_element_type=F32) for r, p in chains}
    e = {c: jnp.exp(s[c] - jnp.max(s[c], axis=-1, keepdims=True)) for c in chains}
    den = {c: jnp.sum(e[c], axis=-1, keepdims=True) for c in chains}
    for r, p in chains:
        o2 = _dot(e[r, p].astype(BF16), v_ref[r, :, sls[p]].astype(BF16)) / den[r, p]
        o_ref[r, :, sls[p]] = _pair_merge(o2, seq).astype(o_ref.dtype)


def _ctx_attn(q, k, v):
    nreq, seq, _ = q.shape
    spec = pl.BlockSpec((CTX_ATTN_PER_STEP, seq, D_ATTN), lambda b: (b, 0, 0))
    return pl.pallas_call(
        _ctx_attn_kernel,
        grid=(nreq // CTX_ATTN_PER_STEP,),
        in_specs=[spec, spec, spec],
        out_specs=spec,
        out_shape=jax.ShapeDtypeStruct((nreq, seq, D_ATTN), BF16),
        compiler_params=_params("arbitrary"),
        name="ctx_attn",
    )(q, k, v)


N_DR = 2 * NA_ROWS - 1
N_DC = 2 * NA_COLS - 1


def _bias_kernel(rpb_ref, o_ref, t2_ref):
    h = pl.program_id(0)
    qi = lax.broadcasted_iota(jnp.int32, (GRID_W, LANES), 0)
    lane = lax.broadcasted_iota(jnp.int32, (GRID_W, LANES), 1)
    kc = lane & (GRID_W - 1)
    d = kc - qi
    cs = jnp.clip(qi - NA_COLS // 2, 0, GRID_W - NA_COLS)
    inside = (kc >= cs) & (kc < cs + NA_COLS)
    group = 5
    for dr0 in range(0, N_DR, group):
        ts = [jnp.full((GRID_W, LANES), MASK_NEG, F32)] * group
        for j in range(N_DC):
            hit = d == j - (NA_COLS - 1)
            ts = [jnp.where(hit, rpb_ref[h * (N_DR * N_DC) + (dr0 + g) * N_DC + j], ts[g])
                  for g in range(group)]
        for g in range(group):
            t2_ref[dr0 + g] = jnp.where(inside, ts[g], MASK_NEG)
    for o in range(NA_ROWS):
        for jj in range(NA_ROWS // 2):
            o_ref[0, o, :, jj * LANES:(jj + 1) * LANES] = jnp.where(
                lane < GRID_W, t2_ref[o + 2 * jj], t2_ref[o + 2 * jj + 1])


def _bias_table(rpb):
    return pl.pallas_call(
        _bias_kernel,
        grid=(N_HEADS,),
        in_specs=[pl.BlockSpec(memory_space=pltpu.SMEM)],
        out_specs=pl.BlockSpec((1, NA_ROWS, GRID_W, NA_ROWS * GRID_W), lambda h: (h, 0, 0, 0)),
        out_shape=jax.ShapeDtypeStruct((N_HEADS, NA_ROWS, GRID_W, NA_ROWS * GRID_W), F32),
        scratch_shapes=[pltpu.VMEM((N_DR, GRID_W, LANES), F32)],
        compiler_params=_params("arbitrary"),
        name="na_bias",
    )(rpb.reshape(-1))


NA_STEP_ROWS = 4


def _na_attn_kernel(q_ref, k_ref, v_ref, kct_ref, vct_ref, tab_ref, o_ref, kcb_ref, vcb_ref, *, rows):
    r = pl.program_id(1)

    @pl.when(r == 0)
    def _():
        kcb_ref[...] = kct_ref[0].astype(BF16)
        vcb_ref[...] = vct_ref[0].astype(BF16)

    nloc = NA_ROWS * GRID_W
    sls = [slice(p * LANES, (p + 1) * LANES) for p in range(N_HEADS // 2)]
    k0, off = [], []
    for i in range(NA_STEP_ROWS):
        qrow = r * NA_STEP_ROWS + i
        rs = jnp.clip(qrow - NA_ROWS // 2, 0, rows - NA_ROWS)
        k0.append(pl.multiple_of(rs * GRID_W, GRID_W))
        off.append(rs - qrow + (NA_ROWS - 1))
    chains = [(i, p) for i in range(NA_STEP_ROWS) for p in range(N_HEADS // 2)]
    qrows = [slice(i * GRID_W, (i + 1) * GRID_W) for i in range(NA_STEP_ROWS)]
    s_loc, s_ctx = {}, {}
    for i, p in chains:
        q2 = _pair_scores_q(q_ref[0, qrows[i], sls[p]])
        bias = jnp.concatenate([tab_ref[2 * p, off[i]], tab_ref[2 * p + 1, off[i]]], axis=0)
        kl = k_ref[0, pl.ds(k0[i], nloc), sls[p]]
        s_loc[i, p] = lax.dot_general(q2, kl, _NT, preferred_element_type=F32) + bias
        s_ctx[i, p] = _dot(q2, kcb_ref[sls[p], :])
    mx = {c: jnp.maximum(jnp.max(s_loc[c], axis=-1, keepdims=True),
                         jnp.max(s_ctx[c], axis=-1, keepdims=True)) for c in chains}
    e_loc = {c: jnp.exp(s_loc[c] - mx[c]) for c in chains}
    e_ctx = {c: jnp.exp(s_ctx[c] - mx[c]) for c in chains}
    den = {c: jnp.sum(e_loc[c], axis=-1, keepdims=True) + jnp.sum(e_ctx[c], axis=-1, keepdims=True)
           for c in chains}
    for i, p in chains:
        vl = v_ref[0, pl.ds(k0[i], nloc), sls[p]]
        o_ctx = lax.dot_general(e_ctx[i, p].astype(BF16), vcb_ref[sls[p], :], _NT,
                                preferred_element_type=F32)
        o2 = (_dot(e_loc[i, p].astype(BF16), vl) + o_ctx) / den[i, p]
        o_ref[0, qrows[i], sls[p]] = _pair_merge(o2, GRID_W).astype(o_ref.dtype)


def _na_attn(q, k, v, kc, vc, tab):
    nreq, seq, _ = q.shape
    rows = seq // GRID_W
    past = kc.shape[2]
    qspec = pl.BlockSpec((1, NA_STEP_ROWS * GRID_W, D_ATTN), lambda b, r: (b, r, 0))
    kvspec = pl.BlockSpec((1, seq, D_ATTN), lambda b, r: (b, 0, 0))
    cspec = pl.BlockSpec((1, D_ATTN, past), lambda b, r: (b, 0, 0))
    return pl.pallas_call(
        functools.partial(_na_attn_kernel, rows=rows),
        grid=(nreq, rows // NA_STEP_ROWS),
        in_specs=[qspec, kvspec, kvspec, cspec, cspec,
                  pl.BlockSpec(tab.shape, lambda b, r: (0, 0, 0, 0))],
        out_specs=qspec,
        out_shape=jax.ShapeDtypeStruct((nreq, seq, D_ATTN), BF16),
        scratch_shapes=[pltpu.VMEM((D_ATTN, past), BF16), pltpu.VMEM((D_ATTN, past), BF16)],
        compiler_params=_params("arbitrary", "arbitrary"),
        name="na_attn",
    )(q, k, v, kc, vc, tab)


MERGE_ROWS = 512
MERGE_SUB = 256


def _merge_kernel(x_ref, mods_ref, g1_ref, g2_ref, c_ref, a_ref, wg_ref, wpw_ref, wao_ref,
                  wout_ref, wr_ref, x1_ref, h2_ref, aff_ref):
    m = mods_ref[0]
    subs = range(MERGE_ROWS // MERGE_SUB)
    rows = [slice(i * MERGE_SUB, (i + 1) * MERGE_SUB) for i in subs]
    x = [x_ref[0, rows[i], :] for i in subs]
    h = [_rms_mod(x[i], g1_ref[...], m[0:1], m[1:2]).astype(BF16) for i in subs]
    gates = [_sigmoid(_dot(h[i], wg_ref[:, D_PROJ_A - D_IN // 2:])) for i in subs]
    conv_out = [_dot(c_ref[0, rows[i], :], wpw_ref[...]) for i in subs]
    attn_out = [_dot(a_ref[0, rows[i], :], wao_ref[...]) for i in subs]
    merged = [(gates[i][:, :D_MODEL] * conv_out[i] + gates[i][:, D_MODEL:] * attn_out[i]).astype(BF16)
              for i in subs]
    x1 = [x[i] + m[2:3] * _dot(merged[i], wout_ref[...]) for i in subs]
    h2 = [_rms_mod(x1[i], g2_ref[...], m[3:4], m[4:5]).astype(BF16) for i in subs]
    logits = [_dot(h2[i], wr_ref[...]) for i in subs]
    e = [jnp.exp(logits[i] - jnp.max(logits[i], axis=-1, keepdims=True)) for i in subs]
    for i in subs:
        x1_ref[0, rows[i], :] = x1[i]
        h2_ref[0, rows[i], :] = h2[i]
        aff_ref[0, rows[i], :] = e[i] / jnp.sum(e[i], axis=-1, keepdims=True)


def _merge(x, mods, mod_row, g1, g2, cact, heads, w_g, w_pw, w_ao, w_out, w_r):
    nreq, seq, _ = x.shape
    tm = MERGE_ROWS
    tok = lambda b, j: (b, j, 0)
    const2 = lambda b, j: (0, 0)
    return pl.pallas_call(
        _merge_kernel,
        grid=(nreq, seq // tm),
        in_specs=[
            pl.BlockSpec((1, tm, D_MODEL), tok),
            pl.BlockSpec((1, N_MOD, D_MODEL), lambda b, j: (mod_row(b), 0, 0)),
            pl.BlockSpec((1, D_MODEL), const2),
            pl.BlockSpec((1, D_MODEL), const2),
            pl.BlockSpec((1, tm, D_CONV), tok),
            pl.BlockSpec((1, tm, D_ATTN), tok),
            pl.BlockSpec((D_MODEL, D_IN // 2), lambda b, j: (0, 1)),
            pl.BlockSpec(w_pw.shape, const2),
            pl.BlockSpec(w_ao.shape, const2),
            pl.BlockSpec(w_out.shape, const2),
            pl.BlockSpec(w_r.shape, const2),
        ],
        out_specs=[
            pl.BlockSpec((1, tm, D_MODEL), tok),
            pl.BlockSpec((1, tm, D_MODEL), tok),
            pl.BlockSpec((1, tm, N_EXPERTS), tok),
        ],
        out_shape=[
            jax.ShapeDtypeStruct((nreq, seq, D_MODEL), F32),
            jax.ShapeDtypeStruct((nreq, seq, D_MODEL), BF16),
            jax.ShapeDtypeStruct((nreq, seq, N_EXPERTS), F32),
        ],
        compiler_params=_params("arbitrary", "arbitrary"),
        name="merge",
    )(x, mods, g1, g2, cact, heads, w_g, w_pw, w_ao, w_out, w_r)


def _route_kernel(aff_ref, slot_t_ref, slot_r_ref, aff_r_ref, *, cap):
    nreq, seq, _ = aff_ref.shape
    ngroup = nreq // REQ_PER_GROUP
    a = jnp.concatenate([aff_ref[b] for b in range(nreq)], axis=1)
    one, zero = jnp.ones_like(a), jnp.zeros_like(a)

    def search(i, thr):
        cand = thr | jnp.left_shift(jnp.int32(1), 30 - i)
        cnt = jnp.sum(jnp.where(a >= pltpu.bitcast(cand, F32), one, zero), axis=0, keepdims=True)
        return jnp.where(cnt >= cap, cand, thr)

    thr = lax.fori_loop(0, 31, search, jnp.zeros((1, ngroup * LANES), jnp.int32))
    gt = a >= pltpu.bitcast(thr + 1, F32)
    eq = (a >= pltpu.bitcast(thr, F32)) & jnp.logical_not(gt)
    need = cap - jnp.sum(jnp.where(gt, one, zero), axis=0, keepdims=True)
    ti = lax.broadcasted_iota(jnp.int32, (seq, seq), 0)
    tj = lax.broadcasted_iota(jnp.int32, (seq, seq), 1)
    before = jnp.where(tj < ti, 1.0, 0.0).astype(BF16)
    eq_rank = _dot(before, jnp.where(eq, one, zero).astype(BF16))
    sel = gt | (eq & (eq_rank < need))
    pos = _dot(before, jnp.where(sel, one, zero).astype(BF16))
    slot = jnp.where(sel, pos, -1.0)
    for g in range(ngroup):
        lanes = slice(g * LANES, (g + 1) * LANES)
        slot_t_ref[g] = slot[:, lanes]
        slot_r_ref[g] = slot[:, lanes].T
        aff_r_ref[g] = a[:, lanes].T


def _route(aff, cap):
    nreq, seq, _ = aff.shape
    ngroup = nreq // REQ_PER_GROUP
    tspec = pl.BlockSpec((ngroup, seq, LANES), lambda g: (0, 0, 0))
    rspec = pl.BlockSpec((ngroup, LANES, seq), lambda g: (0, 0, 0))
    return pl.pallas_call(
        functools.partial(_route_kernel, cap=cap),
        grid=(1,),
        in_specs=[pl.BlockSpec(aff.shape, lambda g: (0, 0, 0))],
        out_specs=[tspec, rspec, rspec],
        out_shape=[
            jax.ShapeDtypeStruct((ngroup, seq, LANES), F32),
            jax.ShapeDtypeStruct((ngroup, LANES, seq), F32),
            jax.ShapeDtypeStruct((ngroup, LANES, seq), F32),
        ],
        compiler_params=_params("arbitrary"),
        name="route",
    )(aff)


GATHER_ROWS = 512


def _gather_kernel(slot_ref, aff_ref, h_ref, x_ref, g_ref, *, cap):
    per_step, seq, _ = h_ref.shape
    nslot = N_EXPERTS * cap
    sub = lax.broadcasted_iota(jnp.int32, (cap, seq), 0).astype(F32)
    one, zero = jnp.ones((cap, seq), F32), jnp.zeros((cap, seq), F32)
    ei = lax.broadcasted_iota(jnp.int32, (N_EXPERTS, nslot), 0)
    si = lax.broadcasted_iota(jnp.int32, (N_EXPERTS, nslot), 1)
    own = (si >= ei * cap) & (si < (ei + 1) * cap)
    per_chunk = GATHER_ROWS // cap
    for r in range(per_step):
        h = h_ref[r]
        a = aff_ref[0, r * N_EXPERTS:(r + 1) * N_EXPERTS, :]
        a_hi = a.astype(BF16)
        r1 = a - a_hi.astype(F32)
        a_mid = r1.astype(BF16)
        a_lo = (r1 - a_mid.astype(F32)).astype(BF16)
        a3 = jnp.concatenate([a_hi, a_mid, a_lo], axis=0)
        g3 = []
        for c in range(nslot // GATHER_ROWS):
            hot = []
            for e in range(c * per_chunk, (c + 1) * per_chunk):
                row = jnp.broadcast_to(slot_ref[0, r * N_EXPERTS + e:r * N_EXPERTS + e + 1, :], (cap, seq))
                hot.append(jnp.where(row == sub, one, zero).astype(BF16))
            hot = jnp.concatenate(hot, axis=0)
            x = _dot(hot, h).astype(x_ref.dtype)
            for i in range(per_chunk):
                x_ref[c * per_chunk + i, r * cap:(r + 1) * cap, :] = x[i * cap:(i + 1) * cap]
            g3.append(lax.dot_general(a3, hot, _NT, preferred_element_type=F32))
        g3 = jnp.concatenate(g3, axis=1)
        gfull = g3[0:N_EXPERTS] + g3[N_EXPERTS:2 * N_EXPERTS] + g3[2 * N_EXPERTS:3 * N_EXPERTS]
        g_ref[r] = jnp.sum(jnp.where(own, gfull, 0.0), axis=0, keepdims=True)


def _gather(slot_r, aff_r, h2, cap, per_step):
    nreq, seq, _ = h2.shape
    nslot = N_EXPERTS * cap
    steps_per_group = REQ_PER_GROUP // per_step
    grp = lambda s: (s // steps_per_group, s % steps_per_group, 0)
    x, g = pl.pallas_call(
        functools.partial(_gather_kernel, cap=cap),
        grid=(nreq // per_step,),
        in_specs=[
            pl.BlockSpec((1, per_step * N_EXPERTS, seq), grp),
            pl.BlockSpec((1, per_step * N_EXPERTS, seq), grp),
            pl.BlockSpec((per_step, seq, D_MODEL), lambda s: (s, 0, 0)),
        ],
        out_specs=[
            pl.BlockSpec((N_EXPERTS, per_step * cap, D_MODEL), lambda s: (0, s, 0)),
            pl.BlockSpec((per_step, 1, nslot), lambda s: (s, 0, 0)),
        ],
        out_shape=[
            jax.ShapeDtypeStruct((N_EXPERTS, nreq * cap, D_MODEL), BF16),
            jax.ShapeDtypeStruct((nreq, 1, nslot), F32),
        ],
        compiler_params=_params("arbitrary"),
        name="gather",
    )(slot_r, aff_r, h2)
    return x, g


EXPERT_FCHUNK = 1024
EXPERT_ROWS = 512
GATE_ROWS = 256


def _expert_kernel(xc_ref, xl_ref, g_ref, wg_ref, wu_ref, wd_ref, yc_ref, yl_ref,
                   acc_ref):
    e = pl.program_id(0)
    f = pl.program_id(1)
    nf = pl.num_programs(1)
    half = xc_ref.shape[1]

    @pl.when((e == 0) & (f == 0))
    def _():
        acc_ref[...] = jnp.zeros_like(acc_ref)

    ri = lax.broadcasted_iota(jnp.int32, (GATE_ROWS, GATE_ROWS), 0)
    ci = lax.broadcasted_iota(jnp.int32, (GATE_ROWS, GATE_ROWS), 1)
    for part, (x_ref, y_ref) in enumerate(((xc_ref, yc_ref), (xl_ref, yl_ref))):
        for i in range(half // EXPERT_ROWS):
            x = x_ref[0, i * EXPERT_ROWS:(i + 1) * EXPERT_ROWS, :]
            gq = _dot(x, wg_ref[0].astype(BF16))
            uq = _dot(x, wu_ref[0].astype(BF16))
            hid = (gq * _sigmoid(gq) * uq).astype(BF16)
            r0 = part * half + i * EXPERT_ROWS
            prev = jnp.where(f == 0, 0.0, acc_ref[r0:r0 + EXPERT_ROWS, :])
            acc = prev + _dot(hid, wd_ref[0].astype(BF16))
            acc_ref[r0:r0 + EXPERT_ROWS, :] = acc
            for j in range(EXPERT_ROWS // GATE_ROWS):
                t = r0 // GATE_ROWS + j
                grow = jnp.broadcast_to(g_ref[0, t:t + 1, :], (GATE_ROWS, GATE_ROWS))
                gcol = jnp.sum(jnp.where(ri == ci, grow, 0.0), axis=-1, keepdims=True)
                y = acc[j * GATE_ROWS:(j + 1) * GATE_ROWS] * gcol
                y_ref[0, i * EXPERT_ROWS + j * GATE_ROWS:i * EXPERT_ROWS + (j + 1) * GATE_ROWS, :] = (
                    y.astype(y_ref.dtype))


def _experts(xc, xl, g, w_gate, w_up, w_down):
    ne, half, _ = xc.shape
    fc = EXPERT_FCHUNK
    xspec = pl.BlockSpec((1, half, D_MODEL), lambda e, f: (e, 0, 0))
    return pl.pallas_call(
        _expert_kernel,
        grid=(ne, D_EXPERT // fc),
        in_specs=[
            xspec, xspec,
            pl.BlockSpec((1,) + g.shape[1:], lambda e, f: (e, 0, 0)),
            pl.BlockSpec((1, D_MODEL, fc), lambda e, f: (e, 0, f)),
            pl.BlockSpec((1, D_MODEL, fc), lambda e, f: (e, 0, f)),
            pl.BlockSpec((1, fc, D_MODEL), lambda e, f: (e, f, 0)),
        ],
        out_specs=[xspec, xspec],
        out_shape=[jax.ShapeDtypeStruct(xc.shape, BF16), jax.ShapeDtypeStruct(xl.shape, BF16)],
        scratch_shapes=[pltpu.VMEM((2 * half, D_MODEL), F32)],
        compiler_params=_params("arbitrary", "arbitrary"),
        name="experts",
    )(xc, xl, g, w_gate, w_up, w_down)


COMBINE_ROWS = 256
COMBINE_LAT_ROWS = 1024
MOE_CTX_PER_STEP = 8


def _combine_kernel(x1_ref, mods_ref, slot_ref, y_ref, gf_ref, o_ref, *, cap):
    per_step, tm, _ = x1_ref.shape
    nslot = N_EXPERTS * cap
    li = lax.broadcasted_iota(jnp.int32, (LANES, nslot), 0)
    si = lax.broadcasted_iota(jnp.int32, (LANES, nslot), 1)
    want = (lax.broadcasted_iota(jnp.int32, (COMBINE_ROWS, nslot), 1) & (cap - 1)).astype(F32)
    chains = [(r, t) for r in range(per_step) for t in range(tm // COMBINE_ROWS)]
    rows = {t: slice(t * COMBINE_ROWS, (t + 1) * COMBINE_ROWS) for _, t in chains}
    expand = {}
    for r in range(per_step):
        lane0 = ((pl.program_id(0) * per_step + r) % REQ_PER_GROUP) * N_EXPERTS
        expand[r] = jnp.where(li == lane0 + si // cap, 1.0, 0.0).astype(BF16)
    slot_exp = {(r, t): _dot(slot_ref[0, rows[t], :].astype(BF16), expand[r]) for r, t in chains}
    scat = {c: jnp.where(slot_exp[c] == want, 1.0, 0.0).astype(BF16) for c in chains}
    moe = {(r, t): _dot(scat[r, t], y_ref[:, r * cap:(r + 1) * cap, :].reshape(nslot, D_MODEL))
           for r, t in chains}
    for r, t in chains:
        x2 = x1_ref[r, rows[t], :] + mods_ref[0][5:6] * moe[r, t]
        o_ref[r, rows[t], :] = (x2 * lax.rsqrt(jnp.mean(x2 * x2, axis=-1, keepdims=True) + EPS)
                                * gf_ref[...])


def _combine(x1, mods, mod_row, slot_t, y, gf, cap, per_step, tm):
    nreq, seq, _ = x1.shape
    assert per_step == 1 or seq == tm
    tok = lambda s, j: (s, j, 0)
    return pl.pallas_call(
        functools.partial(_combine_kernel, cap=cap),
        grid=(nreq // per_step, seq // tm),
        in_specs=[
            pl.BlockSpec((per_step, tm, D_MODEL), tok),
            pl.BlockSpec((1, N_MOD, D_MODEL), lambda s, j: (mod_row(s * per_step), 0, 0)),
            pl.BlockSpec((1, tm, LANES), lambda s, j: (s * per_step // REQ_PER_GROUP, j, 0)),
            pl.BlockSpec((N_EXPERTS, per_step * cap, D_MODEL), lambda s, j: (0, s, 0)),
            pl.BlockSpec((1, D_MODEL), lambda s, j: (0, 0)),
        ],
        out_specs=pl.BlockSpec((per_step, tm, D_MODEL), tok),
        out_shape=jax.ShapeDtypeStruct(x1.shape, F32),
        compiler_params=_params("arbitrary", "arbitrary"),
        name="combine",
    )(x1, mods, slot_t, y, gf)


def kernel(x_prompt, x_sample, cache_ctx_k, cache_ctx_v, c, c_ctx, w_ada, b_ada, norm1_g, w_in,
           conv_dw_w, conv_dw_b, conv_ln_g, conv_ln_b, w_conv_pw, na_rpb, w_attn_o, w_out,
           norm2_g, w_router, w_gate_e, w_up_e, w_down_e, final_g):
    assert w_ada.shape[0] == 1, "single trunk layer"
    nctx, ctx_seq, _ = x_prompt.shape
    nlat, lat_seq, _ = x_sample.shape
    ctx_cap = EC_CAPACITY * ctx_seq // N_EXPERTS
    lat_cap = EC_CAPACITY * lat_seq // N_EXPERTS
    assert nctx * ctx_cap == nlat * lat_cap

    cond = jnp.zeros((2 * nlat, D_MODEL), F32).at[:nlat].set(c).at[nlat].set(c_ctx)
    mods = _ada(cond, w_ada[0], b_ada).reshape(2 * nlat, N_MOD, D_MODEL)
    ctx_row = lambda b: nlat
    lat_row = lambda b: b

    w_a = w_g = w_in[0].astype(BF16)
    w_pw = w_conv_pw[0].astype(BF16)
    w_ao = w_attn_o[0].astype(BF16)
    w_o = w_out[0].astype(BF16)
    w_r = w_router[0].astype(BF16)
    conv_args = (conv_dw_w[0], conv_dw_b, conv_ln_g, conv_ln_b)

    q_c, k_c, v_c, cact_c = _front(x_prompt, mods, nlat, norm1_g, w_a, *conv_args, F32,
                                   FRONT_CTX_PER_STEP)
    q_l, k_l, v_l, cact_l = _front(x_sample, mods, None, norm1_g, w_a, *conv_args, BF16,
                                   FRONT_LAT_PER_STEP)

    heads_c = _ctx_attn(q_c, k_c, v_c)
    tab = _bias_table(na_rpb[0])
    past = cache_ctx_k.shape[2]
    feat_major = lambda t: jnp.transpose(t[:, 0], (0, 2, 3, 1)).reshape(nlat, D_ATTN, past)
    heads_l = _na_attn(q_l, k_l, v_l, feat_major(cache_ctx_k), feat_major(cache_ctx_v), tab)

    merge_w = (w_g, w_pw, w_ao, w_o, w_r)
    flat = lambda t: t.reshape(1, nctx * ctx_seq, t.shape[-1])
    x1_c, h2_c, aff_c = _merge(flat(x_prompt), mods, ctx_row, norm1_g, norm2_g, flat(cact_c),
                               flat(heads_c), *merge_w)
    x1_c, h2_c, aff_c = (t.reshape(nctx, ctx_seq, t.shape[-1]) for t in (x1_c, h2_c, aff_c))
    x1_l, h2_l, aff_l = _merge(x_sample, mods, lat_row, norm1_g, norm2_g, cact_l, heads_l, *merge_w)

    slot_t_c, slot_r_c, aff_r_c = _route(aff_c, ctx_cap)
    slot_t_l, slot_r_l, aff_r_l = _route(aff_l, lat_cap)

    xg_c, g_c = _gather(slot_r_c, aff_r_c, h2_c, ctx_cap, MOE_CTX_PER_STEP)
    xg_l, g_l = _gather(slot_r_l, aff_r_l, h2_l, lat_cap, 2)

    def gate_rows(g, nreq, cap):
        g = jnp.transpose(g.reshape(nreq, N_EXPERTS, cap), (1, 0, 2))
        return g.reshape(N_EXPERTS, nreq * cap // GATE_ROWS, GATE_ROWS)

    gates = jnp.concatenate([gate_rows(g_c, nctx, ctx_cap), gate_rows(g_l, nlat, lat_cap)], axis=1)
    y_c, y_l = _experts(xg_c, xg_l, gates, w_gate_e[0], w_up_e[0], w_down_e[0])

    gf = final_g.reshape(1, D_MODEL)
    y_prompt = _combine(x1_c, mods, ctx_row, slot_t_c, y_c, gf, ctx_cap, MOE_CTX_PER_STEP, ctx_seq)
    y_sample = _combine(x1_l, mods, lat_row, slot_t_l, y_l, gf, lat_cap, 1, COMBINE_LAT_ROWS)

    state_k = k_c.reshape(nctx, 1, ctx_seq, N_HEADS, HEAD_DIM)
    state_v = v_c.reshape(nctx, 1, ctx_seq, N_HEADS, HEAD_DIM)
    return (y_prompt, y_sample, state_k, state_v)
```

```python
import functools

import jax
import jax.numpy as jnp
from jax import lax
from jax.experimental import pallas as pl
from jax.experimental.pallas import tpu as pltpu

F32 = jnp.float32
BF16 = jnp.bfloat16

D_MODEL = 1024
D_CONV = 512
CONV_WIDTH = 31
N_HEADS = 8
HEAD_DIM = 64
D_ATTN = N_HEADS * HEAD_DIM
GRID_W = 64
NA_ROWS = 8
NA_COLS = 16
N_EXPERTS = 16
D_EXPERT = 2048
EC_CAPACITY = 2
N_MOD = 6
EPS = 1e-6
D_PROJ_A = 2 * D_CONV + 3 * D_ATTN
D_IN = D_PROJ_A + 2 * D_MODEL
LANES = 128
REQ_PER_GROUP = LANES // N_EXPERTS
MASK_NEG = -1e30
LOG2_E = 1.4426950408889634
Q_SCALE = HEAD_DIM ** -0.5 * LOG2_E
VMEM_LIMIT = 56 * 1024 * 1024

_NT = (((1,), (1,)), ((), ()))


def _params(*sem):
    return pltpu.CompilerParams(dimension_semantics=sem, vmem_limit_bytes=VMEM_LIMIT)


def _dot(a, b):
    return jnp.dot(a, b, preferred_element_type=F32)


def _sigmoid(x):
    return 1.0 / (1.0 + jnp.exp(-x))


def _split_bf16(x):
    hi = x.astype(BF16)
    lo = (x - hi.astype(F32)).astype(BF16)
    return hi, lo


def _rms_mod(x, g, shift, scale):
    xn = x * lax.rsqrt(jnp.mean(x * x, axis=-1, keepdims=True) + EPS) * g
    return xn * (1.0 + scale) + shift


ADA_COLS = 1024


def _ada_kernel(cond_ref, w_ref, b_ref, o_ref):
    c = cond_ref[...]
    a = c * _sigmoid(c)
    ah, al = _split_bf16(a)
    wh, wl = _split_bf16(w_ref[...])
    o_ref[...] = _dot(ah, wh) + _dot(al, wh) + _dot(ah, wl) + b_ref[...]


def _ada(cond, w_ada, b_ada):
    n = w_ada.shape[1]
    bn = ADA_COLS
    return pl.pallas_call(
        _ada_kernel,
        grid=(n // bn,),
        in_specs=[
            pl.BlockSpec(cond.shape, lambda j: (0, 0)),
            pl.BlockSpec((D_MODEL, bn), lambda j: (0, j)),
            pl.BlockSpec((1, bn), lambda j: (0, j)),
        ],
        out_specs=pl.BlockSpec((cond.shape[0], bn), lambda j: (0, j)),
        out_shape=jax.ShapeDtypeStruct((cond.shape[0], n), F32),
        compiler_params=_params("arbitrary"),
        name="ada",
    )(cond, w_ada, b_ada)


CONV_PAD = 16
FRONT_ROWS = 256
CONV_ROWS = 128
SUBLANES = 8
FRONT_CTX_PER_STEP = 4


def _front_kernel(x_ref, mods_ref, g1_ref, w_ref, dww_ref, dwb_ref, lng_ref, lnb_ref,
                  q_ref, k_ref, v_ref, c_ref, *, seq):
    nreq = x_ref.shape[0]
    m = mods_ref[0]
    shift, scale = m[0:1], m[1:2]
    nchunk = seq // FRONT_ROWS
    per_chunk = FRONT_ROWS // CONV_ROWS
    zeros = jnp.zeros((CONV_PAD, D_CONV), F32)
    glu, stores = {}, []

    def proj(r, i):
        rows = slice(i * FRONT_ROWS, (i + 1) * FRONT_ROWS)
        x = x_ref[r, rows, :]
        h = _rms_mod(x, g1_ref[...], shift, scale).astype(BF16)
        u = _dot(h, w_ref[...])
        glu[r, i] = u[:, :D_CONV] * _sigmoid(u[:, D_CONV:2 * D_CONV])
        o = 2 * D_CONV
        stores.append((q_ref, r, rows, (u[:, o:o + D_ATTN] * Q_SCALE).astype(q_ref.dtype)))
        stores.append((k_ref, r, rows, u[:, o + D_ATTN:o + 2 * D_ATTN].astype(k_ref.dtype)))
        stores.append((v_ref, r, rows, u[:, o + 2 * D_ATTN:o + 3 * D_ATTN].astype(v_ref.dtype)))

    def conv(r, j):
        i, o = j // per_chunk, (j % per_chunk) * CONV_ROWS
        g = glu[r, i]
        if o:
            top = g[o - CONV_PAD:o]
        else:
            top = glu[r, i - 1][FRONT_ROWS - CONV_PAD:] if i else zeros
        if o + CONV_ROWS < FRONT_ROWS:
            bottom = g[o + CONV_ROWS:o + CONV_ROWS + CONV_PAD]
        else:
            bottom = glu[r, i + 1][:CONV_PAD] if i + 1 < nchunk else zeros
        window = jnp.concatenate([top, g[o:o + CONV_ROWS], bottom], axis=0)
        cols = []
        for cb in range(D_CONV // LANES):
            cs = slice(cb * LANES, (cb + 1) * LANES)
            win = window[:, cs]
            acc = None
            for s in range(SUBLANES):
                nrow = CONV_ROWS + (SUBLANES if s else 0)
                z = None
                for a in range(2 * CONV_PAD // SUBLANES):
                    t = SUBLANES * a + s - (CONV_PAD - CONV_WIDTH // 2)
                    if 0 <= t < CONV_WIDTH:
                        term = win[SUBLANES * a:SUBLANES * a + nrow] * dww_ref[t:t + 1, cs]
                        z = term if z is None else z + term
                z = z[s:s + CONV_ROWS]
                acc = z if acc is None else acc + z
            cols.append(acc)
        y = jnp.concatenate(cols, axis=1) + dwb_ref[...]
        mu = jnp.mean(y, axis=-1, keepdims=True)
        yc = y - mu
        var = jnp.mean(yc * yc, axis=-1, keepdims=True)
        z = yc * lax.rsqrt(var + EPS) * lng_ref[...] + lnb_ref[...]
        stores.append((c_ref, r, slice(j * CONV_ROWS, (j + 1) * CONV_ROWS),
                       (z * _sigmoid(z)).astype(c_ref.dtype)))

    for r in range(nreq):
        for i in range(nchunk):
            proj(r, i)
    for r in range(nreq):
        for j in range(seq // CONV_ROWS):
            conv(r, j)
    for ref, r, rows, val in stores:
        ref[r, rows, :] = val


def _front(x, mods, mod_row, g1, w_a, dww, dwb, lng, lnb, kv_dtype, per_step):
    nreq, seq, _ = x.shape
    const2 = lambda b: (0, 0)
    tok = lambda b: (b, 0, 0)
    out_tok = pl.BlockSpec((per_step, seq, D_ATTN), tok)
    return pl.pallas_call(
        functools.partial(_front_kernel, seq=seq),
        grid=(nreq // per_step,),
        in_specs=[
            pl.BlockSpec((per_step, seq, D_MODEL), tok),
            pl.BlockSpec((1, N_MOD, D_MODEL), lambda b: (mod_row(b * per_step), 0, 0)),
            pl.BlockSpec((1, D_MODEL), const2),
            pl.BlockSpec((D_MODEL, D_PROJ_A), const2),
            pl.BlockSpec((CONV_WIDTH, D_CONV), const2),
            pl.BlockSpec((1, D_CONV), const2),
            pl.BlockSpec((1, D_CONV), const2),
            pl.BlockSpec((1, D_CONV), const2),
        ],
        out_specs=[out_tok, out_tok, out_tok, out_tok],
        out_shape=[
            jax.ShapeDtypeStruct((nreq, seq, D_ATTN), BF16),
            jax.ShapeDtypeStruct((nreq, seq, D_ATTN), kv_dtype),
            jax.ShapeDtypeStruct((nreq, seq, D_ATTN), kv_dtype),
            jax.ShapeDtypeStruct((nreq, seq, D_CONV), BF16),
        ],
        compiler_params=_params("arbitrary"),
        name="front",
    )(x, mods, g1, w_a, dww, dwb, lng, lnb)


def _pair_scores_q(q_pair):
    lane = lax.broadcasted_iota(jnp.int32, q_pair.shape, 1)
    zero = jnp.zeros_like(q_pair)
    return jnp.concatenate([jnp.where(lane < HEAD_DIM, q_pair, zero),
                            jnp.where(lane >= HEAD_DIM, q_pair, zero)], axis=0)


def _pair_merge(o2, t):
    lane = lax.broadcasted_iota(jnp.int32, (t, LANES), 1)
    return jnp.where(lane < HEAD_DIM, o2[:t], o2[t:])


CTX_ATTN_PER_STEP = 4


def _ctx_attn_kernel(q_ref, k_ref, v_ref, o_ref):
    nreq, seq, _ = q_ref.shape
    sls = [slice(p * LANES, (p + 1) * LANES) for p in range(N_HEADS // 2)]
    chains = [(r, p) for r in range(nreq) for p in range(N_HEADS // 2)]
    s = {(r, p): lax.dot_general(_pair_scores_q(q_ref[r, :, sls[p]]), k_ref[r, :, sls[p]].astype(BF16),
                                 _NT, preferred_element_type=F32) for r, p in chains}
    e = {c: jnp.exp2(s[c] - jnp.max(s[c], axis=-1, keepdims=True)) for c in chains}
    den = {c: jnp.sum(e[c], axis=-1, keepdims=True) for c in chains}
    for r, p in chains:
        o2 = _dot(e[r, p].astype(BF16), v_ref[r, :, sls[p]].astype(BF16)) / den[r, p]
        o_ref[r, :, sls[p]] = _pair_merge(o2, seq).astype(o_ref.dtype)


def _ctx_attn(q, k, v):
    nreq, seq, _ = q.shape
    spec = pl.BlockSpec((CTX_ATTN_PER_STEP, seq, D_ATTN), lambda b: (b, 0, 0))
    return pl.pallas_call(
        _ctx_attn_kernel,
        grid=(nreq // CTX_ATTN_PER_STEP,),
        in_specs=[spec, spec, spec],
        out_specs=spec,
        out_shape=jax.ShapeDtypeStruct((nreq, seq, D_ATTN), BF16),
        compiler_params=_params("arbitrary"),
        name="ctx_attn",
    )(q, k, v)


N_DR = 2 * NA_ROWS - 1
N_DC = 2 * NA_COLS - 1


def _bias_kernel(rpb_ref, o_ref, t2_ref):
    h = pl.program_id(0)
    qi = lax.broadcasted_iota(jnp.int32, (GRID_W, LANES), 0)
    lane = lax.broadcasted_iota(jnp.int32, (GRID_W, LANES), 1)
    kc = lane & (GRID_W - 1)
    d = kc - qi
    cs = jnp.clip(qi - NA_COLS // 2, 0, GRID_W - NA_COLS)
    inside = (kc >= cs) & (kc < cs + NA_COLS)
    group = 5
    for dr0 in range(0, N_DR, group):
        ts = [jnp.full((GRID_W, LANES), MASK_NEG, F32)] * group
        for j in range(N_DC):
            hit = d == j - (NA_COLS - 1)
            ts = [jnp.where(hit, rpb_ref[h * (N_DR * N_DC) + (dr0 + g) * N_DC + j] * LOG2_E, ts[g])
                  for g in range(group)]
        for g in range(group):
            t2_ref[dr0 + g] = jnp.where(inside, ts[g], MASK_NEG)
    for o in range(NA_ROWS):
        for jj in range(NA_ROWS // 2):
            o_ref[0, o, :, jj * LANES:(jj + 1) * LANES] = jnp.where(
                lane < GRID_W, t2_ref[o + 2 * jj], t2_ref[o + 2 * jj + 1])


def _bias_table(rpb):
    return pl.pallas_call(
        _bias_kernel,
        grid=(N_HEADS,),
        in_specs=[pl.BlockSpec(memory_space=pltpu.SMEM)],
        out_specs=pl.BlockSpec((1, NA_ROWS, GRID_W, NA_ROWS * GRID_W), lambda h: (h, 0, 0, 0)),
        out_shape=jax.ShapeDtypeStruct((N_HEADS, NA_ROWS, GRID_W, NA_ROWS * GRID_W), F32),
        scratch_shapes=[pltpu.VMEM((N_DR, GRID_W, LANES), F32)],
        compiler_params=_params("arbitrary"),
        name="na_bias",
    )(rpb.reshape(-1))


NA_STEP_ROWS = 4


def _na_attn_kernel(q_ref, k_ref, v_ref, kct_ref, vct_ref, tab_ref, o_ref, kcb_ref, vcb_ref, *, rows):
    r = pl.program_id(1)

    @pl.when(r == 0)
    def _():
        kcb_ref[...] = kct_ref[0].astype(BF16)
        vcb_ref[...] = vct_ref[0].astype(BF16)

    nloc = NA_ROWS * GRID_W
    sls = [slice(p * LANES, (p + 1) * LANES) for p in range(N_HEADS // 2)]
    k0, off = [], []
    for i in range(NA_STEP_ROWS):
        qrow = r * NA_STEP_ROWS + i
        rs = jnp.clip(qrow - NA_ROWS // 2, 0, rows - NA_ROWS)
        k0.append(pl.multiple_of(rs * GRID_W, GRID_W))
        off.append(rs - qrow + (NA_ROWS - 1))
    chains = [(i, p) for i in range(NA_STEP_ROWS) for p in range(N_HEADS // 2)]
    qrows = [slice(i * GRID_W, (i + 1) * GRID_W) for i in range(NA_STEP_ROWS)]
    s_loc, s_ctx = {}, {}
    for i, p in chains:
        q2 = _pair_scores_q(q_ref[0, qrows[i], sls[p]])
        bias = jnp.concatenate([tab_ref[2 * p, off[i]], tab_ref[2 * p + 1, off[i]]], axis=0)
        kl = k_ref[0, pl.ds(k0[i], nloc), sls[p]]
        s_loc[i, p] = lax.dot_general(q2, kl, _NT, preferred_element_type=F32) + bias
        s_ctx[i, p] = _dot(q2, kcb_ref[sls[p], :])
    mx = {c: jnp.maximum(jnp.max(s_loc[c], axis=-1, keepdims=True),
                         jnp.max(s_ctx[c], axis=-1, keepdims=True)) for c in chains}
    e_loc = {c: jnp.exp2(s_loc[c] - mx[c]) for c in chains}
    e_ctx = {c: jnp.exp2(s_ctx[c] - mx[c]) for c in chains}
    den = {c: jnp.sum(e_loc[c], axis=-1, keepdims=True) + jnp.sum(e_ctx[c], axis=-1, keepdims=True)
           for c in chains}
    for i, p in chains:
        vl = v_ref[0, pl.ds(k0[i], nloc), sls[p]]
        o_ctx = lax.dot_general(e_ctx[i, p].astype(BF16), vcb_ref[sls[p], :], _NT,
                                preferred_element_type=F32)
        o2 = (_dot(e_loc[i, p].astype(BF16), vl) + o_ctx) / den[i, p]
        o_ref[0, qrows[i], sls[p]] = _pair_merge(o2, GRID_W).astype(o_ref.dtype)


def _na_attn(q, k, v, kc, vc, tab):
    nreq, seq, _ = q.shape
    rows = seq // GRID_W
    past = kc.shape[2]
    qspec = pl.BlockSpec((1, NA_STEP_ROWS * GRID_W, D_ATTN), lambda b, r: (b, r, 0))
    kvspec = pl.BlockSpec((1, seq, D_ATTN), lambda b, r: (b, 0, 0))
    cspec = pl.BlockSpec((1, D_ATTN, past), lambda b, r: (b, 0, 0))
    return pl.pallas_call(
        functools.partial(_na_attn_kernel, rows=rows),
        grid=(nreq, rows // NA_STEP_ROWS),
        in_specs=[qspec, kvspec, kvspec, cspec, cspec,
                  pl.BlockSpec(tab.shape, lambda b, r: (0, 0, 0, 0))],
        out_specs=qspec,
        out_shape=jax.ShapeDtypeStruct((nreq, seq, D_ATTN), BF16),
        scratch_shapes=[pltpu.VMEM((D_ATTN, past), BF16), pltpu.VMEM((D_ATTN, past), BF16)],
        compiler_params=_params("arbitrary", "arbitrary"),
        name="na_attn",
    )(q, k, v, kc, vc, tab)


MERGE_ROWS = 512
MERGE_SUB = 256


def _merge_kernel(x_ref, mods_ref, g1_ref, g2_ref, c_ref, a_ref, wg_ref, wpw_ref, wao_ref,
                  wout_ref, wr_ref, x1_ref, h2_ref, aff_ref):
    m = mods_ref[0]
    subs = range(MERGE_ROWS // MERGE_SUB)
    rows = [slice(i * MERGE_SUB, (i + 1) * MERGE_SUB) for i in subs]
    x = [x_ref[0, rows[i], :] for i in subs]
    h = [_rms_mod(x[i], g1_ref[...], m[0:1], m[1:2]).astype(BF16) for i in subs]
    gates = [_sigmoid(_dot(h[i], wg_ref[:, D_PROJ_A - D_IN // 2:])) for i in subs]
    conv_out = [_dot(c_ref[0, rows[i], :], wpw_ref[...]) for i in subs]
    attn_out = [_dot(a_ref[0, rows[i], :], wao_ref[...]) for i in subs]
    merged = [(gates[i][:, :D_MODEL] * conv_out[i] + gates[i][:, D_MODEL:] * attn_out[i]).astype(BF16)
              for i in subs]
    x1 = [x[i] + m[2:3] * _dot(merged[i], wout_ref[...]) for i in subs]
    h2 = [_rms_mod(x1[i], g2_ref[...], m[3:4], m[4:5]).astype(BF16) for i in subs]
    logits = [_dot(h2[i], wr_ref[...]) for i in subs]
    e = [jnp.exp(logits[i] - jnp.max(logits[i], axis=-1, keepdims=True)) for i in subs]
    for i in subs:
        x1_ref[0, rows[i], :] = x1[i]
        h2_ref[0, rows[i], :] = h2[i]
        aff_ref[0, rows[i], :] = e[i] / jnp.sum(e[i], axis=-1, keepdims=True)


def _merge(x, mods, mod_row, g1, g2, cact, heads, w_g, w_pw, w_ao, w_out, w_r):
    nreq, seq, _ = x.shape
    tm = MERGE_ROWS
    tok = lambda b, j: (b, j, 0)
    const2 = lambda b, j: (0, 0)
    return pl.pallas_call(
        _merge_kernel,
        grid=(nreq, seq // tm),
        in_specs=[
            pl.BlockSpec((1, tm, D_MODEL), tok),
            pl.BlockSpec((1, N_MOD, D_MODEL), lambda b, j: (mod_row(b), 0, 0)),
            pl.BlockSpec((1, D_MODEL), const2),
            pl.BlockSpec((1, D_MODEL), const2),
            pl.BlockSpec((1, tm, D_CONV), tok),
            pl.BlockSpec((1, tm, D_ATTN), tok),
            pl.BlockSpec((D_MODEL, D_IN // 2), lambda b, j: (0, 1)),
            pl.BlockSpec(w_pw.shape, const2),
            pl.BlockSpec(w_ao.shape, const2),
            pl.BlockSpec(w_out.shape, const2),
            pl.BlockSpec(w_r.shape, const2),
        ],
        out_specs=[
            pl.BlockSpec((1, tm, D_MODEL), tok),
            pl.BlockSpec((1, tm, D_MODEL), tok),
            pl.BlockSpec((1, tm, N_EXPERTS), tok),
        ],
        out_shape=[
            jax.ShapeDtypeStruct((nreq, seq, D_MODEL), F32),
            jax.ShapeDtypeStruct((nreq, seq, D_MODEL), BF16),
            jax.ShapeDtypeStruct((nreq, seq, N_EXPERTS), F32),
        ],
        compiler_params=_params("arbitrary", "arbitrary"),
        name="merge",
    )(x, mods, g1, g2, cact, heads, w_g, w_pw, w_ao, w_out, w_r)


def _route_kernel(aff_ref, slot_t_ref, slot_r_ref, aff_r_ref, *, cap):
    nreq, seq, _ = aff_ref.shape
    ngroup = nreq // REQ_PER_GROUP
    a = jnp.concatenate([aff_ref[b] for b in range(nreq)], axis=1)
    one, zero = jnp.ones_like(a), jnp.zeros_like(a)

    def search(i, thr):
        cand = thr | jnp.left_shift(jnp.int32(1), 30 - i)
        cnt = jnp.sum(jnp.where(a >= pltpu.bitcast(cand, F32), one, zero), axis=0, keepdims=True)
        return jnp.where(cnt >= cap, cand, thr)

    thr = lax.fori_loop(0, 31, search, jnp.zeros((1, ngroup * LANES), jnp.int32))
    gt = a >= pltpu.bitcast(thr + 1, F32)
    eq = (a >= pltpu.bitcast(thr, F32)) & jnp.logical_not(gt)
    need = cap - jnp.sum(jnp.where(gt, one, zero), axis=0, keepdims=True)
    ti = lax.broadcasted_iota(jnp.int32, (seq, seq), 0)
    tj = lax.broadcasted_iota(jnp.int32, (seq, seq), 1)
    before = jnp.where(tj < ti, 1.0, 0.0).astype(BF16)
    eq_rank = _dot(before, jnp.where(eq, one, zero).astype(BF16))
    sel = gt | (eq & (eq_rank < need))
    pos = _dot(before, jnp.where(sel, one, zero).astype(BF16))
    slot = jnp.where(sel, pos, -1.0)
    for g in range(ngroup):
        lanes = slice(g * LANES, (g + 1) * LANES)
        slot_t_ref[g] = slot[:, lanes]
        slot_r_ref[g] = slot[:, lanes].T
        aff_r_ref[g] = a[:, lanes].T


def _route(aff, cap):
    nreq, seq, _ = aff.shape
    ngroup = nreq // REQ_PER_GROUP
    tspec = pl.BlockSpec((ngroup, seq, LANES), lambda g: (0, 0, 0))
    rspec = pl.BlockSpec((ngroup, LANES, seq), lambda g: (0, 0, 0))
    return pl.pallas_call(
        functools.partial(_route_kernel, cap=cap),
        grid=(1,),
        in_specs=[pl.BlockSpec(aff.shape, lambda g: (0, 0, 0))],
        out_specs=[tspec, rspec, rspec],
        out_shape=[
            jax.ShapeDtypeStruct((ngroup, seq, LANES), F32),
            jax.ShapeDtypeStruct((ngroup, LANES, seq), F32),
            jax.ShapeDtypeStruct((ngroup, LANES, seq), F32),
        ],
        compiler_params=_params("arbitrary"),
        name="route",
    )(aff)


GATHER_ROWS = 512


def _gather_kernel(slot_ref, aff_ref, h_ref, x_ref, g_ref, *, cap):
    per_step, seq, _ = h_ref.shape
    nslot = N_EXPERTS * cap
    sub = lax.broadcasted_iota(jnp.int32, (cap, seq), 0).astype(F32)
    one, zero = jnp.ones((cap, seq), F32), jnp.zeros((cap, seq), F32)
    ei = lax.broadcasted_iota(jnp.int32, (N_EXPERTS, nslot), 0)
    si = lax.broadcasted_iota(jnp.int32, (N_EXPERTS, nslot), 1)
    own = (si >= ei * cap) & (si < (ei + 1) * cap)
    per_chunk = GATHER_ROWS // cap
    for r in range(per_step):
        h = h_ref[r]
        a = aff_ref[0, r * N_EXPERTS:(r + 1) * N_EXPERTS, :]
        a_hi = a.astype(BF16)
        r1 = a - a_hi.astype(F32)
        a_mid = r1.astype(BF16)
        a_lo = (r1 - a_mid.astype(F32)).astype(BF16)
        a3 = jnp.concatenate([a_hi, a_mid, a_lo], axis=0)
        g3 = []
        for c in range(nslot // GATHER_ROWS):
            hot = []
            for e in range(c * per_chunk, (c + 1) * per_chunk):
                row = jnp.broadcast_to(slot_ref[0, r * N_EXPERTS + e:r * N_EXPERTS + e + 1, :], (cap, seq))
                hot.append(jnp.where(row == sub, one, zero).astype(BF16))
            hot = jnp.concatenate(hot, axis=0)
            x = _dot(hot, h).astype(x_ref.dtype)
            for i in range(per_chunk):
                x_ref[c * per_chunk + i, r * cap:(r + 1) * cap, :] = x[i * cap:(i + 1) * cap]
            g3.append(lax.dot_general(a3, hot, _NT, preferred_element_type=F32))
        g3 = jnp.concatenate(g3, axis=1)
        gfull = g3[0:N_EXPERTS] + g3[N_EXPERTS:2 * N_EXPERTS] + g3[2 * N_EXPERTS:3 * N_EXPERTS]
        g_ref[r] = jnp.sum(jnp.where(own, gfull, 0.0), axis=0, keepdims=True)


def _gather(slot_r, aff_r, h2, cap, per_step):
    nreq, seq, _ = h2.shape
    nslot = N_EXPERTS * cap
    steps_per_group = REQ_PER_GROUP // per_step
    grp = lambda s: (s // steps_per_group, s % steps_per_group, 0)
    x, g = pl.pallas_call(
        functools.partial(_gather_kernel, cap=cap),
        grid=(nreq // per_step,),
        in_specs=[
            pl.BlockSpec((1, per_step * N_EXPERTS, seq), grp),
            pl.BlockSpec((1, per_step * N_EXPERTS, seq), grp),
            pl.BlockSpec((per_step, seq, D_MODEL), lambda s: (s, 0, 0)),
        ],
        out_specs=[
            pl.BlockSpec((N_EXPERTS, per_step * cap, D_MODEL), lambda s: (0, s, 0)),
            pl.BlockSpec((per_step, 1, nslot), lambda s: (s, 0, 0)),
        ],
        out_shape=[
            jax.ShapeDtypeStruct((N_EXPERTS, nreq * cap, D_MODEL), BF16),
            jax.ShapeDtypeStruct((nreq, 1, nslot), F32),
        ],
        compiler_params=_params("arbitrary"),
        name="gather",
    )(slot_r, aff_r, h2)
    return x, g


EXPERT_FCHUNK = 1024
EXPERT_ROWS = 512
GATE_ROWS = 256


def _expert_kernel(xc_ref, xl_ref, g_ref, wg_ref, wu_ref, wd_ref, yc_ref, yl_ref,
                   acc_ref):
    e = pl.program_id(0)
    f = pl.program_id(1)
    nf = pl.num_programs(1)
    half = xc_ref.shape[1]

    @pl.when((e == 0) & (f == 0))
    def _():
        acc_ref[...] = jnp.zeros_like(acc_ref)

    ri = lax.broadcasted_iota(jnp.int32, (GATE_ROWS, GATE_ROWS), 0)
    ci = lax.broadcasted_iota(jnp.int32, (GATE_ROWS, GATE_ROWS), 1)
    for part, (x_ref, y_ref) in enumerate(((xc_ref, yc_ref), (xl_ref, yl_ref))):
        for i in range(half // EXPERT_ROWS):
            x = x_ref[0, i * EXPERT_ROWS:(i + 1) * EXPERT_ROWS, :]
            gq = _dot(x, wg_ref[0].astype(BF16))
            uq = _dot(x, wu_ref[0].astype(BF16))
            hid = (gq * _sigmoid(gq) * uq).astype(BF16)
            r0 = part * half + i * EXPERT_ROWS
            prev = jnp.where(f == 0, 0.0, acc_ref[r0:r0 + EXPERT_ROWS, :])
            acc = prev + _dot(hid, wd_ref[0].astype(BF16))
            acc_ref[r0:r0 + EXPERT_ROWS, :] = acc
            for j in range(EXPERT_ROWS // GATE_ROWS):
                t = r0 // GATE_ROWS + j
                grow = jnp.broadcast_to(g_ref[0, t:t + 1, :], (GATE_ROWS, GATE_ROWS))
                gcol = jnp.sum(jnp.where(ri == ci, grow, 0.0), axis=-1, keepdims=True)
                y = acc[j * GATE_ROWS:(j + 1) * GATE_ROWS] * gcol
                y_ref[0, i * EXPERT_ROWS + j * GATE_ROWS:i * EXPERT_ROWS + (j + 1) * GATE_ROWS, :] = (
                    y.astype(y_ref.dtype))


def _experts(xc, xl, g, w_gate, w_up, w_down):
    ne, half, _ = xc.shape
    fc = EXPERT_FCHUNK
    xspec = pl.BlockSpec((1, half, D_MODEL), lambda e, f: (e, 0, 0))
    return pl.pallas_call(
        _expert_kernel,
        grid=(ne, D_EXPERT // fc),
        in_specs=[
            xspec, xspec,
            pl.BlockSpec((1,) + g.shape[1:], lambda e, f: (e, 0, 0)),
            pl.BlockSpec((1, D_MODEL, fc), lambda e, f: (e, 0, f)),
            pl.BlockSpec((1, D_MODEL, fc), lambda e, f: (e, 0, f)),
            pl.BlockSpec((1, fc, D_MODEL), lambda e, f: (e, f, 0)),
        ],
        out_specs=[xspec, xspec],
        out_shape=[jax.ShapeDtypeStruct(xc.shape, BF16), jax.ShapeDtypeStruct(xl.shape, BF16)],
        scratch_shapes=[pltpu.VMEM((2 * half, D_MODEL), F32)],
        compiler_params=_params("arbitrary", "arbitrary"),
        name="experts",
    )(xc, xl, g, w_gate, w_up, w_down)


COMBINE_ROWS = 256
COMBINE_LAT_ROWS = 1024
MOE_CTX_PER_STEP = 8


def _combine_kernel(x1_ref, mods_ref, slot_ref, y_ref, gf_ref, o_ref, *, cap):
    per_step, tm, _ = x1_ref.shape
    nslot = N_EXPERTS * cap
    li = lax.broadcasted_iota(jnp.int32, (LANES, nslot), 0)
    si = lax.broadcasted_iota(jnp.int32, (LANES, nslot), 1)
    want = (lax.broadcasted_iota(jnp.int32, (COMBINE_ROWS, nslot), 1) & (cap - 1)).astype(F32)
    chains = [(r, t) for r in range(per_step) for t in range(tm // COMBINE_ROWS)]
    rows = {t: slice(t * COMBINE_ROWS, (t + 1) * COMBINE_ROWS) for _, t in chains}
    expand = {}
    for r in range(per_step):
        lane0 = ((pl.program_id(0) * per_step + r) % REQ_PER_GROUP) * N_EXPERTS
        expand[r] = jnp.where(li == lane0 + si // cap, 1.0, 0.0).astype(BF16)
    slot_exp = {(r, t): _dot(slot_ref[0, rows[t], :].astype(BF16), expand[r]) for r, t in chains}
    scat = {c: jnp.where(slot_exp[c] == want, 1.0, 0.0).astype(BF16) for c in chains}
    moe = {(r, t): _dot(scat[r, t], y_ref[:, r * cap:(r + 1) * cap, :].reshape(nslot, D_MODEL))
           for r, t in chains}
    for r, t in chains:
        x2 = x1_ref[r, rows[t], :] + mods_ref[0][5:6] * moe[r, t]
        o_ref[r, rows[t], :] = (x2 * lax.rsqrt(jnp.mean(x2 * x2, axis=-1, keepdims=True) + EPS)
                                * gf_ref[...])


def _combine(x1, mods, mod_row, slot_t, y, gf, cap, per_step, tm):
    nreq, seq, _ = x1.shape
    assert per_step == 1 or seq == tm
    tok = lambda s, j: (s, j, 0)
    return pl.pallas_call(
        functools.partial(_combine_kernel, cap=cap),
        grid=(nreq // per_step, seq // tm),
        in_specs=[
            pl.BlockSpec((per_step, tm, D_MODEL), tok),
            pl.BlockSpec((1, N_MOD, D_MODEL), lambda s, j: (mod_row(s * per_step), 0, 0)),
            pl.BlockSpec((1, tm, LANES), lambda s, j: (s * per_step // REQ_PER_GROUP, j, 0)),
            pl.BlockSpec((N_EXPERTS, per_step * cap, D_MODEL), lambda s, j: (0, s, 0)),
            pl.BlockSpec((1, D_MODEL), lambda s, j: (0, 0)),
        ],
        out_specs=pl.BlockSpec((per_step, tm, D_MODEL), tok),
        out_shape=jax.ShapeDtypeStruct(x1.shape, F32),
        compiler_params=_params("arbitrary", "arbitrary"),
        name="combine",
    )(x1, mods, slot_t, y, gf)


def kernel(x_prompt, x_sample, cache_ctx_k, cache_ctx_v, c, c_ctx, w_ada, b_ada, norm1_g, w_in,
           conv_dw_w, conv_dw_b, conv_ln_g, conv_ln_b, w_conv_pw, na_rpb, w_attn_o, w_out,
           norm2_g, w_router, w_gate_e, w_up_e, w_down_e, final_g):
    assert w_ada.shape[0] == 1, "single trunk layer"
    nctx, ctx_seq, _ = x_prompt.shape
    nlat, lat_seq, _ = x_sample.shape
    ctx_cap = EC_CAPACITY * ctx_seq // N_EXPERTS
    lat_cap = EC_CAPACITY * lat_seq // N_EXPERTS
    assert nctx * ctx_cap == nlat * lat_cap

    cond = jnp.zeros((2 * nlat, D_MODEL), F32).at[:nlat].set(c).at[nlat].set(c_ctx)
    mods = _ada(cond, w_ada[0], b_ada).reshape(2 * nlat, N_MOD, D_MODEL)
    ctx_row = lambda b: nlat
    lat_row = lambda b: b

    w_a = w_g = w_in[0].astype(BF16)
    w_pw = w_conv_pw[0].astype(BF16)
    w_ao = w_attn_o[0].astype(BF16)
    w_o = w_out[0].astype(BF16)
    w_r = w_router[0].astype(BF16)
    conv_args = (conv_dw_w[0], conv_dw_b, conv_ln_g, conv_ln_b)

    q_c, k_c, v_c, cact_c = _front(x_prompt, mods, ctx_row, norm1_g, w_a, *conv_args, F32,
                                   FRONT_CTX_PER_STEP)
    q_l, k_l, v_l, cact_l = _front(x_sample, mods, lat_row, norm1_g, w_a, *conv_args, BF16, 1)

    heads_c = _ctx_attn(q_c, k_c, v_c)
    tab = _bias_table(na_rpb[0])
    past = cache_ctx_k.shape[2]
    feat_major = lambda t: jnp.transpose(t[:, 0], (0, 2, 3, 1)).reshape(nlat, D_ATTN, past)
    heads_l = _na_attn(q_l, k_l, v_l, feat_major(cache_ctx_k), feat_major(cache_ctx_v), tab)

    merge_w = (w_g, w_pw, w_ao, w_o, w_r)
    flat = lambda t: t.reshape(1, nctx * ctx_seq, t.shape[-1])
    x1_c, h2_c, aff_c = _merge(flat(x_prompt), mods, ctx_row, norm1_g, norm2_g, flat(cact_c),
                               flat(heads_c), *merge_w)
    x1_c, h2_c, aff_c = (t.reshape(nctx, ctx_seq, t.shape[-1]) for t in (x1_c, h2_c, aff_c))
    x1_l, h2_l, aff_l = _merge(x_sample, mods, lat_row, norm1_g, norm2_g, cact_l, heads_l, *merge_w)

    slot_t_c, slot_r_c, aff_r_c = _route(aff_c, ctx_cap)
    slot_t_l, slot_r_l, aff_r_l = _route(aff_l, lat_cap)

    xg_c, g_c = _gather(slot_r_c, aff_r_c, h2_c, ctx_cap, MOE_CTX_PER_STEP)
    xg_l, g_l = _gather(slot_r_l, aff_r_l, h2_l, lat_cap, 1)

    def gate_rows(g, nreq, cap):
        g = jnp.transpose(g.reshape(nreq, N_EXPERTS, cap), (1, 0, 2))
        return g.reshape(N_EXPERTS, nreq * cap // GATE_ROWS, GATE_ROWS)

    gates = jnp.concatenate([gate_rows(g_c, nctx, ctx_cap), gate_rows(g_l, nlat, lat_cap)], axis=1)
    y_c, y_l = _experts(xg_c, xg_l, gates, w_gate_e[0], w_up_e[0], w_down_e[0])

    gf = final_g.reshape(1, D_MODEL)
    y_prompt = _combine(x1_c, mods, ctx_row, slot_t_c, y_c, gf, ctx_cap, MOE_CTX_PER_STEP, ctx_seq)
    y_sample = _combine(x1_l, mods, lat_row, slot_t_l, y_l, gf, lat_cap, 1, COMBINE_LAT_ROWS)

    state_k = k_c.reshape(nctx, 1, ctx_seq, N_HEADS, HEAD_DIM)
    state_v = v_c.reshape(nctx, 1, ctx_seq, N_HEADS, HEAD_DIM)
    return (y_prompt, y_sample, state_k, state_v)
```
